```python
import math
import jax
import jax.numpy as jnp
from jax import lax
import numpy as np

D_MODEL = 1024
BATCH = 8
SEQ = 2048
DEPTH = 2

CTX_LEN = 256
GRID_W = 64

DA_HEADS = 4
DA_DH = 64
DA_DV = 128
ML_HEADS = 4
ML_DH = 64
ML_CONV = 3
ML_CHUNK = 64
GL_HEADS = 4
GL_DK = 32
GL_DV = 64
GL_RANK = 16
GL_NORMALIZER = 16.0
GL_CHUNK = 64
Q_BLOCK = 128
ROPE_BASE = 10000.0
MOE_GROUPS = 4
MOE_PER_GROUP = 8
MOE_EXPERTS = MOE_GROUPS * MOE_PER_GROUP
MOE_TOP_K = 2
MOE_HIDDEN = 512
MOE_BLOCK = 128
ADA_STD = 0.5
DN_ALPHA = (2 * DEPTH) ** 0.25
DN_BETA = (8 * DEPTH) ** -0.25
LN_EPS = 1e-6

DA_W = DA_HEADS * DA_DV
ML_W = ML_HEADS * ML_DH
GL_W = GL_HEADS * GL_DV
MIX_W = DA_W + ML_W + GL_W
IN_SIZES = (
    DA_HEADS * 2 * DA_DH,
    DA_HEADS * 2 * DA_DH,
    DA_W,
    2 * ML_W,
    ML_W,
    ML_W,
    2 * ML_HEADS,
    2 * ML_HEADS,
    GL_HEADS * GL_DK,
    GL_HEADS * GL_DK,
    GL_W,
    GL_W,
    2 * GL_RANK,
)
IN_W = sum(IN_SIZES)
IN_OFFSETS = tuple(int(o) for o in np.cumsum(IN_SIZES)[:-1])

F32 = jnp.float32

kernel_name = 'hybrid_diffattn_mlstm_gla_hmoe'


def _ln(x):
    xf = x.astype(F32)
    mu = jnp.mean(xf, -1, keepdims=True)
    var = jnp.mean(jnp.square(xf - mu), -1, keepdims=True)
    return (xf - mu) * lax.rsqrt(var + LN_EPS)


def modulate(x, shift, scale):
    return (_ln(x) * (1.0 + scale) + shift).astype(x.dtype)


def post_norm(x, g, b):
    return (_ln(x) * g + b).astype(x.dtype)


def head_rms(h, g, dtype):
    hf = h.astype(F32)
    y = hf * lax.rsqrt(jnp.mean(jnp.square(hf), -1, keepdims=True) + LN_EPS) * g
    return y.reshape(*h.shape[:-2], -1).astype(dtype)


def rope_2d(n_rows, dim):
    n_freq = dim // 4
    inv_freq = ROPE_BASE ** (-jnp.arange(n_freq, dtype=F32) / n_freq)
    row = jnp.repeat(jnp.arange(n_rows, dtype=F32), GRID_W)
    col = jnp.tile(jnp.arange(GRID_W, dtype=F32), n_rows)
    ang_r = row[:, None] * inv_freq
    ang_c = col[:, None] * inv_freq
    ang = jnp.concatenate([ang_r, ang_r, ang_c, ang_c], axis=-1)
    return jnp.cos(ang), jnp.sin(ang)


def apply_rope(x, cos, sin):
    x1, x2, x3, x4 = jnp.split(x, 4, axis=-1)
    rot = jnp.concatenate([-x2, x1, -x4, x3], axis=-1)
    c, s = cos[:, None, None, :], sin[:, None, None, :]
    return (x.astype(F32) * c + rot.astype(F32) * s).astype(x.dtype)


def short_conv(x, w, b):
    y = lax.conv_general_dilated(x, w[:, None, :].astype(x.dtype), window_strides=(1,), padding='SAME',
                                 dimension_numbers=('NWC', 'WIO', 'NWC'), feature_group_count=x.shape[-1])
    return y + b.astype(x.dtype)


def diff_attend(q, k, v, lam):
    s = jnp.einsum('bqhmd,bkhmd->bhmqk', q, k, preferred_element_type=F32) * DA_DH ** -0.5
    p = jax.nn.softmax(s, axis=-1)
    a = p[:, :, 0] - lam * p[:, :, 1]
    return jnp.einsum('bhqk,bkhd->bqhd', a.astype(v.dtype), v)


def diff_attention(lat, ctx, lam_vecs, norm_g, lam_init, cos, sin, ctx_out):
    (q_l, k_l, v_l), (q_c, k_c, v_c) = lat, ctx
    B, S, _ = q_l.shape
    qk_heads = lambda t: t.reshape(t.shape[0], t.shape[1], DA_HEADS, 2, DA_DH)
    v_heads = lambda t: t.reshape(t.shape[0], t.shape[1], DA_HEADS, DA_DV)
    q_l = apply_rope(qk_heads(q_l), cos, sin)
    k_l = apply_rope(qk_heads(k_l), cos, sin)
    q_c, k_c, v_c, v_l = qk_heads(q_c), qk_heads(k_c), v_heads(v_c), v_heads(v_l)
    lq1, lk1, lq2, lk2 = lam_vecs.astype(F32)
    lam = jnp.exp(jnp.dot(lq1, lk1)) - jnp.exp(jnp.dot(lq2, lk2)) + lam_init
    k_all = jnp.concatenate([k_c, k_l], axis=1)
    v_all = jnp.concatenate([v_c, v_l], axis=1)
    n_blocks = S // Q_BLOCK
    q_blocks = jnp.moveaxis(q_l.reshape(B, n_blocks, Q_BLOCK, DA_HEADS, 2, DA_DH), 1, 0)
    o_l = lax.map(lambda qb: diff_attend(qb, k_all, v_all, lam), q_blocks)
    o_l = jnp.moveaxis(o_l, 0, 1).reshape(B, S, DA_HEADS, DA_DV)
    out_l = head_rms(o_l, norm_g, q_l.dtype) * (1.0 - lam_init)
    out_c = head_rms(diff_attend(q_c, k_c, v_c, lam), norm_g, q_c.dtype) * (1.0 - lam_init) if ctx_out else None
    return out_l, out_c


def _chunks(t, L):
    B, H, N = t.shape[:3]
    return jnp.moveaxis(t.reshape(B, H, N // L, L, *t.shape[3:]), 2, 0)


def _unchunk(t):
    t = jnp.moveaxis(t, 0, 2)
    return t.reshape(t.shape[0], t.shape[1], -1, *t.shape[4:])


def mlstm_scan(q, k, v, ig, lf, state):
    L = ML_CHUNK
    causal = jnp.tril(jnp.ones((L, L), dtype=bool))

    def step(carry, xs):
        C, n, m = carry
        qc, kc, vc, ic, fc = xs
        b = jnp.cumsum(fc, axis=-1)
        log_d = jnp.where(causal, b[..., :, None] - b[..., None, :] + ic[..., None, :], -jnp.inf)
        log_a = b + m[..., None]
        m_t = jnp.maximum(log_a, log_d.max(-1))
        d = jnp.exp(log_d - m_t[..., None])
        a = jnp.exp(log_a - m_t)
        s = jnp.einsum('bhld,bhsd->bhls', qc, kc) * d
        num = a[..., None] * jnp.einsum('bhld,bhde->bhle', qc, C) + jnp.einsum('bhls,bhse->bhle', s, vc)
        qn = a * jnp.einsum('bhld,bhd->bhl', qc, n) + s.sum(-1)
        h = num / jnp.maximum(jnp.abs(qn), jnp.exp(-m_t))[..., None]
        m_new = m_t[..., -1]
        w = jnp.exp(b[..., -1:] - b + ic - m_new[..., None])
        carry_decay = jnp.exp(b[..., -1] + m - m_new)
        C = carry_decay[..., None, None] * C + jnp.einsum('bhs,bhsd,bhse->bhde', w, kc, vc)
        n = carry_decay[..., None] * n + jnp.einsum('bhs,bhsd->bhd', w, kc)
        return (C, n, m_new), h

    state, h = lax.scan(step, state, tuple(_chunks(t, L) for t in (q, k, v, ig, lf)))
    return _unchunk(h), state


def gla_scan(q, k, v, la, state):
    L = GL_CHUNK
    causal = jnp.tril(jnp.ones((L, L), dtype=bool))[..., None]

    def step(S, xs):
        qc, kc, vc, lc = xs
        b = jnp.cumsum(lc, axis=-2)
        inter = jnp.einsum('bhld,bhde->bhle', qc * jnp.exp(b), S)
        decay = jnp.exp(jnp.where(causal, b[..., :, None, :] - b[..., None, :, :], -jnp.inf))
        att = jnp.einsum('bhld,bhlsd,bhsd->bhls', qc, decay, kc)
        h = inter + jnp.einsum('bhls,bhse->bhle', att, vc)
        S = jnp.exp(b[..., -1, :])[..., None] * S + jnp.einsum('bhsd,bhse->bhde', kc * jnp.exp(b[..., -1:, :] - b), vc)
        return S, h

    state, h = lax.scan(step, state, tuple(_chunks(t, L) for t in (q, k, v, la)))
    return _unchunk(h), state


def bidirectional_scan(scan_fn, init, shared_c, gates_c, shared_l, gates_l):
    outs_c, outs_l = [], []
    for d in range(2):
        f = (lambda t: jnp.flip(t, axis=2)) if d else (lambda t: t)
        h_c, state = scan_fn(*[f(t) for t in shared_c], *[f(g[d]) for g in gates_c], init)
        h_l, _ = scan_fn(*[f(t) for t in shared_l], *[f(g[d]) for g in gates_l], state)
        outs_c.append(f(h_c))
        outs_l.append(f(h_l))
    return outs_c[0] + outs_c[1], outs_l[0] + outs_l[1]


def mlstm_mixer(lat, ctx, conv_w, conv_b, ib, fb, norm_g, ctx_out):
    def prep(qk, v, ig, fg):
        B, N, _ = v.shape
        q, k = jnp.split(jax.nn.silu(short_conv(qk, conv_w, conv_b)), 2, axis=-1)
        heads = lambda t: t.reshape(B, N, ML_HEADS, ML_DH).transpose(0, 2, 1, 3).astype(F32)
        gates = lambda g, bias: (g.reshape(B, N, 2, ML_HEADS).astype(F32) + bias).transpose(2, 0, 3, 1)
        return (heads(q) * ML_DH ** -0.5, heads(k), heads(v)), (gates(ig, ib), jax.nn.log_sigmoid(gates(fg, fb)))

    qk_l, v_l, o_l, i_l, f_l = lat
    qk_c, v_c, o_c, i_c, f_c = ctx
    shared_l, gates_l = prep(qk_l, v_l, i_l, f_l)
    shared_c, gates_c = prep(qk_c, v_c, i_c, f_c)
    B = v_l.shape[0]
    init = (jnp.zeros((B, ML_HEADS, ML_DH, ML_DH), F32), jnp.zeros((B, ML_HEADS, ML_DH), F32),
            jnp.zeros((B, ML_HEADS), F32))
    h_c, h_l = bidirectional_scan(mlstm_scan, init, shared_c, gates_c, shared_l, gates_l)
    finish = lambda h, o: head_rms(h.transpose(0, 2, 1, 3), norm_g, o.dtype) * jax.nn.sigmoid(o)
    return finish(h_l, o_l), (finish(h_c, o_c) if ctx_out else None)


def gla_mixer(lat, ctx, wa, ba, norm_g, ctx_out):
    def prep(q, k, v, a):
        B, N, _ = v.shape
        heads = lambda t, dh: t.reshape(B, N, GL_HEADS, dh).transpose(0, 2, 1, 3).astype(F32)
        z = jnp.einsum('bnrj,rjc->rbnc', a.reshape(B, N, 2, GL_RANK), wa) + ba[:, None, None, :]
        log_a = jax.nn.log_sigmoid(z.astype(F32)) / GL_NORMALIZER
        log_a = log_a.reshape(2, B, N, GL_HEADS, GL_DK).transpose(0, 1, 3, 2, 4)
        return (heads(q, GL_DK) * GL_DK ** -0.5, heads(k, GL_DK), heads(v, GL_DV)), (log_a,)

    q_l, k_l, v_l, r_l, a_l = lat
    q_c, k_c, v_c, r_c, a_c = ctx
    shared_l, gates_l = prep(q_l, k_l, v_l, a_l)
    shared_c, gates_c = prep(q_c, k_c, v_c, a_c)
    init = jnp.zeros((v_l.shape[0], GL_HEADS, GL_DK, GL_DV), F32)
    h_c, h_l = bidirectional_scan(gla_scan, init, shared_c, gates_c, shared_l, gates_l)
    finish = lambda h, r: head_rms(h.transpose(0, 2, 1, 3), norm_g, r.dtype) * jax.nn.silu(r)
    return finish(h_l, r_l), (finish(h_c, r_c) if ctx_out else None)


def token_mixers(h_l, h_c, p, lam_init, cos, sin, ctx_out):
    pl = jnp.split(h_l @ p['w_in'], IN_OFFSETS, axis=-1)
    pc = jnp.split(h_c @ p['w_in'], IN_OFFSETS, axis=-1)
    da_l, da_c = diff_attention(pl[0:3], pc[0:3], p['da_lambda'], p['da_norm'], lam_init, cos, sin, ctx_out)
    ml_l, ml_c = mlstm_mixer(pl[3:8], pc[3:8], p['ml_conv_w'], p['ml_conv_b'], p['ml_ib'], p['ml_fb'],
                             p['ml_norm'], ctx_out)
    gl_l, gl_c = gla_mixer(pl[8:13], pc[8:13], p['gl_wa'], p['gl_ba'], p['gl_norm'], ctx_out)
    y_l = jnp.concatenate([da_l, ml_l, gl_l], axis=-1) @ p['w_out']
    y_c = jnp.concatenate([da_c, ml_c, gl_c], axis=-1) @ p['w_out'] if ctx_out else None
    return y_l, y_c


def hier_moe(t, p):
    T, D = t.shape
    g_prob = jax.nn.softmax((t @ p['moe_wg']).astype(F32), axis=-1)
    g_gate, g_idx = lax.top_k(g_prob, 1)
    e_logits = (t @ p['moe_we']).astype(F32).reshape(T, MOE_GROUPS, MOE_PER_GROUP)
    e_logits = jnp.take_along_axis(e_logits, g_idx[:, :, None], axis=1)[:, 0]
    e_gate, e_idx = lax.top_k(jax.nn.softmax(e_logits, axis=-1), MOE_TOP_K)
    weights = g_gate * e_gate / jnp.sum(e_gate, -1, keepdims=True)
    expert = g_idx * MOE_PER_GROUP + e_idx
    A = T * MOE_TOP_K
    n_blocks = -(-(A + MOE_EXPERTS * (MOE_BLOCK - 1)) // MOE_BLOCK)
    flat_e = expert.reshape(-1)
    flat_tok = jnp.repeat(jnp.arange(T, dtype=jnp.int32), MOE_TOP_K)
    order = jnp.argsort(flat_e)
    sorted_e = flat_e[order]
    counts = jnp.bincount(flat_e, length=MOE_EXPERTS)
    padded = (counts + MOE_BLOCK - 1) // MOE_BLOCK * MOE_BLOCK
    start = jnp.cumsum(counts) - counts
    pad_end = jnp.cumsum(padded)
    pad_start = pad_end - padded
    dest = pad_start[sorted_e] + jnp.arange(A, dtype=jnp.int32) - start[sorted_e]
    slot_tok = jnp.full((n_blocks * MOE_BLOCK,), T, jnp.int32).at[dest].set(flat_tok[order])
    slot_w = jnp.zeros((n_blocks * MOE_BLOCK,), F32).at[dest].set(weights.reshape(-1)[order])
    block_e = jnp.minimum(jnp.searchsorted(pad_end, jnp.arange(n_blocks, dtype=jnp.int32) * MOE_BLOCK, side='right'),
                          MOE_EXPERTS - 1)
    x_slots = jnp.concatenate([t, jnp.zeros((1, D), t.dtype)], axis=0)[slot_tok].reshape(n_blocks, MOE_BLOCK, D)

    def expert_block(args):
        xb, e = args
        return (jax.nn.silu(xb @ p['moe_w1'][e]) * (xb @ p['moe_w3'][e])) @ p['moe_w2'][e]

    y_slots = lax.map(expert_block, (x_slots, block_e)).reshape(-1, D)
    y = jax.ops.segment_sum(y_slots.astype(F32) * slot_w[:, None], slot_tok, num_segments=T + 1)[:T]
    return y.astype(t.dtype)


def hybrid_layer(x_l, x_c, c, c_ctx, p, lam_init, cos, sin, ctx_out):
    B, S, D = x_l.shape
    mod_l = jax.nn.silu(c) @ p['w_ada'] + p['b_ada']
    mod_c = jax.nn.silu(c_ctx) @ p['w_ada'] + p['b_ada']
    sh1_l, sc1_l, g1_l, sh2_l, sc2_l, g2_l = jnp.split(mod_l[:, None, :], 6, axis=-1)
    sh1_c, sc1_c, g1_c, sh2_c, sc2_c, g2_c = jnp.split(mod_c, 6, axis=-1)

    y_l, y_c = token_mixers(modulate(x_l, sh1_l, sc1_l), modulate(x_c, sh1_c, sc1_c), p, lam_init, cos, sin, ctx_out)
    x_l = post_norm(DN_ALPHA * x_l + g1_l * y_l, p['ln_mix_g'], p['ln_mix_b'])
    f_l = modulate(x_l, sh2_l, sc2_l).reshape(B * S, D)
    if ctx_out:
        x_c = post_norm(DN_ALPHA * x_c + g1_c * y_c, p['ln_mix_g'], p['ln_mix_b'])
        f_c = modulate(x_c, sh2_c, sc2_c).reshape(-1, D)
        y = hier_moe(jnp.concatenate([f_l, f_c], axis=0), p)
        y_l, y_c = y[:B * S], y[B * S:]
        x_c = post_norm(DN_ALPHA * x_c + g2_c * y_c.reshape(x_c.shape), p['ln_ffn_g'], p['ln_ffn_b'])
    else:
        y_l = hier_moe(f_l, p)
    x_l = post_norm(DN_ALPHA * x_l + g2_l * y_l.reshape(B, S, D), p['ln_ffn_g'], p['ln_ffn_b'])
    return x_l, x_c


def setup_inputs(seed: int = 0) -> dict:
    key = jax.random.key(seed)
    ks = iter(jax.random.split(key, 32))
    nrm = lambda shape, std: std * jax.random.normal(next(ks), shape, F32)
    D = D_MODEL
    return {
        'x': nrm((BATCH, SEQ, D), 1.0),
        'c': nrm((BATCH, D), 1.0),
        'ctx': nrm((BATCH, CTX_LEN, D), 1.0),
        'c_ctx': nrm((D,), 1.0),
        'w_ada': nrm((DEPTH, D, 6 * D), ADA_STD * D ** -0.5),
        'b_ada': nrm((DEPTH, 6 * D), 0.02),
        'w_in': nrm((DEPTH, D, IN_W), D ** -0.5),
        'da_lambda': nrm((DEPTH, 4, DA_DH), 0.1),
        'da_norm': 1.0 + nrm((DEPTH, DA_DV), 0.05),
        'ml_conv_w': nrm((DEPTH, ML_CONV, 2 * ML_W), ML_CONV ** -0.5),
        'ml_conv_b': nrm((DEPTH, 2 * ML_W), 0.02),
        'ml_ib': nrm((DEPTH, 2, ML_HEADS), 0.1),
        'ml_fb': jnp.linspace(3.0, 6.0, ML_HEADS, dtype=F32) + nrm((DEPTH, 2, ML_HEADS), 0.1),
        'ml_norm': 1.0 + nrm((DEPTH, ML_DH), 0.05),
        'gl_wa': nrm((DEPTH, 2, GL_RANK, GL_HEADS * GL_DK), GL_RANK ** -0.5),
        'gl_ba': nrm((DEPTH, 2, GL_HEADS * GL_DK), 0.1),
        'gl_norm': 1.0 + nrm((DEPTH, GL_DV), 0.05),
        'w_out': nrm((DEPTH, MIX_W, D), DN_BETA * MIX_W ** -0.5),
        'ln_mix_g': 1.0 + nrm((DEPTH, D), 0.05),
        'ln_mix_b': nrm((DEPTH, D), 0.02),
        'ln_ffn_g': 1.0 + nrm((DEPTH, D), 0.05),
        'ln_ffn_b': nrm((DEPTH, D), 0.02),
        'moe_wg': nrm((DEPTH, D, MOE_GROUPS), D ** -0.5),
        'moe_we': nrm((DEPTH, D, MOE_EXPERTS), D ** -0.5),
        'moe_w1': nrm((DEPTH, MOE_EXPERTS, D, MOE_HIDDEN), D ** -0.5),
        'moe_w3': nrm((DEPTH, MOE_EXPERTS, D, MOE_HIDDEN), D ** -0.5),
        'moe_w2': nrm((DEPTH, MOE_EXPERTS, MOE_HIDDEN, D), DN_BETA * MOE_HIDDEN ** -0.5),
    }


def reference(x, c, ctx, c_ctx, w_ada, b_ada, w_in, da_lambda, da_norm, ml_conv_w, ml_conv_b, ml_ib, ml_fb,
              ml_norm, gl_wa, gl_ba, gl_norm, w_out, ln_mix_g, ln_mix_b, ln_ffn_g, ln_ffn_b,
              moe_wg, moe_we, moe_w1, moe_w3, moe_w2):
    n_latent = x.shape[1]
    rows = n_latent // GRID_W
    cos, sin = rope_2d(rows, DA_DH)
    x_l, x_c = x, ctx
    for i in range(DEPTH):
        p = dict(w_ada=w_ada[i], b_ada=b_ada[i], w_in=w_in[i], da_lambda=da_lambda[i], da_norm=da_norm[i],
                 ml_conv_w=ml_conv_w[i], ml_conv_b=ml_conv_b[i], ml_ib=ml_ib[i], ml_fb=ml_fb[i],
                 ml_norm=ml_norm[i], gl_wa=gl_wa[i], gl_ba=gl_ba[i], gl_norm=gl_norm[i], w_out=w_out[i],
                 ln_mix_g=ln_mix_g[i], ln_mix_b=ln_mix_b[i], ln_ffn_g=ln_ffn_g[i], ln_ffn_b=ln_ffn_b[i],
                 moe_wg=moe_wg[i], moe_we=moe_we[i], moe_w1=moe_w1[i], moe_w3=moe_w3[i], moe_w2=moe_w2[i])
        lam_init = 0.8 - 0.6 * math.exp(-0.3 * i)
        x_l, x_c = hybrid_layer(x_l, x_c, c, c_ctx, p, lam_init, cos, sin, ctx_out=(i < DEPTH - 1))
    return x_l
```

```python
import functools
import math

import numpy as np
import jax
import jax.numpy as jnp
from jax import lax
from jax.experimental import pallas as pl
from jax.experimental.pallas import tpu as pltpu

F32 = jnp.float32
BF16 = jnp.bfloat16
HI = lax.Precision.HIGHEST

GRID_W = 64
HEADS = 4
DA_DH = 64
ML_DH = 64
GL_DK = 32
GL_DV = 64
GL_RANK = 16
GL_NORMALIZER = 16.0
CHUNK = 64
SUB = 16
ROPE_BASE = 10000.0
MOE_GROUPS = 4
MOE_PER_GROUP = 8
MOE_EXPERTS = 32
MOE_HIDDEN = 512
LN_EPS = 1e-6

LANES = 128
SUBLANES = 8
VMEM_LIMIT = 48 * 1024 * 1024

TM = 256
MOE_BLOCK = 256
PAD_W = HEADS * LANES

C_QA, C_KA, C_VA, C_MQK, C_MV, C_MO = 0, 512, 1024, 1536, 2048, 2304
C_GQ, C_GK, C_GV, C_GR, C_MISC, IN_WP = 2560, 2688, 2816, 3072, 3328, 3456
M_IG, M_FG, M_GA = 0, 8, 16


def _cparams(sem):
    return pltpu.CompilerParams(dimension_semantics=sem, vmem_limit_bytes=VMEM_LIMIT)


def _sigmoid(x):
    return 1.0 / (1.0 + jnp.exp(-x))


def _silu(x):
    return x * _sigmoid(x)


def _log_sigmoid(x):
    return jnp.minimum(x, 0.0) - jnp.log(1.0 + jnp.exp(-jnp.abs(x)))


def _ln(x):
    mu = jnp.mean(x, axis=-1, keepdims=True)
    xc = x - mu
    var = jnp.mean(xc * xc, axis=-1, keepdims=True)
    return xc * lax.rsqrt(var + LN_EPS)


def _dot(a, b):
    return jnp.dot(a, b, preferred_element_type=F32)


def _dot_hi(a, b):
    return jnp.dot(a, b, precision=HI, preferred_element_type=F32)


def _dot_nt(a, b):
    return lax.dot_general(a, b, (((1,), (1,)), ((), ())), preferred_element_type=F32)


def _dot_tn(a, b):
    return lax.dot_general(a, b, (((0,), (0,)), ((), ())), preferred_element_type=F32)


def _ada_kernel(c_ref, w_ref, b_ref, o_ref):
    o_ref[...] = _dot_hi(_silu(c_ref[...]), w_ref[...]) + b_ref[...]


def _ada(cc, w_ada, b_ada):
    depth, d, n6 = w_ada.shape
    tn = 1536
    rows = cc.shape[0]
    return pl.pallas_call(
        _ada_kernel,
        grid=(depth, n6 // tn),
        in_specs=[
            pl.BlockSpec((rows, d), lambda l, j: (0, 0)),
            pl.BlockSpec((None, d, tn), lambda l, j: (l, 0, j)),
            pl.BlockSpec((None, 1, tn), lambda l, j: (l, 0, j)),
        ],
        out_specs=pl.BlockSpec((None, rows, tn), lambda l, j: (l, 0, j)),
        out_shape=jax.ShapeDtypeStruct((depth, rows, n6), F32),
        compiler_params=_cparams(("arbitrary", "arbitrary")),
        name="ada_mod",
    )(cc, w_ada, b_ada.reshape(depth, 1, n6))


def _in_kernel(n_ctx_tiles, x_ref, xp_ref, xn_ref, mod_ref, w_ref, rope_ref, cw_ref, cb_ref,
               e64_ref, e32_ref,
               qa_ref, ka_ref, va_ref, mq_ref, mk_ref, mv_ref, mo_ref,
               gq_ref, gk_ref, gv_ref, gr_ref, misc_ref, pext_ref):
    i = pl.program_id(1)
    n_tiles = pl.num_programs(1)
    tm = x_ref.shape[0]
    shift, scale = mod_ref[0:1, :], mod_ref[1:2, :]

    def prenorm(x):
        return (_ln(x) * (1.0 + scale) + shift).astype(BF16)

    h = prenorm(x_ref[...])

    def proj(a, lo, hi):
        return _dot(a, w_ref[:, lo:hi])

    cos, sin_a, sin_b = rope_ref[0], rope_ref[1], rope_ref[2]
    for lo, out_ref, mult in ((C_QA, qa_ref, DA_DH ** -0.5), (C_KA, ka_ref, 1.0)):
        acc = proj(h, lo, lo + PAD_W)
        for hh in range(HEADS):
            xh = acc[:, hh * LANES:(hh + 1) * LANES]
            r = xh * cos + pltpu.roll(xh, LANES - 16, 1) * sin_a + pltpu.roll(xh, 16, 1) * sin_b
            out_ref[:, hh * LANES:(hh + 1) * LANES] = (r * mult).astype(BF16)
    va_ref[...] = proj(h, C_VA, C_VA + PAD_W).astype(BF16)

    seg_first = jnp.logical_or(i == 0, i == n_ctx_tiles)
    seg_last = jnp.logical_or(i == n_ctx_tiles - 1, i == n_tiles - 1)
    keep_prev = jnp.where(seg_first, 0.0, 1.0)
    keep_next = jnp.where(seg_last, 0.0, 1.0)
    p = proj(h, C_MQK, C_MQK + 512)
    pext_ref[0:SUBLANES, :] = proj(prenorm(xp_ref[...]), C_MQK, C_MQK + 512) * keep_prev
    pext_ref[SUBLANES:SUBLANES + tm, :] = p
    pext_ref[SUBLANES + tm:2 * SUBLANES + tm, :] = proj(prenorm(xn_ref[...]), C_MQK, C_MQK + 512) * keep_next
    y = (cw_ref[0:1, :] * pext_ref[SUBLANES - 1:SUBLANES - 1 + tm, :] + cw_ref[1:2, :] * p
         + cw_ref[2:3, :] * pext_ref[SUBLANES + 1:SUBLANES + 1 + tm, :] + cb_ref[...])
    y = _silu(y)
    e64 = e64_ref[...]
    e32 = e32_ref[...]
    mq_ref[...] = _dot((y[:, :256] * ML_DH ** -0.5).astype(BF16), e64).astype(BF16)
    mk_ref[...] = _dot(y[:, 256:].astype(BF16), e64).astype(BF16)
    lane = lax.broadcasted_iota(jnp.int32, (tm, PAD_W), 1)
    ones_col = jnp.where(lane % LANES == ML_DH, 1.0, 0.0)
    mv_ref[...] = (_dot(proj(h, C_MV, C_MV + 256).astype(BF16), e64) + ones_col).astype(BF16)
    mo_ref[...] = _dot(proj(h, C_MO, C_MO + 256).astype(BF16), e64).astype(BF16)

    gq_ref[...] = _dot((proj(h, C_GQ, C_GQ + 128) * GL_DK ** -0.5).astype(BF16), e32).astype(BF16)
    gk_ref[...] = _dot(proj(h, C_GK, C_GK + 128).astype(BF16), e32).astype(BF16)
    gv_ref[...] = _dot(proj(h, C_GV, C_GV + 256).astype(BF16), e64).astype(BF16)
    gr_ref[...] = _dot(proj(h, C_GR, C_GR + 256).astype(BF16), e64).astype(BF16)
    misc_ref[...] = proj(h, C_MISC, C_MISC + LANES)


def _in_proj(xc, modtab, w_in_p, rope, conv_w, conv_b, e64, e32, n_ctx):
    b, n, d = xc.shape
    n_tiles = n // TM
    n_ctx_tiles = n_ctx // TM
    hb = TM // SUBLANES
    n_hb = n // SUBLANES
    row = lambda bb, i: (bb, i, 0)
    const2 = lambda bb, i: (0, 0)
    outs = [jax.ShapeDtypeStruct((b, n, PAD_W), BF16)] * 11 + [jax.ShapeDtypeStruct((b, n, LANES), F32)]
    out_specs = [pl.BlockSpec((None, TM, PAD_W), row)] * 11 + [pl.BlockSpec((None, TM, LANES), row)]
    return pl.pallas_call(
        functools.partial(_in_kernel, n_ctx_tiles),
        grid=(b, n_tiles),
        in_specs=[
            pl.BlockSpec((None, TM, d), row),
            pl.BlockSpec((None, SUBLANES, d), lambda bb, i: (bb, jnp.maximum(i * hb - 1, 0), 0)),
            pl.BlockSpec((None, SUBLANES, d), lambda bb, i: (bb, jnp.minimum((i + 1) * hb, n_hb - 1), 0)),
            pl.BlockSpec((None, None, 6, d), lambda bb, i: (bb, jnp.where(i >= n_ctx_tiles, 1, 0), 0, 0)),
            pl.BlockSpec((d, IN_WP), const2),
            pl.BlockSpec((3, TM, LANES), lambda bb, i: (0, i, 0)),
            pl.BlockSpec((3, 512), const2),
            pl.BlockSpec((1, 512), const2),
            pl.BlockSpec((256, PAD_W), const2),
            pl.BlockSpec((128, PAD_W), const2),
        ],
        out_specs=out_specs,
        out_shape=outs,
        scratch_shapes=[pltpu.VMEM((TM + 2 * SUBLANES, 512), F32)],
        compiler_params=_cparams(("parallel", "arbitrary")),
        name="in_proj",
    )(xc, xc, xc, modtab, w_in_p, rope, conv_w, conv_b, e64, e32)


def _attn_kernel(lam_init, n_ctx, tile0, q_ref, k_ref, v_ref, lamp_ref, g_ref, o_ref):
    i = pl.program_id(2) + tile0
    lp = lamp_ref[...]
    lam = (jnp.exp(jnp.sum(lp[0:1] * lp[1:2], axis=1, keepdims=True))
           - jnp.exp(jnp.sum(lp[2:3] * lp[3:4], axis=1, keepdims=True)) + lam_init)
    q = q_ref[...]
    lane = lax.broadcasted_iota(jnp.int32, q.shape, 1)
    zero = jnp.zeros_like(q)
    q1 = jnp.where(lane < DA_DH, q, zero)
    q2 = jnp.where(lane >= DA_DH, q, zero)

    def attend(nk):
        k = k_ref[0:nk, :]
        v = v_ref[0:nk, :]

        def softmax_parts(qm):
            s = _dot_nt(qm, k)
            e = jnp.exp(s - jnp.max(s, axis=1, keepdims=True))
            return e, 1.0 / jnp.sum(e, axis=1, keepdims=True)

        e1, r1 = softmax_parts(q1)
        e2, r2 = softmax_parts(q2)
        a = e1 * r1 - e2 * (r2 * lam)
        o = _dot(a.astype(BF16), v)
        y = o * lax.rsqrt(jnp.mean(o * o, axis=1, keepdims=True) + LN_EPS) * g_ref[...]
        o_ref[...] = (y * (1.0 - lam_init)).astype(BF16)

    n_ctx_tiles = n_ctx // q_ref.shape[0]

    @pl.when(i < n_ctx_tiles)
    def _():
        attend(n_ctx)

    @pl.when(i >= n_ctx_tiles)
    def _():
        attend(k_ref.shape[0])


def _attention(qa, ka, va, lam_p, norm_g, lam_init, n_ctx, tile0):
    b, n, _ = qa.shape
    tq = TM
    n_q = n // tq - tile0
    return pl.pallas_call(
        functools.partial(_attn_kernel, lam_init, n_ctx, tile0),
        grid=(b, HEADS, n_q),
        in_specs=[
            pl.BlockSpec((None, tq, LANES), lambda bb, h, i: (bb, i + tile0, h)),
            pl.BlockSpec((None, n, LANES), lambda bb, h, i: (bb, 0, h)),
            pl.BlockSpec((None, n, LANES), lambda bb, h, i: (bb, 0, h)),
            pl.BlockSpec((4, DA_DH), lambda bb, h, i: (0, 0)),
            pl.BlockSpec((1, LANES), lambda bb, h, i: (0, 0)),
        ],
        out_specs=pl.BlockSpec((None, tq, LANES), lambda bb, h, i: (bb, i, h)),
        out_shape=jax.ShapeDtypeStruct((b, n_q * tq, PAD_W), BF16),
        compiler_params=_cparams(("parallel", "parallel", "arbitrary")),
        name="diff_attn",
    )(qa, ka, va, lam_p, norm_g)


def _chunk_maps(n_ctx_chunks, n_chunks):
    def fwd(bb, t):
        return (bb, t, 0)

    def bwd(bb, t):
        return (bb, jnp.where(t < n_ctx_chunks, n_ctx_chunks - 1 - t, n_chunks - 1 + n_ctx_chunks - t), 0)

    return fwd, bwd


def _tri(direction):
    l = lax.broadcasted_iota(jnp.int32, (CHUNK, CHUNK), 0)
    s = lax.broadcasted_iota(jnp.int32, (CHUNK, CHUNK), 1)
    return (s <= l) if direction == 0 else (s >= l)


def _mlstm_kernel(qf_ref, kf_ref, vf_ref, gf_ref, qb_ref, kb_ref, vb_ref, gb_ref, bias_ref,
                  hf_ref, hb_ref, c_ref, m_ref):
    t = pl.program_id(1)

    @pl.when(t == 0)
    def _():
        c_ref[...] = jnp.zeros_like(c_ref)
        m_ref[...] = jnp.zeros_like(m_ref)

    lane = lax.broadcasted_iota(jnp.int32, (CHUNK, LANES), 1)
    is_fg = jnp.logical_and(lane >= M_FG, lane < M_GA)
    for d, (q_ref, k_ref, v_ref, g_ref, out_ref) in enumerate(
            ((qf_ref, kf_ref, vf_ref, gf_ref, hf_ref), (qb_ref, kb_ref, vb_ref, gb_ref, hb_ref))):
        mask = _tri(d)
        tri = jnp.where(mask, 1.0, 0.0)
        last = CHUNK - 1 if d == 0 else 0
        g = g_ref[...] + bias_ref[...]
        g = jnp.where(is_fg, _log_sigmoid(g), g)
        cum = _dot_hi(tri, g)
        g_t = g.T
        cum_t = cum.T
        for h in range(HEADS):
            ci, cf = M_IG + HEADS * d + h, M_FG + HEADS * d + h
            hs = slice(h * LANES, (h + 1) * LANES)
            q, k, v = q_ref[:, hs], k_ref[:, hs], v_ref[:, hs]
            bcol = cum[:, cf:cf + 1]
            icol = g[:, ci:ci + 1]
            row = g_t[ci:ci + 1, :] - cum_t[cf:cf + 1, :]
            m_prev = m_ref[d, h][0:1, 0:1]
            c_prev = c_ref[d, h]
            log_d = jnp.where(mask, bcol + row, -jnp.inf)
            log_a = bcol + m_prev
            m_t = jnp.maximum(log_a, jnp.max(log_d, axis=1, keepdims=True))
            dmat = jnp.exp(log_d - m_t)
            a = jnp.exp(log_a - m_t)
            s = _dot_nt(q, k) * dmat
            num = a * _dot(q, c_prev.astype(BF16)) + _dot(s.astype(BF16), v)
            qn = num[:, ML_DH:ML_DH + 1]
            den = jnp.maximum(jnp.abs(qn), jnp.exp(-m_t))
            out_ref[:, hs] = jnp.where(lane < ML_DH, num / den, 0.0)
            m_new = m_t[last:last + 1, :]
            b_last = bcol[last:last + 1, :]
            w = jnp.exp(b_last - bcol + icol - m_new)
            decay = jnp.exp(b_last + m_prev - m_new)
            kw = (k.astype(F32) * w).astype(BF16)
            c_ref[d, h] = decay * c_prev + _dot_tn(kw, v)
            m_ref[d, h] = jnp.broadcast_to(m_new, (SUBLANES, LANES))


def _mlstm(mq, mk, mv, misc, gate_bias, n_ctx):
    b, n, _ = mq.shape
    n_chunks = n // CHUNK
    fwd, bwd = _chunk_maps(n_ctx // CHUNK, n_chunks)
    wide = lambda m: pl.BlockSpec((None, CHUNK, PAD_W), m)
    thin = lambda m: pl.BlockSpec((None, CHUNK, LANES), m)
    return pl.pallas_call(
        _mlstm_kernel,
        grid=(b, n_chunks),
        in_specs=[wide(fwd), wide(fwd), wide(fwd), thin(fwd), wide(bwd), wide(bwd), wide(bwd), thin(bwd),
                  pl.BlockSpec((1, LANES), lambda bb, t: (0, 0))],
        out_specs=[wide(fwd), wide(bwd)],
        out_shape=[jax.ShapeDtypeStruct((b, n, PAD_W), F32)] * 2,
        scratch_shapes=[pltpu.VMEM((2, HEADS, LANES, LANES), F32),
                        pltpu.VMEM((2, HEADS, SUBLANES, LANES), F32)],
        compiler_params=_cparams(("parallel", "arbitrary")),
        name="mlstm_scan",
    )(mq, mk, mv, misc, mq, mk, mv, misc, gate_bias)


def _gla_kernel(qf_ref, kf_ref, vf_ref, gf_ref, qb_ref, kb_ref, vb_ref, gb_ref, wa_ref, ba_ref,
                hf_ref, hb_ref, st_ref):
    t = pl.program_id(1)

    @pl.when(t == 0)
    def _():
        st_ref[...] = jnp.zeros_like(st_ref)

    n_sub = CHUNK // SUB
    row_id = lax.broadcasted_iota(jnp.int32, (CHUNK, LANES), 0)
    row_blk = row_id // SUB
    col_id = lax.broadcasted_iota(jnp.int32, (CHUNK, LANES), 1)
    for d, (q_ref, k_ref, v_ref, g_ref, out_ref) in enumerate(
            ((qf_ref, kf_ref, vf_ref, gf_ref, hf_ref), (qb_ref, kb_ref, vb_ref, gb_ref, hb_ref))):
        tri = jnp.where(_tri(d), 1.0, 0.0)
        causal = (col_id <= row_id) if d == 0 else jnp.logical_and(col_id >= row_id, col_id < CHUNK)
        last = CHUNK - 1 if d == 0 else 0
        z = _dot_hi(g_ref[...], wa_ref[d]) + ba_ref[d]
        la = _log_sigmoid(z) * (1.0 / GL_NORMALIZER)
        cum = _dot_hi(tri, la)
        excl = cum - la
        firsts = [I * SUB if d == 0 else I * SUB + SUB - 1 for I in range(n_sub)]
        betas = [excl[r:r + 1, :] for r in firsts]
        beta_rows = jnp.concatenate([jnp.broadcast_to(bt, (SUB, PAD_W)) for bt in betas], axis=0)
        q = q_ref[...].astype(F32)
        k = k_ref[...].astype(F32)
        q_loc = (q * jnp.exp(cum - beta_rows)).astype(BF16)
        q_abs = (q * jnp.exp(cum)).astype(BF16)
        b_last = cum[last:last + 1, :]
        k_end = (k * jnp.exp(b_last - cum)).astype(BF16)
        rows_w = lax.broadcasted_iota(jnp.int32, (CHUNK, PAD_W), 0)
        k_subs = []
        for I in range(n_sub):
            seen = (rows_w < (I + 1) * SUB) if d == 0 else (rows_w >= I * SUB)
            k_subs.append((k * jnp.exp(jnp.where(seen, betas[I] - cum, -jnp.inf))).astype(BF16))
        zeros = jnp.zeros((CHUNK, LANES), BF16)
        for h in range(HEADS):
            hs = slice(h * LANES, (h + 1) * LANES)
            v = v_ref[:, hs]
            st = st_ref[d, h]
            kcat = jnp.concatenate([x for I in range(n_sub) for x in (k_subs[I][:, hs], zeros)], axis=0)
            r = _dot_nt(q_loc[:, hs], kcat)
            att = r[:, 0:LANES]
            for I in range(1, n_sub):
                att = jnp.where(row_blk == I, r[:, I * LANES:(I + 1) * LANES], att)
            att = jnp.where(causal, att, 0.0)
            out = _dot_nt(q_abs[:, hs], st.astype(BF16)) + _dot(att[:, 0:CHUNK].astype(BF16), v)
            out_ref[:, hs] = out
            st_ref[d, h] = st * jnp.exp(b_last[:, hs]) + _dot_tn(v, k_end[:, hs])


def _gla(gq, gk, gv, misc, wa_p, ba_p, n_ctx):
    b, n, _ = gq.shape
    n_chunks = n // CHUNK
    fwd, bwd = _chunk_maps(n_ctx // CHUNK, n_chunks)
    wide = lambda m: pl.BlockSpec((None, CHUNK, PAD_W), m)
    thin = lambda m: pl.BlockSpec((None, CHUNK, LANES), m)
    return pl.pallas_call(
        _gla_kernel,
        grid=(b, n_chunks),
        in_specs=[wide(fwd), wide(fwd), wide(fwd), thin(fwd), wide(bwd), wide(bwd), wide(bwd), thin(bwd),
                  pl.BlockSpec((2, LANES, PAD_W), lambda bb, t: (0, 0, 0)),
                  pl.BlockSpec((2, 1, PAD_W), lambda bb, t: (0, 0, 0))],
        out_specs=[wide(fwd), wide(bwd)],
        out_shape=[jax.ShapeDtypeStruct((b, n, PAD_W), F32)] * 2,
        scratch_shapes=[pltpu.VMEM((2, HEADS, LANES, LANES), F32)],
        compiler_params=_cparams(("parallel", "arbitrary")),
        name="gla_scan",
    )(gq, gk, gv, misc, gq, gk, gv, misc, wa_p, ba_p)


def _head_rms_padded(x, width):
    parts = []
    for h in range(HEADS):
        xh = x[:, h * LANES:(h + 1) * LANES]
        ms = jnp.sum(xh * xh, axis=1, keepdims=True) * (1.0 / width)
        parts.append(xh * lax.rsqrt(ms + LN_EPS))
    return jnp.concatenate(parts, axis=1)


def _out_kernel(dn_alpha, x_ref, da_ref, mhf_ref, mhb_ref, mo_ref, ghf_ref, ghb_ref, gr_ref,
                mod_ref, lng_ref, lnb_ref, mlg_ref, glg_ref, wo_ref, wr_ref,
                x1_ref, f_ref, ri_ref, rw_ref):
    ml = _head_rms_padded(mhf_ref[...] + mhb_ref[...], ML_DH) * mlg_ref[...] * _sigmoid(mo_ref[...].astype(F32))
    gl = _head_rms_padded(ghf_ref[...] + ghb_ref[...], GL_DV) * glg_ref[...] * _silu(gr_ref[...].astype(F32))
    y = (_dot(da_ref[...], wo_ref[0:PAD_W, :]) + _dot(ml.astype(BF16), wo_ref[PAD_W:2 * PAD_W, :])
         + _dot(gl.astype(BF16), wo_ref[2 * PAD_W:3 * PAD_W, :]))
    gate1, shift2, scale2 = mod_ref[2:3, :], mod_ref[3:4, :], mod_ref[4:5, :]
    x1 = _ln(dn_alpha * x_ref[...] + gate1 * y) * lng_ref[...] + lnb_ref[...]
    x1_ref[...] = x1
    f = _ln(x1) * (1.0 + scale2) + shift2
    f_ref[...] = f

    logits = _dot_hi(f, wr_ref[...])
    lane = lax.broadcasted_iota(jnp.int32, logits.shape, 1)
    neg = -jnp.inf
    big = jnp.int32(LANES)

    def first_argmax(vals, vmax):
        return jnp.min(jnp.where(vals == vmax, lane, big), axis=1, keepdims=True)

    gl_ = jnp.where(lane < MOE_GROUPS, logits, neg)
    gmax = jnp.max(gl_, axis=1, keepdims=True)
    g_idx = first_argmax(gl_, gmax)
    g_gate = 1.0 / jnp.sum(jnp.exp(gl_ - gmax), axis=1, keepdims=True)
    in_group = jnp.logical_and(lane >= MOE_GROUPS, (lane - MOE_GROUPS) // MOE_PER_GROUP == g_idx)
    el = jnp.where(in_group, logits, neg)
    emax = jnp.max(el, axis=1, keepdims=True)
    e0 = first_argmax(el, emax)
    el2 = jnp.where(lane == e0, neg, el)
    emax2 = jnp.max(el2, axis=1, keepdims=True)
    e1 = first_argmax(el2, emax2)
    esum = jnp.sum(jnp.exp(el - emax), axis=1, keepdims=True)
    p0 = 1.0 / esum
    p1 = jnp.exp(emax2 - emax) / esum
    w0 = g_gate * p0 / (p0 + p1)
    w1 = g_gate * p1 / (p0 + p1)
    ri_ref[...] = jnp.where(lane == 0, e0 - MOE_GROUPS, jnp.where(lane == 1, e1 - MOE_GROUPS, 0))
    rw_ref[...] = jnp.where(lane == 0, w0, jnp.where(lane == 1, w1, 0.0))


def _out_proj(xc, da, mhf, mhb, mo, ghf, ghb, gr, modtab, ln_g, ln_b, ml_g, gl_g, wo_p, w_route,
              dn_alpha, n_ctx, tile0):
    b, n, d = xc.shape
    n_ctx_tiles = n_ctx // TM
    n_t = n // TM - tile0
    row = lambda bb, i: (bb, i + tile0, 0)
    orow = lambda bb, i: (bb, i, 0)
    const2 = lambda bb, i: (0, 0)
    wide = pl.BlockSpec((None, TM, PAD_W), row)
    n_out = n_t * TM
    return pl.pallas_call(
        functools.partial(_out_kernel, dn_alpha),
        grid=(b, n_t),
        in_specs=[
            pl.BlockSpec((None, TM, d), row), pl.BlockSpec((None, TM, PAD_W), orow),
            wide, wide, wide, wide, wide, wide,
            pl.BlockSpec((None, None, 6, d), lambda bb, i: (bb, jnp.where(i + tile0 >= n_ctx_tiles, 1, 0), 0, 0)),
            pl.BlockSpec((1, d), const2), pl.BlockSpec((1, d), const2),
            pl.BlockSpec((1, PAD_W), const2), pl.BlockSpec((1, PAD_W), const2),
            pl.BlockSpec((3 * PAD_W, d), const2),
            pl.BlockSpec((d, LANES), const2),
        ],
        out_specs=[pl.BlockSpec((None, TM, d), orow), pl.BlockSpec((None, TM, d), orow),
                   pl.BlockSpec((None, TM, LANES), orow), pl.BlockSpec((None, TM, LANES), orow)],
        out_shape=[jax.ShapeDtypeStruct((b, n_out, d), F32), jax.ShapeDtypeStruct((b, n_out, d), F32),
                   jax.ShapeDtypeStruct((b, n_out, LANES), jnp.int32),
                   jax.ShapeDtypeStruct((b, n_out, LANES), F32)],
        compiler_params=_cparams(("parallel", "arbitrary")),
        name="out_proj",
    )(xc, da, mhf, mhb, mo, ghf, ghb, gr, modtab, ln_g, ln_b, ml_g, gl_g, wo_p, w_route)


def _row_copy(src_hbm, idx, dst, r, sem):
    return pltpu.make_async_copy(src_hbm.at[pl.ds(idx, 1), :], dst.at[pl.ds(r, 1), :], sem)


def _moe_kernel(be_ref, nu_ref, tok_ref, f_hbm, w1_ref, w3_ref, w2_ref, y_ref, xbuf, sems):
    j = pl.program_id(0)
    n_used = nu_ref[0]
    blk = xbuf.shape[1]

    def gather(block, slot, start):
        def body(r, carry):
            cp = _row_copy(f_hbm, tok_ref[block * blk + r], xbuf.at[slot], r, sems.at[slot])
            if start:
                cp.start()
            else:
                cp.wait()
            return carry
        lax.fori_loop(0, blk, body, 0, unroll=8)

    slot = j % 2

    @pl.when(jnp.logical_and(j == 0, n_used > 0))
    def _():
        gather(0, 0, True)

    @pl.when(j + 1 < n_used)
    def _():
        gather(j + 1, 1 - slot, True)

    @pl.when(j < n_used)
    def _():
        gather(j, slot, False)
        x = xbuf[slot].astype(BF16)
        hidden = _silu(_dot(x, w1_ref[...])) * _dot(x, w3_ref[...])
        y_ref[...] = _dot(hidden.astype(BF16), w2_ref[...])

    @pl.when(j >= n_used)
    def _():
        y_ref[...] = jnp.zeros_like(y_ref)


def _moe(f_flat, slot_tok, block_e, n_used, w1, w3, w2):
    t, d = f_flat.shape
    n_blocks = block_e.shape[0]
    hid = w1.shape[-1]
    wmap = lambda j, be, nu, tok: (be[j], 0, 0)
    return pl.pallas_call(
        _moe_kernel,
        grid_spec=pltpu.PrefetchScalarGridSpec(
            num_scalar_prefetch=3,
            grid=(n_blocks,),
            in_specs=[
                pl.BlockSpec(memory_space=pl.ANY),
                pl.BlockSpec((None, d, hid), wmap),
                pl.BlockSpec((None, d, hid), wmap),
                pl.BlockSpec((None, hid, d), wmap),
            ],
            out_specs=pl.BlockSpec((MOE_BLOCK, d), lambda j, be, nu, tok: (j, 0)),
            scratch_shapes=[pltpu.VMEM((2, MOE_BLOCK, d), F32), pltpu.SemaphoreType.DMA((2,))],
        ),
        out_shape=jax.ShapeDtypeStruct((n_blocks * MOE_BLOCK, d), F32),
        compiler_params=_cparams(("arbitrary",)),
        name="moe_experts",
    )(block_e, n_used, slot_tok, f_flat, w1, w3, w2)


def _comb_kernel(dn_alpha, dest_ref, y_hbm, x1_ref, rw_ref, mod_ref, lng_ref, lnb_ref, o_ref, ybuf, sems):
    bb = pl.program_id(0)
    i = pl.program_id(1)
    tm = x1_ref.shape[0]
    base = (bb * pl.num_programs(1) + i) * tm

    def gather(start):
        def body(r, carry):
            for kk in range(2):
                cp = _row_copy(y_hbm, dest_ref[(base + r) * 2 + kk], ybuf.at[kk], r, sems.at[kk])
                if start:
                    cp.start()
                else:
                    cp.wait()
            return carry
        lax.fori_loop(0, tm, body, 0, unroll=8)

    gather(True)
    gather(False)
    rw = rw_ref[...]
    y = rw[:, 0:1] * ybuf[0] + rw[:, 1:2] * ybuf[1]
    gate2 = mod_ref[5:6, :]
    o_ref[...] = _ln(dn_alpha * x1_ref[...] + gate2 * y) * lng_ref[...] + lnb_ref[...]


def _combine(dest, y_slots, x1, rw, modtab, ln_g, ln_b, dn_alpha, seg_of_tile):
    b, n, d = x1.shape
    n_t = n // TM
    row = lambda bb, i, dr: (bb, i, 0)
    const2 = lambda bb, i, dr: (0, 0)
    return pl.pallas_call(
        functools.partial(_comb_kernel, dn_alpha),
        grid_spec=pltpu.PrefetchScalarGridSpec(
            num_scalar_prefetch=1,
            grid=(b, n_t),
            in_specs=[
                pl.BlockSpec(memory_space=pl.ANY),
                pl.BlockSpec((None, TM, d), row),
                pl.BlockSpec((None, TM, LANES), row),
                pl.BlockSpec((None, None, 6, d), lambda bb, i, dr: (bb, seg_of_tile(i), 0, 0)),
                pl.BlockSpec((1, d), const2), pl.BlockSpec((1, d), const2),
            ],
            out_specs=pl.BlockSpec((None, TM, d), row),
            scratch_shapes=[pltpu.VMEM((2, TM, d), F32), pltpu.SemaphoreType.DMA((2,))],
        ),
        out_shape=jax.ShapeDtypeStruct((b, n, d), F32),
        compiler_params=_cparams(("arbitrary", "arbitrary")),
        name="moe_combine",
    )(dest, y_slots, x1, rw, modtab, ln_g, ln_b)


def _dispatch_tables(expert, n_blocks):
    a = expert.shape[0]
    onehot = (expert[:, None] == jnp.arange(MOE_EXPERTS, dtype=jnp.int32)[None, :]).astype(jnp.int32)
    incl = jnp.cumsum(onehot, axis=0)
    rank = jnp.sum((incl - onehot) * onehot, axis=1)
    counts = incl[-1]
    padded = (counts + MOE_BLOCK - 1) // MOE_BLOCK * MOE_BLOCK
    pad_end = jnp.cumsum(padded)
    pad_start = pad_end - padded
    dest = (pad_start[expert] + rank).astype(jnp.int32)
    tok = jnp.arange(a, dtype=jnp.int32) // 2
    slot_tok = jnp.zeros((n_blocks * MOE_BLOCK,), jnp.int32).at[dest].set(tok)
    block_start = jnp.arange(n_blocks, dtype=jnp.int32) * MOE_BLOCK
    block_e = jnp.minimum(jnp.searchsorted(pad_end, block_start, side='right'), MOE_EXPERTS - 1).astype(jnp.int32)
    n_used = (pad_end[-1] // MOE_BLOCK).astype(jnp.int32).reshape(1)
    return dest, slot_tok, block_e, n_used


def _pad_heads(w, width):
    lead = w.shape[:-1]
    w = w.reshape(*lead, HEADS, width)
    w = jnp.pad(w, [(0, 0)] * len(lead) + [(0, 0), (0, LANES - width)])
    return w.reshape(*lead, PAD_W)


def _expand_matrix(width):
    e = np.zeros((HEADS * width, PAD_W), np.float32)
    for h in range(HEADS):
        e[h * width + np.arange(width), h * LANES + np.arange(width)] = 1.0
    return jnp.asarray(e, BF16)


def _rope_tables(n_ctx, n_lat):
    n_freq = DA_DH // 4
    inv_freq = ROPE_BASE ** (-jnp.arange(n_freq, dtype=F32) / n_freq)
    pos = jnp.arange(n_lat, dtype=jnp.int32)
    ang_r = (pos // GRID_W).astype(F32)[:, None] * inv_freq
    ang_c = (pos % GRID_W).astype(F32)[:, None] * inv_freq
    ang = jnp.concatenate([ang_r, ang_r, ang_c, ang_c] * 2, axis=-1)
    cos, sin = jnp.cos(ang), jnp.sin(ang)
    first = (jnp.arange(LANES) % 32) < 16
    sin_a = jnp.where(first, -sin, 0.0)
    sin_b = jnp.where(first, 0.0, sin)
    ctx_rows = lambda v: jnp.full((n_ctx, LANES), v, F32)
    return jnp.stack([jnp.concatenate([ctx_rows(1.0), cos]), jnp.concatenate([ctx_rows(0.0), sin_a]),
                      jnp.concatenate([ctx_rows(0.0), sin_b])])


def kernel(x, c, ctx, c_ctx, w_ada, b_ada, w_in, da_lambda, da_norm, ml_conv_w, ml_conv_b, ml_ib, ml_fb,
           ml_norm, gl_wa, gl_ba, gl_norm, w_out, ln_mix_g, ln_mix_b, ln_ffn_g, ln_ffn_b,
           moe_wg, moe_we, moe_w1, moe_w3, moe_w2):
    b, s, d = x.shape
    n_ctx = ctx.shape[1]
    n = n_ctx + s
    depth = w_in.shape[0]
    assert n_ctx % TM == 0 and s % TM == 0 and d % LANES == 0
    dn_alpha = (2 * depth) ** 0.25

    rows = -(-(b + 1) // SUBLANES) * SUBLANES
    cc = jnp.concatenate([c, c_ctx[None, :], jnp.zeros((rows - b - 1, d), F32)], axis=0)
    mod = _ada(cc, w_ada, b_ada).reshape(depth, rows, 6, d)

    xc = jnp.concatenate([ctx, x], axis=1)
    rope = _rope_tables(n_ctx, s)
    e64, e32 = _expand_matrix(64), _expand_matrix(32)

    for l in range(depth):
        last = l == depth - 1
        tile0 = n_ctx // TM if last else 0
        lam_init = 0.8 - 0.6 * math.exp(-0.3 * l)
        modtab = jnp.stack([jnp.broadcast_to(mod[l, b], (b, 6, d)), mod[l, :b]], axis=1)

        wi = w_in[l]
        o = np.cumsum([0, 512, 512, 512, 512, 256, 256, 8, 8, 128, 128, 256, 256, 32])
        misc_w = jnp.concatenate([wi[:, o[6]:o[8]], wi[:, o[12]:o[13]],
                                  jnp.zeros((d, LANES - 16 - 2 * GL_RANK), F32)], axis=1)
        w_in_p = jnp.concatenate([wi[:, :o[6]], wi[:, o[8]:o[12]], misc_w], axis=1).astype(BF16)

        (qa, ka, va, mq, mk, mv, mo, gq, gk, gv, gr, misc) = _in_proj(
            xc, modtab, w_in_p, rope, ml_conv_w[l], ml_conv_b[l].reshape(1, -1), e64, e32, n_ctx)

        da = _attention(qa, ka, va, da_lambda[l], da_norm[l].reshape(1, -1), lam_init, n_ctx, tile0)

        gate_bias = jnp.concatenate([ml_ib[l].reshape(-1), ml_fb[l].reshape(-1),
                                     jnp.zeros((LANES - 16,), F32)]).reshape(1, LANES)
        mhf, mhb = _mlstm(mq, mk, mv, misc, gate_bias, n_ctx)

        wa_p = jnp.zeros((2, LANES, PAD_W), F32)
        for dd in range(2):
            wa_p = wa_p.at[dd, M_GA + dd * GL_RANK:M_GA + (dd + 1) * GL_RANK].set(_pad_heads(gl_wa[l, dd], GL_DK))
        ba_p = _pad_heads(gl_ba[l], GL_DK).reshape(2, 1, PAD_W)
        ghf, ghb = _gla(gq, gk, gv, misc, wa_p, ba_p, n_ctx)

        wo = w_out[l]
        wo_p = jnp.concatenate([
            wo[:512],
            jnp.pad(wo[512:768].reshape(HEADS, 64, d), ((0, 0), (0, 64), (0, 0))).reshape(PAD_W, d),
            jnp.pad(wo[768:1024].reshape(HEADS, 64, d), ((0, 0), (0, 64), (0, 0))).reshape(PAD_W, d),
        ], axis=0).astype(BF16)
        w_route = jnp.concatenate([moe_wg[l], moe_we[l],
                                   jnp.zeros((d, LANES - MOE_GROUPS - MOE_EXPERTS), F32)], axis=1)
        ml_g = _pad_heads(jnp.tile(ml_norm[l], HEADS), ML_DH).reshape(1, PAD_W)
        gl_g = _pad_heads(jnp.tile(gl_norm[l], HEADS), GL_DV).reshape(1, PAD_W)
        x1, f, ri, rw = _out_proj(xc, da, mhf, mhb, mo, ghf, ghb, gr, modtab,
                                  ln_mix_g[l].reshape(1, d), ln_mix_b[l].reshape(1, d), ml_g, gl_g,
                                  wo_p, w_route, dn_alpha, n_ctx, tile0)

        n_rows = x1.shape[1]
        t_all = b * n_rows
        expert = ri[:, :, :2].reshape(-1)
        a_all = t_all * 2
        n_blocks = -(-(a_all + MOE_EXPERTS * (MOE_BLOCK - 1)) // MOE_BLOCK)
        dest, slot_tok, block_e, n_used = _dispatch_tables(expert, n_blocks)
        y_slots = _moe(f.reshape(t_all, d), slot_tok, block_e, n_used,
                       moe_w1[l].astype(BF16), moe_w3[l].astype(BF16), moe_w2[l].astype(BF16))
        n_ctx_tiles_here = 0 if last else n_ctx // TM
        seg = lambda i: jnp.where(i >= n_ctx_tiles_here, 1, 0)
        xc = _combine(dest, y_slots, x1, rw, modtab, ln_ffn_g[l].reshape(1, d), ln_ffn_b[l].reshape(1, d),
                      dn_alpha, seg)
    return xc
```

```python
import functools
import math

import numpy as np
import jax
import jax.numpy as jnp
from jax import lax
from jax.experimental import pallas as pl
from jax.experimental.pallas import tpu as pltpu

F32 = jnp.float32
BF16 = jnp.bfloat16
HI = lax.Precision.HIGHEST

GRID_W = 64
HEADS = 4
DA_DH = 64
ML_DH = 64
GL_DK = 32
GL_DV = 64
GL_RANK = 16
GL_NORMALIZER = 16.0
CHUNK = 64
SUB = 16
ROPE_BASE = 10000.0
MOE_GROUPS = 4
MOE_PER_GROUP = 8
MOE_EXPERTS = 32
MOE_HIDDEN = 512
LN_EPS = 1e-6

LANES = 128
SUBLANES = 8
VMEM_LIMIT = 48 * 1024 * 1024

TM = 256
MOE_BLOCK = 256
PAD_W = HEADS * LANES

C_QA, C_KA, C_VA, C_MQK, C_MV, C_MO = 0, 512, 1024, 1536, 2048, 2304
C_GQ, C_GK, C_GV, C_GR, C_MISC, IN_WP = 2560, 2688, 2816, 3072, 3328, 3456
M_IG, M_FG, M_GA = 0, 8, 16


def _cparams(sem):
    return pltpu.CompilerParams(dimension_semantics=sem, vmem_limit_bytes=VMEM_LIMIT)


def _sigmoid(x):
    return 1.0 / (1.0 + jnp.exp(-x))


def _silu(x):
    return x * _sigmoid(x)


def _log_sigmoid(x):
    return jnp.minimum(x, 0.0) - jnp.log(1.0 + jnp.exp(-jnp.abs(x)))


def _ln(x):
    mu = jnp.mean(x, axis=-1, keepdims=True)
    xc = x - mu
    var = jnp.mean(xc * xc, axis=-1, keepdims=True)
    return xc * lax.rsqrt(var + LN_EPS)


def _dot(a, b):
    return jnp.dot(a, b, preferred_element_type=F32)


def _dot_hi(a, b):
    return jnp.dot(a, b, precision=HI, preferred_element_type=F32)


def _dot_nt(a, b):
    return lax.dot_general(a, b, (((1,), (1,)), ((), ())), preferred_element_type=F32)


def _dot_tn(a, b):
    return lax.dot_general(a, b, (((0,), (0,)), ((), ())), preferred_element_type=F32)


def _ada_kernel(c_ref, w_ref, b_ref, o_ref):
    o_ref[...] = _dot_hi(_silu(c_ref[...]), w_ref[...]) + b_ref[...]


def _ada(cc, w_ada, b_ada):
    depth, d, n6 = w_ada.shape
    tn = 1536
    rows = cc.shape[0]
    return pl.pallas_call(
        _ada_kernel,
        grid=(depth, n6 // tn),
        in_specs=[
            pl.BlockSpec((rows, d), lambda l, j: (0, 0)),
            pl.BlockSpec((None, d, tn), lambda l, j: (l, 0, j)),
            pl.BlockSpec((None, 1, tn), lambda l, j: (l, 0, j)),
        ],
        out_specs=pl.BlockSpec((None, rows, tn), lambda l, j: (l, 0, j)),
        out_shape=jax.ShapeDtypeStruct((depth, rows, n6), F32),
        compiler_params=_cparams(("arbitrary", "arbitrary")),
        name="ada_mod",
    )(cc, w_ada, b_ada.reshape(depth, 1, n6))


def _in_kernel(n_ctx_tiles, x_ref, xp_ref, xn_ref, mod_ref, w_ref, rope_ref, cw_ref, cb_ref,
               e64_ref, e32_ref,
               qa_ref, ka_ref, va_ref, mq_ref, mk_ref, mv_ref, mo_ref,
               gq_ref, gk_ref, gv_ref, gr_ref, misc_ref, pext_ref):
    i = pl.program_id(1)
    n_tiles = pl.num_programs(1)
    tm = x_ref.shape[0]
    shift, scale = mod_ref[0:1, :], mod_ref[1:2, :]

    def prenorm(x):
        return (_ln(x) * (1.0 + scale) + shift).astype(BF16)

    h = prenorm(x_ref[...])

    def proj(a, lo, hi):
        return _dot(a, w_ref[:, lo:hi])

    cos, sin_a, sin_b = rope_ref[0], rope_ref[1], rope_ref[2]
    for lo, out_ref, mult in ((C_QA, qa_ref, DA_DH ** -0.5), (C_KA, ka_ref, 1.0)):
        acc = proj(h, lo, lo + PAD_W)
        for hh in range(HEADS):
            xh = acc[:, hh * LANES:(hh + 1) * LANES]
            r = xh * cos + pltpu.roll(xh, LANES - 16, 1) * sin_a + pltpu.roll(xh, 16, 1) * sin_b
            out_ref[:, hh * LANES:(hh + 1) * LANES] = (r * mult).astype(BF16)
    va_ref[...] = proj(h, C_VA, C_VA + PAD_W).astype(BF16)

    seg_first = jnp.logical_or(i == 0, i == n_ctx_tiles)
    seg_last = jnp.logical_or(i == n_ctx_tiles - 1, i == n_tiles - 1)
    keep_prev = jnp.where(seg_first, 0.0, 1.0)
    keep_next = jnp.where(seg_last, 0.0, 1.0)
    p = proj(h, C_MQK, C_MQK + 512)
    pext_ref[0:SUBLANES, :] = proj(prenorm(xp_ref[...]), C_MQK, C_MQK + 512) * keep_prev
    pext_ref[SUBLANES:SUBLANES + tm, :] = p
    pext_ref[SUBLANES + tm:2 * SUBLANES + tm, :] = proj(prenorm(xn_ref[...]), C_MQK, C_MQK + 512) * keep_next
    y = (cw_ref[0:1, :] * pext_ref[SUBLANES - 1:SUBLANES - 1 + tm, :] + cw_ref[1:2, :] * p
         + cw_ref[2:3, :] * pext_ref[SUBLANES + 1:SUBLANES + 1 + tm, :] + cb_ref[...])
    y = _silu(y)
    e64 = e64_ref[...]
    e32 = e32_ref[...]
    mq_ref[...] = _dot((y[:, :256] * ML_DH ** -0.5).astype(BF16), e64).astype(BF16)
    mk_ref[...] = _dot(y[:, 256:].astype(BF16), e64).astype(BF16)
    lane = lax.broadcasted_iota(jnp.int32, (tm, PAD_W), 1)
    ones_col = jnp.where(lane % LANES == ML_DH, 1.0, 0.0)
    mv_ref[...] = (_dot(proj(h, C_MV, C_MV + 256).astype(BF16), e64) + ones_col).astype(BF16)
    mo_ref[...] = _dot(proj(h, C_MO, C_MO + 256).astype(BF16), e64).astype(BF16)

    gq_ref[...] = _dot((proj(h, C_GQ, C_GQ + 128) * GL_DK ** -0.5).astype(BF16), e32).astype(BF16)
    gk_ref[...] = _dot(proj(h, C_GK, C_GK + 128).astype(BF16), e32).astype(BF16)
    gv_ref[...] = _dot(proj(h, C_GV, C_GV + 256).astype(BF16), e64).astype(BF16)
    gr_ref[...] = _dot(proj(h, C_GR, C_GR + 256).astype(BF16), e64).astype(BF16)
    misc_ref[...] = proj(h, C_MISC, C_MISC + LANES)


def _in_proj(xc, modtab, w_in_p, rope, conv_w, conv_b, e64, e32, n_ctx):
    b, n, d = xc.shape
    n_tiles = n // TM
    n_ctx_tiles = n_ctx // TM
    hb = TM // SUBLANES
    n_hb = n // SUBLANES
    row = lambda bb, i: (bb, i, 0)
    const2 = lambda bb, i: (0, 0)
    outs = [jax.ShapeDtypeStruct((b, n, PAD_W), BF16)] * 11 + [jax.ShapeDtypeStruct((b, n, LANES), F32)]
    out_specs = [pl.BlockSpec((None, TM, PAD_W), row)] * 11 + [pl.BlockSpec((None, TM, LANES), row)]
    return pl.pallas_call(
        functools.partial(_in_kernel, n_ctx_tiles),
        grid=(b, n_tiles),
        in_specs=[
            pl.BlockSpec((None, TM, d), row),
            pl.BlockSpec((None, SUBLANES, d), lambda bb, i: (bb, jnp.maximum(i * hb - 1, 0), 0)),
            pl.BlockSpec((None, SUBLANES, d), lambda bb, i: (bb, jnp.minimum((i + 1) * hb, n_hb - 1), 0)),
            pl.BlockSpec((None, None, 6, d), lambda bb, i: (bb, jnp.where(i >= n_ctx_tiles, 1, 0), 0, 0)),
            pl.BlockSpec((d, IN_WP), const2),
            pl.BlockSpec((3, TM, LANES), lambda bb, i: (0, i, 0)),
            pl.BlockSpec((3, 512), const2),
            pl.BlockSpec((1, 512), const2),
            pl.BlockSpec((256, PAD_W), const2),
            pl.BlockSpec((128, PAD_W), const2),
        ],
        out_specs=out_specs,
        out_shape=outs,
        scratch_shapes=[pltpu.VMEM((TM + 2 * SUBLANES, 512), F32)],
        compiler_params=_cparams(("parallel", "arbitrary")),
        name="in_proj",
    )(xc, xc, xc, modtab, w_in_p, rope, conv_w, conv_b, e64, e32)


def _attn_kernel(lam_init, n_ctx, tile0, q_ref, k_ref, v_ref, lamp_ref, g_ref, o_ref):
    i = pl.program_id(2) + tile0
    lp = lamp_ref[...]
    lam = (jnp.exp(jnp.sum(lp[0:1] * lp[1:2], axis=1, keepdims=True))
           - jnp.exp(jnp.sum(lp[2:3] * lp[3:4], axis=1, keepdims=True)) + lam_init)
    q = q_ref[...]
    lane = lax.broadcasted_iota(jnp.int32, q.shape, 1)
    zero = jnp.zeros_like(q)
    q1 = jnp.where(lane < DA_DH, q, zero)
    q2 = jnp.where(lane >= DA_DH, q, zero)

    def attend(nk):
        k = k_ref[0:nk, :]
        v = v_ref[0:nk, :]

        def softmax_parts(qm):
            s = _dot_nt(qm, k)
            e = jnp.exp(s - jnp.max(s, axis=1, keepdims=True))
            return e, 1.0 / jnp.sum(e, axis=1, keepdims=True)

        e1, r1 = softmax_parts(q1)
        e2, r2 = softmax_parts(q2)
        a = e1 * r1 - e2 * (r2 * lam)
        o = _dot(a.astype(BF16), v)
        y = o * lax.rsqrt(jnp.mean(o * o, axis=1, keepdims=True) + LN_EPS) * g_ref[...]
        o_ref[...] = (y * (1.0 - lam_init)).astype(BF16)

    n_ctx_tiles = n_ctx // q_ref.shape[0]

    @pl.when(i < n_ctx_tiles)
    def _():
        attend(n_ctx)

    @pl.when(i >= n_ctx_tiles)
    def _():
        attend(k_ref.shape[0])


def _attention(qa, ka, va, lam_p, norm_g, lam_init, n_ctx, tile0):
    b, n, _ = qa.shape
    tq = TM
    n_q = n // tq - tile0
    return pl.pallas_call(
        functools.partial(_attn_kernel, lam_init, n_ctx, tile0),
        grid=(b, HEADS, n_q),
        in_specs=[
            pl.BlockSpec((None, tq, LANES), lambda bb, h, i: (bb, i + tile0, h)),
            pl.BlockSpec((None, n, LANES), lambda bb, h, i: (bb, 0, h)),
            pl.BlockSpec((None, n, LANES), lambda bb, h, i: (bb, 0, h)),
            pl.BlockSpec((4, DA_DH), lambda bb, h, i: (0, 0)),
            pl.BlockSpec((1, LANES), lambda bb, h, i: (0, 0)),
        ],
        out_specs=pl.BlockSpec((None, tq, LANES), lambda bb, h, i: (bb, i, h)),
        out_shape=jax.ShapeDtypeStruct((b, n_q * tq, PAD_W), BF16),
        compiler_params=_cparams(("parallel", "parallel", "arbitrary")),
        name="diff_attn",
    )(qa, ka, va, lam_p, norm_g)


def _chunk_maps(n_ctx_chunks, n_chunks):
    def fwd(bb, t):
        return (bb, t, 0)

    def bwd(bb, t):
        return (bb, jnp.where(t < n_ctx_chunks, n_ctx_chunks - 1 - t, n_chunks - 1 + n_ctx_chunks - t), 0)

    return fwd, bwd


def _tri(direction):
    l = lax.broadcasted_iota(jnp.int32, (CHUNK, CHUNK), 0)
    s = lax.broadcasted_iota(jnp.int32, (CHUNK, CHUNK), 1)
    return (s <= l) if direction == 0 else (s >= l)


def _mlstm_kernel(qf_ref, kf_ref, vf_ref, gf_ref, qb_ref, kb_ref, vb_ref, gb_ref, bias_ref,
                  hf_ref, hb_ref, c_ref, m_ref):
    t = pl.program_id(1)

    @pl.when(t == 0)
    def _():
        c_ref[...] = jnp.zeros_like(c_ref)
        m_ref[...] = jnp.zeros_like(m_ref)

    lane = lax.broadcasted_iota(jnp.int32, (CHUNK, LANES), 1)
    is_fg = jnp.logical_and(lane >= M_FG, lane < M_GA)
    for d, (q_ref, k_ref, v_ref, g_ref, out_ref) in enumerate(
            ((qf_ref, kf_ref, vf_ref, gf_ref, hf_ref), (qb_ref, kb_ref, vb_ref, gb_ref, hb_ref))):
        mask = _tri(d)
        tri = jnp.where(mask, 1.0, 0.0)
        last = CHUNK - 1 if d == 0 else 0
        g = g_ref[...] + bias_ref[...]
        g = jnp.where(is_fg, _log_sigmoid(g), g)
        cum = _dot_hi(tri, g)
        g_t = g.T
        cum_t = cum.T
        for h in range(HEADS):
            ci, cf = M_IG + HEADS * d + h, M_FG + HEADS * d + h
            hs = slice(h * LANES, (h + 1) * LANES)
            q, k, v = q_ref[:, hs], k_ref[:, hs], v_ref[:, hs]
            bcol = cum[:, cf:cf + 1]
            icol = g[:, ci:ci + 1]
            row = g_t[ci:ci + 1, :] - cum_t[cf:cf + 1, :]
            m_prev = m_ref[d, h][0:1, 0:1]
            c_prev = c_ref[d, h]
            log_d = jnp.where(mask, bcol + row, -jnp.inf)
            log_a = bcol + m_prev
            m_t = jnp.maximum(log_a, jnp.max(log_d, axis=1, keepdims=True))
            dmat = jnp.exp(log_d - m_t)
            a = jnp.exp(log_a - m_t)
            s = _dot_nt(q, k) * dmat
            num = a * _dot(q, c_prev.astype(BF16)) + _dot(s.astype(BF16), v)
            qn = num[:, ML_DH:ML_DH + 1]
            den = jnp.maximum(jnp.abs(qn), jnp.exp(-m_t))
            out_ref[:, hs] = jnp.where(lane < ML_DH, num / den, 0.0)
            m_new = m_t[last:last + 1, :]
            b_last = bcol[last:last + 1, :]
            w = jnp.exp(b_last - bcol + icol - m_new)
            decay = jnp.exp(b_last + m_prev - m_new)
            kw = (k.astype(F32) * w).astype(BF16)
            c_ref[d, h] = decay * c_prev + _dot_tn(kw, v)
            m_ref[d, h] = jnp.broadcast_to(m_new, (SUBLANES, LANES))


def _mlstm(mq, mk, mv, misc, gate_bias, n_ctx):
    b, n, _ = mq.shape
    n_chunks = n // CHUNK
    fwd, bwd = _chunk_maps(n_ctx // CHUNK, n_chunks)
    wide = lambda m: pl.BlockSpec((None, CHUNK, PAD_W), m)
    thin = lambda m: pl.BlockSpec((None, CHUNK, LANES), m)
    return pl.pallas_call(
        _mlstm_kernel,
        grid=(b, n_chunks),
        in_specs=[wide(fwd), wide(fwd), wide(fwd), thin(fwd), wide(bwd), wide(bwd), wide(bwd), thin(bwd),
                  pl.BlockSpec((1, LANES), lambda bb, t: (0, 0))],
        out_specs=[wide(fwd), wide(bwd)],
        out_shape=[jax.ShapeDtypeStruct((b, n, PAD_W), F32)] * 2,
        scratch_shapes=[pltpu.VMEM((2, HEADS, LANES, LANES), F32),
                        pltpu.VMEM((2, HEADS, SUBLANES, LANES), F32)],
        compiler_params=_cparams(("parallel", "arbitrary")),
        name="mlstm_scan",
    )(mq, mk, mv, misc, mq, mk, mv, misc, gate_bias)


def _gla_kernel(qf_ref, kf_ref, vf_ref, gf_ref, qb_ref, kb_ref, vb_ref, gb_ref, wa_ref, ba_ref,
                hf_ref, hb_ref, st_ref):
    t = pl.program_id(1)

    @pl.when(t == 0)
    def _():
        st_ref[...] = jnp.zeros_like(st_ref)

    n_sub = CHUNK // SUB
    row_id = lax.broadcasted_iota(jnp.int32, (CHUNK, LANES), 0)
    row_blk = row_id // SUB
    col_id = lax.broadcasted_iota(jnp.int32, (CHUNK, LANES), 1)
    for d, (q_ref, k_ref, v_ref, g_ref, out_ref) in enumerate(
            ((qf_ref, kf_ref, vf_ref, gf_ref, hf_ref), (qb_ref, kb_ref, vb_ref, gb_ref, hb_ref))):
        tri = jnp.where(_tri(d), 1.0, 0.0)
        causal = (col_id <= row_id) if d == 0 else jnp.logical_and(col_id >= row_id, col_id < CHUNK)
        last = CHUNK - 1 if d == 0 else 0
        z = _dot_hi(g_ref[...], wa_ref[d]) + ba_ref[d]
        la = _log_sigmoid(z) * (1.0 / GL_NORMALIZER)
        cum = _dot_hi(tri, la)
        excl = cum - la
        firsts = [I * SUB if d == 0 else I * SUB + SUB - 1 for I in range(n_sub)]
        betas = [excl[r:r + 1, :] for r in firsts]
        beta_rows = jnp.concatenate([jnp.broadcast_to(bt, (SUB, PAD_W)) for bt in betas], axis=0)
        q = q_ref[...].astype(F32)
        k = k_ref[...].astype(F32)
        q_loc = (q * jnp.exp(cum - beta_rows)).astype(BF16)
        q_abs = (q * jnp.exp(cum)).astype(BF16)
        b_last = cum[last:last + 1, :]
        k_end = (k * jnp.exp(b_last - cum)).astype(BF16)
        rows_w = lax.broadcasted_iota(jnp.int32, (CHUNK, PAD_W), 0)
        k_subs = []
        for I in range(n_sub):
            seen = (rows_w < (I + 1) * SUB) if d == 0 else (rows_w >= I * SUB)
            k_subs.append((k * jnp.exp(jnp.where(seen, betas[I] - cum, -jnp.inf))).astype(BF16))
        zeros = jnp.zeros((CHUNK, LANES), BF16)
        for h in range(HEADS):
            hs = slice(h * LANES, (h + 1) * LANES)
            v = v_ref[:, hs]
            st = st_ref[d, h]
            kcat = jnp.concatenate([x for I in range(n_sub) for x in (k_subs[I][:, hs], zeros)], axis=0)
            r = _dot_nt(q_loc[:, hs], kcat)
            att = r[:, 0:LANES]
            for I in range(1, n_sub):
                att = jnp.where(row_blk == I, r[:, I * LANES:(I + 1) * LANES], att)
            att = jnp.where(causal, att, 0.0)
            out = _dot_nt(q_abs[:, hs], st.astype(BF16)) + _dot(att[:, 0:CHUNK].astype(BF16), v)
            out_ref[:, hs] = out
            st_ref[d, h] = st * jnp.exp(b_last[:, hs]) + _dot_tn(v, k_end[:, hs])


def _gla(gq, gk, gv, misc, wa_p, ba_p, n_ctx):
    b, n, _ = gq.shape
    n_chunks = n // CHUNK
    fwd, bwd = _chunk_maps(n_ctx // CHUNK, n_chunks)
    wide = lambda m: pl.BlockSpec((None, CHUNK, PAD_W), m)
    thin = lambda m: pl.BlockSpec((None, CHUNK, LANES), m)
    return pl.pallas_call(
        _gla_kernel,
        grid=(b, n_chunks),
        in_specs=[wide(fwd), wide(fwd), wide(fwd), thin(fwd), wide(bwd), wide(bwd), wide(bwd), thin(bwd),
                  pl.BlockSpec((2, LANES, PAD_W), lambda bb, t: (0, 0, 0)),
                  pl.BlockSpec((2, 1, PAD_W), lambda bb, t: (0, 0, 0))],
        out_specs=[wide(fwd), wide(bwd)],
        out_shape=[jax.ShapeDtypeStruct((b, n, PAD_W), F32)] * 2,
        scratch_shapes=[pltpu.VMEM((2, HEADS, LANES, LANES), F32)],
        compiler_params=_cparams(("parallel", "arbitrary")),
        name="gla_scan",
    )(gq, gk, gv, misc, gq, gk, gv, misc, wa_p, ba_p)


def _head_rms_padded(x, width):
    parts = []
    for h in range(HEADS):
        xh = x[:, h * LANES:(h + 1) * LANES]
        ms = jnp.sum(xh * xh, axis=1, keepdims=True) * (1.0 / width)
        parts.append(xh * lax.rsqrt(ms + LN_EPS))
    return jnp.concatenate(parts, axis=1)


def _out_kernel(dn_alpha, x_ref, da_ref, mhf_ref, mhb_ref, mo_ref, ghf_ref, ghb_ref, gr_ref,
                mod_ref, lng_ref, lnb_ref, mlg_ref, glg_ref, wo_ref, wr_ref,
                x1_ref, f_ref, info_ref, cnt_ref, rw_ref):
    ml = _head_rms_padded(mhf_ref[...] + mhb_ref[...], ML_DH) * mlg_ref[...] * _sigmoid(mo_ref[...].astype(F32))
    gl = _head_rms_padded(ghf_ref[...] + ghb_ref[...], GL_DV) * glg_ref[...] * _silu(gr_ref[...].astype(F32))
    y = (_dot(da_ref[...], wo_ref[0:PAD_W, :]) + _dot(ml.astype(BF16), wo_ref[PAD_W:2 * PAD_W, :])
         + _dot(gl.astype(BF16), wo_ref[2 * PAD_W:3 * PAD_W, :]))
    gate1, shift2, scale2 = mod_ref[2:3, :], mod_ref[3:4, :], mod_ref[4:5, :]
    x1 = _ln(dn_alpha * x_ref[...] + gate1 * y) * lng_ref[...] + lnb_ref[...]
    x1_ref[...] = x1
    f = _ln(x1) * (1.0 + scale2) + shift2
    tm = f.shape[0]
    for s in range(SUBLANES):
        f_ref[pl.ds(s, tm, stride=SUBLANES), :] = f[:, s * LANES:(s + 1) * LANES]

    logits = _dot_hi(f, wr_ref[...])
    lane = lax.broadcasted_iota(jnp.int32, logits.shape, 1)
    neg = -jnp.inf
    big = jnp.int32(LANES)

    def first_argmax(vals, vmax):
        return jnp.min(jnp.where(vals == vmax, lane, big), axis=1, keepdims=True)

    gl_ = jnp.where(lane < MOE_GROUPS, logits, neg)
    gmax = jnp.max(gl_, axis=1, keepdims=True)
    g_idx = first_argmax(gl_, gmax)
    g_gate = 1.0 / jnp.sum(jnp.exp(gl_ - gmax), axis=1, keepdims=True)
    in_group = jnp.logical_and(lane >= MOE_GROUPS, (lane - MOE_GROUPS) // MOE_PER_GROUP == g_idx)
    el = jnp.where(in_group, logits, neg)
    emax = jnp.max(el, axis=1, keepdims=True)
    e0 = first_argmax(el, emax)
    el2 = jnp.where(lane == e0, neg, el)
    emax2 = jnp.max(el2, axis=1, keepdims=True)
    e1 = first_argmax(el2, emax2)
    esum = jnp.sum(jnp.exp(el - emax), axis=1, keepdims=True)
    p0 = 1.0 / esum
    p1 = jnp.exp(emax2 - emax) / esum
    w0 = g_gate * p0 / (p0 + p1)
    w1 = g_gate * p1 / (p0 + p1)
    rw_ref[...] = jnp.where(lane == 0, w0, jnp.where(lane == 1, w1, 0.0))

    oh0 = jnp.where(lane == e0, 1.0, 0.0)
    oh1 = jnp.where(lane == e1, 1.0, 0.0)
    both = (oh0 + oh1).astype(BF16)
    r_id = lax.broadcasted_iota(jnp.int32, (tm, tm), 0)
    c_id = lax.broadcasted_iota(jnp.int32, (tm, tm), 1)
    earlier = jnp.where(c_id < r_id, 1.0, 0.0).astype(BF16)
    before = _dot(earlier, both)
    rank0 = jnp.sum(before * oh0, axis=1, keepdims=True)
    rank1 = jnp.sum(before * oh1, axis=1, keepdims=True)
    cnt_ref[...] = _dot(jnp.ones((SUBLANES, tm), BF16), both)
    info = jnp.where(lane == 0, (e0 - MOE_GROUPS).astype(F32),
                     jnp.where(lane == 1, (e1 - MOE_GROUPS).astype(F32),
                               jnp.where(lane == 2, rank0, jnp.where(lane == 3, rank1, 0.0))))
    info_ref[...] = info.T[0:SUBLANES, :]


def _out_proj(xc, da, mhf, mhb, mo, ghf, ghb, gr, modtab, ln_g, ln_b, ml_g, gl_g, wo_p, w_route,
              dn_alpha, n_ctx, tile0):
    b, n, d = xc.shape
    n_ctx_tiles = n_ctx // TM
    n_t = n // TM - tile0
    row = lambda bb, i: (bb, i + tile0, 0)
    orow = lambda bb, i: (bb, i, 0)
    const2 = lambda bb, i: (0, 0)
    wide = pl.BlockSpec((None, TM, PAD_W), row)
    n_out = n_t * TM
    return pl.pallas_call(
        functools.partial(_out_kernel, dn_alpha),
        grid=(b, n_t),
        in_specs=[
            pl.BlockSpec((None, TM, d), row), pl.BlockSpec((None, TM, PAD_W), orow),
            wide, wide, wide, wide, wide, wide,
            pl.BlockSpec((None, None, 6, d), lambda bb, i: (bb, jnp.where(i + tile0 >= n_ctx_tiles, 1, 0), 0, 0)),
            pl.BlockSpec((1, d), const2), pl.BlockSpec((1, d), const2),
            pl.BlockSpec((1, PAD_W), const2), pl.BlockSpec((1, PAD_W), const2),
            pl.BlockSpec((3 * PAD_W, d), const2),
            pl.BlockSpec((d, LANES), const2),
        ],
        out_specs=[pl.BlockSpec((None, TM, d), orow),
                   pl.BlockSpec((None, TM * SUBLANES, LANES), orow),
                   pl.BlockSpec((None, SUBLANES, TM), orow),
                   pl.BlockSpec((None, SUBLANES, LANES), orow),
                   pl.BlockSpec((None, TM, LANES), orow)],
        out_shape=[jax.ShapeDtypeStruct((b, n_out, d), F32),
                   jax.ShapeDtypeStruct((b, n_out * SUBLANES, LANES), F32),
                   jax.ShapeDtypeStruct((b, n_t * SUBLANES, TM), F32),
                   jax.ShapeDtypeStruct((b, n_t * SUBLANES, LANES), F32),
                   jax.ShapeDtypeStruct((b, n_out, LANES), F32)],
        compiler_params=_cparams(("parallel", "arbitrary")),
        name="out_proj",
    )(xc, da, mhf, mhb, mo, ghf, ghb, gr, modtab, ln_g, ln_b, ml_g, gl_g, wo_p, w_route)


def _tile_copy(src, src_tile, dst, dst_tile, sem):
    aligned = lambda t: t if isinstance(t, int) else pl.multiple_of(t, SUBLANES)
    return pltpu.make_async_copy(src.at[pl.ds(aligned(src_tile), SUBLANES), :],
                                 dst.at[pl.ds(aligned(dst_tile), SUBLANES), :], sem)


def _rows_from_tiles(ref, n_rows):
    return jnp.concatenate([ref[pl.ds(s, n_rows, stride=SUBLANES), :] for s in range(SUBLANES)], axis=1)


def _rows_to_tiles(ref, val):
    n_rows = val.shape[0]
    for s in range(SUBLANES):
        ref[pl.ds(s, n_rows, stride=SUBLANES), :] = val[:, s * LANES:(s + 1) * LANES]


def _disp_kernel(dest_ref, cnt_ref, start_ref, padded_ref, f_ref, xs_hbm, zero_ref, sem, zsem):
    bb = pl.program_id(0)
    i = pl.program_id(1)
    tm = f_ref.shape[0] // SUBLANES
    n_tok = pl.num_programs(0) * pl.num_programs(1) * tm
    base = (bb * pl.num_programs(1) + i) * tm

    def scatter(start):
        def body(r, carry):
            for kk in range(2):
                cp = _tile_copy(f_ref, r * SUBLANES, xs_hbm, dest_ref[kk * n_tok + base + r], sem)
                if start:
                    cp.start()
                else:
                    cp.wait()
            return carry
        lax.fori_loop(0, tm, body, 0, unroll=8)

    scatter(True)
    scatter(False)

    @pl.when(jnp.logical_and(bb == pl.num_programs(0) - 1, i == pl.num_programs(1) - 1))
    def _():
        zero_ref[...] = jnp.zeros_like(zero_ref)

        def fill(start):
            def per_expert(e, carry):
                def body(r, c2):
                    cp = _tile_copy(zero_ref, 0, xs_hbm, (start_ref[e] + r) * SUBLANES, zsem)
                    if start:
                        cp.start()
                    else:
                        cp.wait()
                    return c2
                return lax.fori_loop(cnt_ref[e], padded_ref[e], body, carry)
            lax.fori_loop(0, MOE_EXPERTS, per_expert, 0)

        fill(True)
        fill(False)


def _dispatch(dest8, counts, pad_start, padded, f3, n_slots):
    b, rows, _ = f3.shape
    n_t = rows // (TM * SUBLANES)
    return pl.pallas_call(
        _disp_kernel,
        grid_spec=pltpu.PrefetchScalarGridSpec(
            num_scalar_prefetch=4,
            grid=(b, n_t),
            in_specs=[pl.BlockSpec((TM * SUBLANES, LANES), lambda bb, i, *_: (bb * n_t + i, 0))],
            out_specs=pl.BlockSpec(memory_space=pl.ANY),
            scratch_shapes=[pltpu.VMEM((SUBLANES, LANES), F32), pltpu.SemaphoreType.DMA(()),
                            pltpu.SemaphoreType.DMA(())],
        ),
        out_shape=jax.ShapeDtypeStruct((n_slots * SUBLANES, LANES), F32),
        compiler_params=_cparams(("arbitrary", "arbitrary")),
        name="moe_dispatch",
    )(dest8, counts, pad_start, padded, f3.reshape(b * rows, LANES))


def _moe_kernel(be_ref, nu_ref, xs_ref, w1_ref, w3_ref, w2_ref, y_ref, w1b, w3b, w2b):
    j = pl.program_id(0)
    n_used = nu_ref[0]
    blk = xs_ref.shape[0] // SUBLANES

    @pl.when(jnp.logical_and(j < n_used, jnp.logical_or(j == 0, be_ref[j] != be_ref[jnp.maximum(j - 1, 0)])))
    def _():
        w1b[...] = w1_ref[...].astype(BF16)
        w3b[...] = w3_ref[...].astype(BF16)
        w2b[...] = w2_ref[...].astype(BF16)

    @pl.when(j < n_used)
    def _():
        x = _rows_from_tiles(xs_ref, blk).astype(BF16)
        hidden = _silu(_dot(x, w1b[...])) * _dot(x, w3b[...])
        _rows_to_tiles(y_ref, _dot(hidden.astype(BF16), w2b[...]))

    @pl.when(j >= n_used)
    def _():
        y_ref[...] = jnp.zeros_like(y_ref)


def _moe(xs, block_e, n_used, w1, w3, w2):
    n_blocks = block_e.shape[0]
    d, hid = w1.shape[-2:]
    wmap = lambda j, be, nu: (be[j], 0, 0)
    tiles = MOE_BLOCK * SUBLANES
    return pl.pallas_call(
        _moe_kernel,
        grid_spec=pltpu.PrefetchScalarGridSpec(
            num_scalar_prefetch=2,
            grid=(n_blocks,),
            in_specs=[
                pl.BlockSpec((tiles, LANES), lambda j, be, nu: (jnp.minimum(j, jnp.maximum(nu[0] - 1, 0)), 0)),
                pl.BlockSpec((None, d, hid), wmap),
                pl.BlockSpec((None, d, hid), wmap),
                pl.BlockSpec((None, hid, d), wmap),
            ],
            out_specs=pl.BlockSpec((tiles, LANES), lambda j, be, nu: (j, 0)),
            scratch_shapes=[pltpu.VMEM((d, hid), BF16), pltpu.VMEM((d, hid), BF16), pltpu.VMEM((hid, d), BF16)],
        ),
        out_shape=jax.ShapeDtypeStruct((n_blocks * tiles, LANES), F32),
        compiler_params=_cparams(("arbitrary",)),
        name="moe_experts",
    )(block_e, n_used, xs, w1, w3, w2)


def _comb_kernel(dn_alpha, dest_ref, y_hbm, x1_ref, rw_ref, mod_ref, lng_ref, lnb_ref, o_ref, ybuf, sems):
    bb = pl.program_id(0)
    i = pl.program_id(1)
    tm = x1_ref.shape[0]
    n_tok = pl.num_programs(0) * pl.num_programs(1) * tm
    base = (bb * pl.num_programs(1) + i) * tm

    def gather(start):
        def body(r, carry):
            for kk in range(2):
                cp = _tile_copy(y_hbm, dest_ref[kk * n_tok + base + r], ybuf.at[kk], r * SUBLANES, sems.at[kk])
                if start:
                    cp.start()
                else:
                    cp.wait()
            return carry
        lax.fori_loop(0, tm, body, 0, unroll=8)

    gather(True)
    gather(False)
    rw = rw_ref[...]
    y = rw[:, 0:1] * _rows_from_tiles(ybuf.at[0], tm) + rw[:, 1:2] * _rows_from_tiles(ybuf.at[1], tm)
    gate2 = mod_ref[5:6, :]
    o_ref[...] = _ln(dn_alpha * x1_ref[...] + gate2 * y) * lng_ref[...] + lnb_ref[...]


def _combine(dest, y_slots, x1, rw, modtab, ln_g, ln_b, dn_alpha, seg_of_tile):
    b, n, d = x1.shape
    n_t = n // TM
    row = lambda bb, i, dr: (bb, i, 0)
    const2 = lambda bb, i, dr: (0, 0)
    return pl.pallas_call(
        functools.partial(_comb_kernel, dn_alpha),
        grid_spec=pltpu.PrefetchScalarGridSpec(
            num_scalar_prefetch=1,
            grid=(b, n_t),
            in_specs=[
                pl.BlockSpec(memory_space=pl.ANY),
                pl.BlockSpec((None, TM, d), row),
                pl.BlockSpec((None, TM, LANES), row),
                pl.BlockSpec((None, None, 6, d), lambda bb, i, dr: (bb, seg_of_tile(i), 0, 0)),
                pl.BlockSpec((1, d), const2), pl.BlockSpec((1, d), const2),
            ],
            out_specs=pl.BlockSpec((None, TM, d), row),
            scratch_shapes=[pltpu.VMEM((2, TM * SUBLANES, LANES), F32), pltpu.SemaphoreType.DMA((2,))],
        ),
        out_shape=jax.ShapeDtypeStruct((b, n, d), F32),
        compiler_params=_cparams(("arbitrary", "arbitrary")),
        name="moe_combine",
    )(dest, y_slots, x1, rw, modtab, ln_g, ln_b)


def _dispatch_tables(info, tile_counts, n_blocks):
    b, n_t, _, tm = info.shape
    info = info.astype(jnp.int32)
    expert = jnp.moveaxis(info[:, :, 0:2, :], 2, 0).reshape(2, b * n_t, tm)
    rank = jnp.moveaxis(info[:, :, 2:4, :], 2, 0).reshape(2, b * n_t, tm)
    counts = jnp.sum(tile_counts, axis=0)
    padded = (counts + MOE_BLOCK - 1) // MOE_BLOCK * MOE_BLOCK
    pad_end = jnp.cumsum(padded)
    pad_start = pad_end - padded
    tile_base = pad_start[None, :] + jnp.cumsum(tile_counts, axis=0) - tile_counts
    onehot = expert[..., None] == jnp.arange(MOE_EXPERTS, dtype=jnp.int32)
    dest = jnp.sum(jnp.where(onehot, tile_base[None, :, None, :], 0), axis=-1) + rank
    dest8 = (dest * SUBLANES).reshape(-1).astype(jnp.int32)
    block_start = jnp.arange(n_blocks, dtype=jnp.int32) * MOE_BLOCK
    block_e = jnp.minimum(jnp.sum(pad_end[None, :] <= block_start[:, None], axis=1), MOE_EXPERTS - 1).astype(jnp.int32)
    n_used = (pad_end[-1] // MOE_BLOCK).astype(jnp.int32).reshape(1)
    i32 = lambda v: v.astype(jnp.int32)
    return dest8, i32(counts), i32(pad_start), i32(padded), block_e, n_used


def _pad_heads(w, width):
    lead = w.shape[:-1]
    w = w.reshape(*lead, HEADS, width)
    w = jnp.pad(w, [(0, 0)] * len(lead) + [(0, 0), (0, LANES - width)])
    return w.reshape(*lead, PAD_W)


def _expand_matrix(width):
    e = np.zeros((HEADS * width, PAD_W), np.float32)
    for h in range(HEADS):
        e[h * width + np.arange(width), h * LANES + np.arange(width)] = 1.0
    return jnp.asarray(e, BF16)


def _rope_tables(n_ctx, n_lat):
    n_freq = DA_DH // 4
    inv_freq = ROPE_BASE ** (-jnp.arange(n_freq, dtype=F32) / n_freq)
    pos = jnp.arange(n_lat, dtype=jnp.int32)
    ang_r = (pos // GRID_W).astype(F32)[:, None] * inv_freq
    ang_c = (pos % GRID_W).astype(F32)[:, None] * inv_freq
    ang = jnp.concatenate([ang_r, ang_r, ang_c, ang_c] * 2, axis=-1)
    cos, sin = jnp.cos(ang), jnp.sin(ang)
    first = (jnp.arange(LANES) % 32) < 16
    sin_a = jnp.where(first, -sin, 0.0)
    sin_b = jnp.where(first, 0.0, sin)
    ctx_rows = lambda v: jnp.full((n_ctx, LANES), v, F32)
    return jnp.stack([jnp.concatenate([ctx_rows(1.0), cos]), jnp.concatenate([ctx_rows(0.0), sin_a]),
                      jnp.concatenate([ctx_rows(0.0), sin_b])])


def kernel(x, c, ctx, c_ctx, w_ada, b_ada, w_in, da_lambda, da_norm, ml_conv_w, ml_conv_b, ml_ib, ml_fb,
           ml_norm, gl_wa, gl_ba, gl_norm, w_out, ln_mix_g, ln_mix_b, ln_ffn_g, ln_ffn_b,
           moe_wg, moe_we, moe_w1, moe_w3, moe_w2):
    b, s, d = x.shape
    n_ctx = ctx.shape[1]
    n = n_ctx + s
    depth = w_in.shape[0]
    assert n_ctx % TM == 0 and s % TM == 0 and d % LANES == 0
    dn_alpha = (2 * depth) ** 0.25

    rows = -(-(b + 1) // SUBLANES) * SUBLANES
    cc = jnp.concatenate([c, c_ctx[None, :], jnp.zeros((rows - b - 1, d), F32)], axis=0)
    mod = _ada(cc, w_ada, b_ada).reshape(depth, rows, 6, d)

    xc = jnp.concatenate([ctx, x], axis=1)
    rope = _rope_tables(n_ctx, s)
    e64, e32 = _expand_matrix(64), _expand_matrix(32)

    for l in range(depth):
        last = l == depth - 1
        tile0 = n_ctx // TM if last else 0
        lam_init = 0.8 - 0.6 * math.exp(-0.3 * l)
        modtab = jnp.stack([jnp.broadcast_to(mod[l, b], (b, 6, d)), mod[l, :b]], axis=1)

        wi = w_in[l]
        o = np.cumsum([0, 512, 512, 512, 512, 256, 256, 8, 8, 128, 128, 256, 256, 32])
        misc_w = jnp.concatenate([wi[:, o[6]:o[8]], wi[:, o[12]:o[13]],
                                  jnp.zeros((d, LANES - 16 - 2 * GL_RANK), F32)], axis=1)
        w_in_p = jnp.concatenate([wi[:, :o[6]], wi[:, o[8]:o[12]], misc_w], axis=1).astype(BF16)

        (qa, ka, va, mq, mk, mv, mo, gq, gk, gv, gr, misc) = _in_proj(
            xc, modtab, w_in_p, rope, ml_conv_w[l], ml_conv_b[l].reshape(1, -1), e64, e32, n_ctx)

        da = _attention(qa, ka, va, da_lambda[l], da_norm[l].reshape(1, -1), lam_init, n_ctx, tile0)

        gate_bias = jnp.concatenate([ml_ib[l].reshape(-1), ml_fb[l].reshape(-1),
                                     jnp.zeros((LANES - 16,), F32)]).reshape(1, LANES)
        mhf, mhb = _mlstm(mq, mk, mv, misc, gate_bias, n_ctx)

        wa_p = jnp.zeros((2, LANES, PAD_W), F32)
        for dd in range(2):
            wa_p = wa_p.at[dd, M_GA + dd * GL_RANK:M_GA + (dd + 1) * GL_RANK].set(_pad_heads(gl_wa[l, dd], GL_DK))
        ba_p = _pad_heads(gl_ba[l], GL_DK).reshape(2, 1, PAD_W)
        ghf, ghb = _gla(gq, gk, gv, misc, wa_p, ba_p, n_ctx)

        wo = w_out[l]
        wo_p = jnp.concatenate([
            wo[:512],
            jnp.pad(wo[512:768].reshape(HEADS, 64, d), ((0, 0), (0, 64), (0, 0))).reshape(PAD_W, d),
            jnp.pad(wo[768:1024].reshape(HEADS, 64, d), ((0, 0), (0, 64), (0, 0))).reshape(PAD_W, d),
        ], axis=0).astype(BF16)
        w_route = jnp.concatenate([moe_wg[l], moe_we[l],
                                   jnp.zeros((d, LANES - MOE_GROUPS - MOE_EXPERTS), F32)], axis=1)
        ml_g = _pad_heads(jnp.tile(ml_norm[l], HEADS), ML_DH).reshape(1, PAD_W)
        gl_g = _pad_heads(jnp.tile(gl_norm[l], HEADS), GL_DV).reshape(1, PAD_W)
        x1, f3, info, cnt, rw = _out_proj(xc, da, mhf, mhb, mo, ghf, ghb, gr, modtab,
                                          ln_mix_g[l].reshape(1, d), ln_mix_b[l].reshape(1, d), ml_g, gl_g,
                                          wo_p, w_route, dn_alpha, n_ctx, tile0)

        n_rows = x1.shape[1]
        n_t = n_rows // TM
        a_all = b * n_rows * 2
        n_blocks = -(-(a_all + MOE_EXPERTS * (MOE_BLOCK - 1)) // MOE_BLOCK)
        tile_counts = cnt.reshape(b * n_t, SUBLANES, LANES)[:, 0, MOE_GROUPS:MOE_GROUPS + MOE_EXPERTS]
        dest8, counts, pad_start, padded, block_e, n_used = _dispatch_tables(
            info.reshape(b, n_t, SUBLANES, TM), tile_counts.astype(jnp.int32), n_blocks)
        xs = _dispatch(dest8, counts, pad_start, padded, f3, n_blocks * MOE_BLOCK)
        y_slots = _moe(xs, block_e, n_used, moe_w1[l], moe_w3[l], moe_w2[l])
        n_ctx_tiles_here = 0 if last else n_ctx // TM
        seg = lambda i: jnp.where(i >= n_ctx_tiles_here, 1, 0)
        xc = _combine(dest8, y_slots, x1, rw, modtab, ln_ffn_g[l].reshape(1, d), ln_ffn_b[l].reshape(1, d),
                      dn_alpha, seg)
    return xc
```

```python
import functools
import math

import numpy as np
import jax
import jax.numpy as jnp
from jax import lax
from jax.experimental import pallas as pl
from jax.experimental.pallas import tpu as pltpu

F32 = jnp.float32
BF16 = jnp.bfloat16
HI = lax.Precision.HIGHEST

GRID_W = 64
HEADS = 4
DA_DH = 64
ML_DH = 64
GL_DK = 32
GL_DV = 64
GL_RANK = 16
GL_NORMALIZER = 16.0
CHUNK = 64
SUB = 16
ROPE_BASE = 10000.0
MOE_GROUPS = 4
MOE_PER_GROUP = 8
MOE_EXPERTS = 32
MOE_HIDDEN = 512
LN_EPS = 1e-6

LANES = 128
SUBLANES = 8
VMEM_LIMIT = 48 * 1024 * 1024

TM = 256
MOE_BLOCK = 256
PAD_W = HEADS * LANES

C_QA, C_KA, C_VA, C_MQK, C_MV, C_MO = 0, 512, 1024, 1536, 2048, 2304
C_GQ, C_GK, C_GV, C_GR, C_MISC, IN_WP = 2560, 2688, 2816, 3072, 3328, 3456
M_IG, M_FG, M_GA = 0, 8, 16


def _cparams(sem):
    return pltpu.CompilerParams(dimension_semantics=sem, vmem_limit_bytes=VMEM_LIMIT)


def _sigmoid(x):
    return 1.0 / (1.0 + jnp.exp(-x))


def _silu(x):
    return x * _sigmoid(x)


def _log_sigmoid(x):
    return jnp.minimum(x, 0.0) - jnp.log(1.0 + jnp.exp(-jnp.abs(x)))


def _ln(x):
    mu = jnp.mean(x, axis=-1, keepdims=True)
    xc = x - mu
    var = jnp.mean(xc * xc, axis=-1, keepdims=True)
    return xc * lax.rsqrt(var + LN_EPS)


def _bf16_part(x):
    bits = lax.bitcast_convert_type(x, jnp.int32) & jnp.int32(-65536)
    return lax.bitcast_convert_type(bits, F32)


def _dot(a, b):
    return jnp.dot(a, b, preferred_element_type=F32)


def _dot_hi(a, b):
    return jnp.dot(a, b, precision=HI, preferred_element_type=F32)


def _dot_nt(a, b):
    return lax.dot_general(a, b, (((1,), (1,)), ((), ())), preferred_element_type=F32)


def _dot_tn(a, b):
    return lax.dot_general(a, b, (((0,), (0,)), ((), ())), preferred_element_type=F32)


def _ada_kernel(c_ref, w_ref, b_ref, o_ref):
    o_ref[...] = _dot_hi(_silu(c_ref[...]), w_ref[...]) + b_ref[...]


def _ada(cc, w_ada, b_ada):
    depth, d, n6 = w_ada.shape
    tn = 1536
    rows = cc.shape[0]
    return pl.pallas_call(
        _ada_kernel,
        grid=(depth, n6 // tn),
        in_specs=[
            pl.BlockSpec((rows, d), lambda l, j: (0, 0)),
            pl.BlockSpec((None, d, tn), lambda l, j: (l, 0, j)),
            pl.BlockSpec((None, 1, tn), lambda l, j: (l, 0, j)),
        ],
        out_specs=pl.BlockSpec((None, rows, tn), lambda l, j: (l, 0, j)),
        out_shape=jax.ShapeDtypeStruct((depth, rows, n6), F32),
        compiler_params=_cparams(("arbitrary", "arbitrary")),
        name="ada_mod",
    )(cc, w_ada, b_ada.reshape(depth, 1, n6))


def _in_kernel(n_ctx_tiles, x_ref, xp_ref, xn_ref, mod_ref, w_ref, rope_ref, cw_ref, cb_ref,
               e64_ref, e32_ref,
               qa_ref, ka_ref, va_ref, mq_ref, mk_ref, mv_ref, mo_ref,
               gq_ref, gk_ref, gv_ref, gr_ref, misc_ref, pext_ref):
    i = pl.program_id(1)
    n_tiles = pl.num_programs(1)
    tm = x_ref.shape[0]
    shift, scale = mod_ref[0:1, :], mod_ref[1:2, :]

    def prenorm(x):
        return (_ln(x) * (1.0 + scale) + shift).astype(BF16)

    h = prenorm(x_ref[...])

    def proj(a, lo, hi):
        return _dot(a, w_ref[:, lo:hi])

    cos, sin_a, sin_b = rope_ref[0], rope_ref[1], rope_ref[2]
    for lo, out_ref, mult in ((C_QA, qa_ref, DA_DH ** -0.5), (C_KA, ka_ref, 1.0)):
        acc = proj(h, lo, lo + PAD_W)
        for hh in range(HEADS):
            xh = acc[:, hh * LANES:(hh + 1) * LANES]
            r = xh * cos + pltpu.roll(xh, LANES - 16, 1) * sin_a + pltpu.roll(xh, 16, 1) * sin_b
            out_ref[:, hh * LANES:(hh + 1) * LANES] = (r * mult).astype(BF16)
    va_ref[...] = proj(h, C_VA, C_VA + PAD_W).astype(BF16)

    seg_first = jnp.logical_or(i == 0, i == n_ctx_tiles)
    seg_last = jnp.logical_or(i == n_ctx_tiles - 1, i == n_tiles - 1)
    keep_prev = jnp.where(seg_first, 0.0, 1.0)
    keep_next = jnp.where(seg_last, 0.0, 1.0)
    p = proj(h, C_MQK, C_MQK + 512)
    pext_ref[0:SUBLANES, :] = proj(prenorm(xp_ref[...]), C_MQK, C_MQK + 512) * keep_prev
    pext_ref[SUBLANES:SUBLANES + tm, :] = p
    pext_ref[SUBLANES + tm:2 * SUBLANES + tm, :] = proj(prenorm(xn_ref[...]), C_MQK, C_MQK + 512) * keep_next
    y = (cw_ref[0:1, :] * pext_ref[SUBLANES - 1:SUBLANES - 1 + tm, :] + cw_ref[1:2, :] * p
         + cw_ref[2:3, :] * pext_ref[SUBLANES + 1:SUBLANES + 1 + tm, :] + cb_ref[...])
    y = _silu(y)
    e64 = e64_ref[...]
    e32 = e32_ref[...]
    mq_ref[...] = _dot((y[:, :256] * ML_DH ** -0.5).astype(BF16), e64).astype(BF16)
    mk_ref[...] = _dot(y[:, 256:].astype(BF16), e64).astype(BF16)
    lane = lax.broadcasted_iota(jnp.int32, (tm, PAD_W), 1)
    ones_col = jnp.where(lane % LANES == ML_DH, 1.0, 0.0)
    mv_ref[...] = (_dot(proj(h, C_MV, C_MV + 256).astype(BF16), e64) + ones_col).astype(BF16)
    mo_ref[...] = _dot(proj(h, C_MO, C_MO + 256).astype(BF16), e64).astype(BF16)

    gq_ref[...] = _dot((proj(h, C_GQ, C_GQ + 128) * GL_DK ** -0.5).astype(BF16), e32).astype(BF16)
    gk_ref[...] = _dot(proj(h, C_GK, C_GK + 128).astype(BF16), e32).astype(BF16)
    gv_ref[...] = _dot(proj(h, C_GV, C_GV + 256).astype(BF16), e64).astype(BF16)
    gr_ref[...] = _dot(proj(h, C_GR, C_GR + 256).astype(BF16), e64).astype(BF16)
    misc_ref[...] = proj(h, C_MISC, C_MISC + LANES)


def _in_proj(xc, modtab, w_in_p, rope, conv_w, conv_b, e64, e32, n_ctx):
    b, n, d = xc.shape
    n_tiles = n // TM
    n_ctx_tiles = n_ctx // TM
    hb = TM // SUBLANES
    n_hb = n // SUBLANES
    row = lambda bb, i: (bb, i, 0)
    const2 = lambda bb, i: (0, 0)
    outs = [jax.ShapeDtypeStruct((b, n, PAD_W), BF16)] * 11 + [jax.ShapeDtypeStruct((b, n, LANES), F32)]
    out_specs = [pl.BlockSpec((None, TM, PAD_W), row)] * 11 + [pl.BlockSpec((None, TM, LANES), row)]
    return pl.pallas_call(
        functools.partial(_in_kernel, n_ctx_tiles),
        grid=(b, n_tiles),
        in_specs=[
            pl.BlockSpec((None, TM, d), row),
            pl.BlockSpec((None, SUBLANES, d), lambda bb, i: (bb, jnp.maximum(i * hb - 1, 0), 0)),
            pl.BlockSpec((None, SUBLANES, d), lambda bb, i: (bb, jnp.minimum((i + 1) * hb, n_hb - 1), 0)),
            pl.BlockSpec((None, None, 6, d), lambda bb, i: (bb, jnp.where(i >= n_ctx_tiles, 1, 0), 0, 0)),
            pl.BlockSpec((d, IN_WP), const2),
            pl.BlockSpec((3, TM, LANES), lambda bb, i: (0, i, 0)),
            pl.BlockSpec((3, 512), const2),
            pl.BlockSpec((1, 512), const2),
            pl.BlockSpec((256, PAD_W), const2),
            pl.BlockSpec((128, PAD_W), const2),
        ],
        out_specs=out_specs,
        out_shape=outs,
        scratch_shapes=[pltpu.VMEM((TM + 2 * SUBLANES, 512), F32)],
        compiler_params=_cparams(("parallel", "arbitrary")),
        name="in_proj",
    )(xc, xc, xc, modtab, w_in_p, rope, conv_w, conv_b, e64, e32)


def _attn_kernel(lam_init, n_ctx, tile0, q_ref, k_ref, v_ref, lamp_ref, g_ref, o_ref):
    i = pl.program_id(2) + tile0
    lp = lamp_ref[...]
    lam = (jnp.exp(jnp.sum(lp[0:1] * lp[1:2], axis=1, keepdims=True))
           - jnp.exp(jnp.sum(lp[2:3] * lp[3:4], axis=1, keepdims=True)) + lam_init)
    q = q_ref[...]
    lane = lax.broadcasted_iota(jnp.int32, q.shape, 1)
    zero = jnp.zeros_like(q)
    q1 = jnp.where(lane < DA_DH, q, zero)
    q2 = jnp.where(lane >= DA_DH, q, zero)

    def attend(nk):
        k = k_ref[0:nk, :]
        v = v_ref[0:nk, :]

        def softmax_parts(qm):
            s = _dot_nt(qm, k)
            e = jnp.exp(s - jnp.max(s, axis=1, keepdims=True))
            return e, 1.0 / jnp.sum(e, axis=1, keepdims=True)

        e1, r1 = softmax_parts(q1)
        e2, r2 = softmax_parts(q2)
        a = e1 * r1 - e2 * (r2 * lam)
        o = _dot(a.astype(BF16), v)
        y = o * lax.rsqrt(jnp.mean(o * o, axis=1, keepdims=True) + LN_EPS) * g_ref[...]
        o_ref[...] = (y * (1.0 - lam_init)).astype(BF16)

    n_ctx_tiles = n_ctx // q_ref.shape[0]

    @pl.when(i < n_ctx_tiles)
    def _():
        attend(n_ctx)

    @pl.when(i >= n_ctx_tiles)
    def _():
        attend(k_ref.shape[0])


def _attention(qa, ka, va, lam_p, norm_g, lam_init, n_ctx, tile0):
    b, n, _ = qa.shape
    tq = TM
    n_q = n // tq - tile0
    return pl.pallas_call(
        functools.partial(_attn_kernel, lam_init, n_ctx, tile0),
        grid=(b, HEADS, n_q),
        in_specs=[
            pl.BlockSpec((None, tq, LANES), lambda bb, h, i: (bb, i + tile0, h)),
            pl.BlockSpec((None, n, LANES), lambda bb, h, i: (bb, 0, h)),
            pl.BlockSpec((None, n, LANES), lambda bb, h, i: (bb, 0, h)),
            pl.BlockSpec((4, DA_DH), lambda bb, h, i: (0, 0)),
            pl.BlockSpec((1, LANES), lambda bb, h, i: (0, 0)),
        ],
        out_specs=pl.BlockSpec((None, tq, LANES), lambda bb, h, i: (bb, i, h)),
        out_shape=jax.ShapeDtypeStruct((b, n_q * tq, PAD_W), BF16),
        compiler_params=_cparams(("parallel", "parallel", "arbitrary")),
        name="diff_attn",
    )(qa, ka, va, lam_p, norm_g)


def _chunk_maps(n_ctx_chunks, n_chunks):
    def fwd(bb, t):
        return (bb, t, 0)

    def bwd(bb, t):
        return (bb, jnp.where(t < n_ctx_chunks, n_ctx_chunks - 1 - t, n_chunks - 1 + n_ctx_chunks - t), 0)

    return fwd, bwd


def _tri(direction):
    l = lax.broadcasted_iota(jnp.int32, (CHUNK, CHUNK), 0)
    s = lax.broadcasted_iota(jnp.int32, (CHUNK, CHUNK), 1)
    return (s <= l) if direction == 0 else (s >= l)


def _mlstm_kernel(qf_ref, kf_ref, vf_ref, gf_ref, qb_ref, kb_ref, vb_ref, gb_ref, bias_ref,
                  hf_ref, hb_ref, c_ref, m_ref):
    t = pl.program_id(1)

    @pl.when(t == 0)
    def _():
        c_ref[...] = jnp.zeros_like(c_ref)
        m_ref[...] = jnp.zeros_like(m_ref)

    lane = lax.broadcasted_iota(jnp.int32, (CHUNK, LANES), 1)
    is_fg = jnp.logical_and(lane >= M_FG, lane < M_GA)
    for d, (q_ref, k_ref, v_ref, g_ref, out_ref) in enumerate(
            ((qf_ref, kf_ref, vf_ref, gf_ref, hf_ref), (qb_ref, kb_ref, vb_ref, gb_ref, hb_ref))):
        mask = _tri(d)
        tri = jnp.where(mask, 1.0, 0.0)
        last = CHUNK - 1 if d == 0 else 0
        g = g_ref[...] + bias_ref[...]
        g = jnp.where(is_fg, _log_sigmoid(g), g)
        cum = _dot_hi(tri, g)
        g_t = g.T
        cum_t = cum.T
        for h in range(HEADS):
            ci, cf = M_IG + HEADS * d + h, M_FG + HEADS * d + h
            hs = slice(h * LANES, (h + 1) * LANES)
            q, k, v = q_ref[:, hs], k_ref[:, hs], v_ref[:, hs]
            bcol = cum[:, cf:cf + 1]
            icol = g[:, ci:ci + 1]
            row = g_t[ci:ci + 1, :] - cum_t[cf:cf + 1, :]
            m_prev = m_ref[d, h][0:1, 0:1]
            c_prev = c_ref[d, h]
            log_d = jnp.where(mask, bcol + row, -jnp.inf)
            log_a = bcol + m_prev
            m_t = jnp.maximum(log_a, jnp.max(log_d, axis=1, keepdims=True))
            dmat = jnp.exp(log_d - m_t)
            a = jnp.exp(log_a - m_t)
            s = _dot_nt(q, k) * dmat
            num = a * _dot(q, c_prev.astype(BF16)) + _dot(s.astype(BF16), v)
            qn = num[:, ML_DH:ML_DH + 1]
            den = jnp.maximum(jnp.abs(qn), jnp.exp(-m_t))
            out_ref[:, hs] = jnp.where(lane < ML_DH, num / den, 0.0)
            m_new = m_t[last:last + 1, :]
            b_last = bcol[last:last + 1, :]
            w = jnp.exp(b_last - bcol + icol - m_new)
            decay = jnp.exp(b_last + m_prev - m_new)
            kw = (k.astype(F32) * w).astype(BF16)
            c_ref[d, h] = decay * c_prev + _dot_tn(kw, v)
            m_ref[d, h] = jnp.broadcast_to(m_new, (SUBLANES, LANES))


def _mlstm(mq, mk, mv, misc, gate_bias, n_ctx):
    b, n, _ = mq.shape
    n_chunks = n // CHUNK
    fwd, bwd = _chunk_maps(n_ctx // CHUNK, n_chunks)
    wide = lambda m: pl.BlockSpec((None, CHUNK, PAD_W), m)
    thin = lambda m: pl.BlockSpec((None, CHUNK, LANES), m)
    return pl.pallas_call(
        _mlstm_kernel,
        grid=(b, n_chunks),
        in_specs=[wide(fwd), wide(fwd), wide(fwd), thin(fwd), wide(bwd), wide(bwd), wide(bwd), thin(bwd),
                  pl.BlockSpec((1, LANES), lambda bb, t: (0, 0))],
        out_specs=[wide(fwd), wide(bwd)],
        out_shape=[jax.ShapeDtypeStruct((b, n, PAD_W), F32)] * 2,
        scratch_shapes=[pltpu.VMEM((2, HEADS, LANES, LANES), F32),
                        pltpu.VMEM((2, HEADS, SUBLANES, LANES), F32)],
        compiler_params=_cparams(("parallel", "arbitrary")),
        name="mlstm_scan",
    )(mq, mk, mv, misc, mq, mk, mv, misc, gate_bias)


def _gla_kernel(qf_ref, kf_ref, vf_ref, gf_ref, qb_ref, kb_ref, vb_ref, gb_ref, wa_ref, ba_ref,
                hf_ref, hb_ref, st_ref):
    t = pl.program_id(1)

    @pl.when(t == 0)
    def _():
        st_ref[...] = jnp.zeros_like(st_ref)

    n_sub = CHUNK // SUB
    row_id = lax.broadcasted_iota(jnp.int32, (CHUNK, LANES), 0)
    row_blk = row_id // SUB
    col_id = lax.broadcasted_iota(jnp.int32, (CHUNK, LANES), 1)
    for d, (q_ref, k_ref, v_ref, g_ref, out_ref) in enumerate(
            ((qf_ref, kf_ref, vf_ref, gf_ref, hf_ref), (qb_ref, kb_ref, vb_ref, gb_ref, hb_ref))):
        tri = jnp.where(_tri(d), 1.0, 0.0)
        causal = (col_id <= row_id) if d == 0 else jnp.logical_and(col_id >= row_id, col_id < CHUNK)
        last = CHUNK - 1 if d == 0 else 0
        z = _dot_hi(g_ref[...], wa_ref[d]) + ba_ref[d]
        la = _log_sigmoid(z) * (1.0 / GL_NORMALIZER)
        cum = _dot_hi(tri, la)
        excl = cum - la
        firsts = [I * SUB if d == 0 else I * SUB + SUB - 1 for I in range(n_sub)]
        betas = [excl[r:r + 1, :] for r in firsts]
        beta_rows = jnp.concatenate([jnp.broadcast_to(bt, (SUB, PAD_W)) for bt in betas], axis=0)
        q = q_ref[...].astype(F32)
        k = k_ref[...].astype(F32)
        q_loc = (q * jnp.exp(cum - beta_rows)).astype(BF16)
        q_abs = (q * jnp.exp(cum)).astype(BF16)
        b_last = cum[last:last + 1, :]
        k_end = (k * jnp.exp(b_last - cum)).astype(BF16)
        rows_w = lax.broadcasted_iota(jnp.int32, (CHUNK, PAD_W), 0)
        k_subs = []
        for I in range(n_sub):
            seen = (rows_w < (I + 1) * SUB) if d == 0 else (rows_w >= I * SUB)
            k_subs.append((k * jnp.exp(jnp.where(seen, betas[I] - cum, -jnp.inf))).astype(BF16))
        zeros = jnp.zeros((CHUNK, LANES), BF16)
        for h in range(HEADS):
            hs = slice(h * LANES, (h + 1) * LANES)
            v = v_ref[:, hs]
            st = st_ref[d, h]
            kcat = jnp.concatenate([x for I in range(n_sub) for x in (k_subs[I][:, hs], zeros)], axis=0)
            r = _dot_nt(q_loc[:, hs], kcat)
            att = r[:, 0:LANES]
            for I in range(1, n_sub):
                att = jnp.where(row_blk == I, r[:, I * LANES:(I + 1) * LANES], att)
            att = jnp.where(causal, att, 0.0)
            out = _dot_nt(q_abs[:, hs], st.astype(BF16)) + _dot(att[:, 0:CHUNK].astype(BF16), v)
            out_ref[:, hs] = out
            st_ref[d, h] = st * jnp.exp(b_last[:, hs]) + _dot_tn(v, k_end[:, hs])


def _gla(gq, gk, gv, misc, wa_p, ba_p, n_ctx):
    b, n, _ = gq.shape
    n_chunks = n // CHUNK
    fwd, bwd = _chunk_maps(n_ctx // CHUNK, n_chunks)
    wide = lambda m: pl.BlockSpec((None, CHUNK, PAD_W), m)
    thin = lambda m: pl.BlockSpec((None, CHUNK, LANES), m)
    return pl.pallas_call(
        _gla_kernel,
        grid=(b, n_chunks),
        in_specs=[wide(fwd), wide(fwd), wide(fwd), thin(fwd), wide(bwd), wide(bwd), wide(bwd), thin(bwd),
                  pl.BlockSpec((2, LANES, PAD_W), lambda bb, t: (0, 0, 0)),
                  pl.BlockSpec((2, 1, PAD_W), lambda bb, t: (0, 0, 0))],
        out_specs=[wide(fwd), wide(bwd)],
        out_shape=[jax.ShapeDtypeStruct((b, n, PAD_W), F32)] * 2,
        scratch_shapes=[pltpu.VMEM((2, HEADS, LANES, LANES), F32)],
        compiler_params=_cparams(("parallel", "arbitrary")),
        name="gla_scan",
    )(gq, gk, gv, misc, gq, gk, gv, misc, wa_p, ba_p)


def _head_rms_padded(x, width):
    parts = []
    for h in range(HEADS):
        xh = x[:, h * LANES:(h + 1) * LANES]
        ms = jnp.sum(xh * xh, axis=1, keepdims=True) * (1.0 / width)
        parts.append(xh * lax.rsqrt(ms + LN_EPS))
    return jnp.concatenate(parts, axis=1)


def _out_kernel(dn_alpha, x_ref, da_ref, mhf_ref, mhb_ref, mo_ref, ghf_ref, ghb_ref, gr_ref,
                mod_ref, lng_ref, lnb_ref, mlg_ref, glg_ref, wo_ref, wr_ref,
                x1_ref, f_ref, info_ref, cnt_ref, rw_ref):
    ml = _head_rms_padded(mhf_ref[...] + mhb_ref[...], ML_DH) * mlg_ref[...] * _sigmoid(mo_ref[...].astype(F32))
    gl = _head_rms_padded(ghf_ref[...] + ghb_ref[...], GL_DV) * glg_ref[...] * _silu(gr_ref[...].astype(F32))
    y = (_dot(da_ref[...], wo_ref[0:PAD_W, :]) + _dot(ml.astype(BF16), wo_ref[PAD_W:2 * PAD_W, :])
         + _dot(gl.astype(BF16), wo_ref[2 * PAD_W:3 * PAD_W, :]))
    gate1, shift2, scale2 = mod_ref[2:3, :], mod_ref[3:4, :], mod_ref[4:5, :]
    x1 = _ln(dn_alpha * x_ref[...] + gate1 * y) * lng_ref[...] + lnb_ref[...]
    x1_ref[...] = x1
    f = _ln(x1) * (1.0 + scale2) + shift2
    tm = f.shape[0]
    for s in range(SUBLANES):
        f_ref[pl.ds(s, tm, stride=SUBLANES), :] = f[:, s * LANES:(s + 1) * LANES]

    f_top = _bf16_part(f)
    f_hi, f_lo = f_top.astype(BF16), (f - f_top).astype(BF16)
    logits = _dot(f_hi, wr_ref[0]) + (_dot(f_lo, wr_ref[0]) + _dot(f_hi, wr_ref[1]) + _dot(f_lo, wr_ref[1]))
    lane = lax.broadcasted_iota(jnp.int32, logits.shape, 1)
    neg = -jnp.inf
    big = jnp.int32(LANES)

    def first_argmax(vals, vmax):
        return jnp.min(jnp.where(vals == vmax, lane, big), axis=1, keepdims=True)

    gl_ = jnp.where(lane < MOE_GROUPS, logits, neg)
    gmax = jnp.max(gl_, axis=1, keepdims=True)
    g_idx = first_argmax(gl_, gmax)
    g_gate = 1.0 / jnp.sum(jnp.exp(gl_ - gmax), axis=1, keepdims=True)
    in_group = jnp.logical_and(lane >= MOE_GROUPS, (lane - MOE_GROUPS) // MOE_PER_GROUP == g_idx)
    el = jnp.where(in_group, logits, neg)
    emax = jnp.max(el, axis=1, keepdims=True)
    e0 = first_argmax(el, emax)
    el2 = jnp.where(lane == e0, neg, el)
    emax2 = jnp.max(el2, axis=1, keepdims=True)
    e1 = first_argmax(el2, emax2)
    esum = jnp.sum(jnp.exp(el - emax), axis=1, keepdims=True)
    p0 = 1.0 / esum
    p1 = jnp.exp(emax2 - emax) / esum
    w0 = g_gate * p0 / (p0 + p1)
    w1 = g_gate * p1 / (p0 + p1)
    rw_ref[...] = jnp.where(lane == 0, w0, jnp.where(lane == 1, w1, 0.0))

    oh0 = jnp.where(lane == e0, 1.0, 0.0)
    oh1 = jnp.where(lane == e1, 1.0, 0.0)
    both = (oh0 + oh1).astype(BF16)
    r_id = lax.broadcasted_iota(jnp.int32, (tm, tm), 0)
    c_id = lax.broadcasted_iota(jnp.int32, (tm, tm), 1)
    earlier = jnp.where(c_id < r_id, 1.0, 0.0).astype(BF16)
    before = _dot(earlier, both)
    rank0 = jnp.sum(before * oh0, axis=1, keepdims=True)
    rank1 = jnp.sum(before * oh1, axis=1, keepdims=True)
    cnt_ref[...] = _dot(jnp.ones((SUBLANES, tm), BF16), both)
    info = jnp.where(lane == 0, (e0 - MOE_GROUPS).astype(F32),
                     jnp.where(lane == 1, (e1 - MOE_GROUPS).astype(F32),
                               jnp.where(lane == 2, rank0, jnp.where(lane == 3, rank1, 0.0))))
    info_ref[...] = info.T[0:SUBLANES, :]


def _out_proj(xc, da, mhf, mhb, mo, ghf, ghb, gr, modtab, ln_g, ln_b, ml_g, gl_g, wo_p, w_route,
              dn_alpha, n_ctx, tile0):
    b, n, d = xc.shape
    n_ctx_tiles = n_ctx // TM
    n_t = n // TM - tile0
    row = lambda bb, i: (bb, i + tile0, 0)
    orow = lambda bb, i: (bb, i, 0)
    const2 = lambda bb, i: (0, 0)
    wide = pl.BlockSpec((None, TM, PAD_W), row)
    n_out = n_t * TM
    return pl.pallas_call(
        functools.partial(_out_kernel, dn_alpha),
        grid=(b, n_t),
        in_specs=[
            pl.BlockSpec((None, TM, d), row), pl.BlockSpec((None, TM, PAD_W), orow),
            wide, wide, wide, wide, wide, wide,
            pl.BlockSpec((None, None, 6, d), lambda bb, i: (bb, jnp.where(i + tile0 >= n_ctx_tiles, 1, 0), 0, 0)),
            pl.BlockSpec((1, d), const2), pl.BlockSpec((1, d), const2),
            pl.BlockSpec((1, PAD_W), const2), pl.BlockSpec((1, PAD_W), const2),
            pl.BlockSpec((3 * PAD_W, d), const2),
            pl.BlockSpec((2, d, LANES), lambda bb, i: (0, 0, 0)),
        ],
        out_specs=[pl.BlockSpec((None, TM, d), orow),
                   pl.BlockSpec((None, TM * SUBLANES, LANES), orow),
                   pl.BlockSpec((None, SUBLANES, TM), orow),
                   pl.BlockSpec((None, SUBLANES, LANES), orow),
                   pl.BlockSpec((None, TM, LANES), orow)],
        out_shape=[jax.ShapeDtypeStruct((b, n_out, d), F32),
                   jax.ShapeDtypeStruct((b, n_out * SUBLANES, LANES), F32),
                   jax.ShapeDtypeStruct((b, n_t * SUBLANES, TM), F32),
                   jax.ShapeDtypeStruct((b, n_t * SUBLANES, LANES), F32),
                   jax.ShapeDtypeStruct((b, n_out, LANES), F32)],
        compiler_params=_cparams(("parallel", "arbitrary")),
        name="out_proj",
    )(xc, da, mhf, mhb, mo, ghf, ghb, gr, modtab, ln_g, ln_b, ml_g, gl_g, wo_p, w_route)


def _tile_copy(src, src_tile, dst, dst_tile, sem):
    aligned = lambda t: t if isinstance(t, int) else pl.multiple_of(t, SUBLANES)
    return pltpu.make_async_copy(src.at[pl.ds(aligned(src_tile), SUBLANES), :],
                                 dst.at[pl.ds(aligned(dst_tile), SUBLANES), :], sem)


def _rows_from_tiles(ref, n_rows):
    return jnp.concatenate([ref[pl.ds(s, n_rows, stride=SUBLANES), :] for s in range(SUBLANES)], axis=1)


def _rows_to_tiles(ref, val):
    n_rows = val.shape[0]
    for s in range(SUBLANES):
        ref[pl.ds(s, n_rows, stride=SUBLANES), :] = val[:, s * LANES:(s + 1) * LANES]


def _disp_kernel(dest_ref, cnt_ref, start_ref, padded_ref, f_ref, xs_hbm, zero_ref, sem, zsem, bsem):
    bb = pl.program_id(0)
    i = pl.program_id(1)
    tm = f_ref.shape[0] // SUBLANES
    n_tok = pl.num_programs(0) * pl.num_programs(1) * tm
    base = (bb * pl.num_programs(1) + i) * tm

    def scatter(start):
        def body(r, carry):
            for kk in range(2):
                cp = _tile_copy(f_ref, r * SUBLANES, xs_hbm, dest_ref[kk * n_tok + base + r], sem)
                if start:
                    cp.start(priority=kk)
                else:
                    cp.wait()
            return carry
        lax.fori_loop(0, tm, body, 0, unroll=8)

    scatter(True)
    scatter(False)

    @pl.when(jnp.logical_and(bb == pl.num_programs(0) - 1, i == pl.num_programs(1) - 1))
    def _():
        zero_ref[...] = jnp.zeros_like(zero_ref)
        n_slots = xs_hbm.shape[0] // SUBLANES
        last = MOE_EXPERTS - 1

        blk_tiles = zero_ref.shape[0]

        def fill(start):
            def row(r, c2):
                cp = _tile_copy(zero_ref, 0, xs_hbm, r * SUBLANES, zsem)
                if start:
                    cp.start()
                else:
                    cp.wait()
                return c2

            def block(j, c2):
                cp = pltpu.make_async_copy(zero_ref, xs_hbm.at[pl.ds(pl.multiple_of(j * blk_tiles, blk_tiles),
                                                                     blk_tiles), :], bsem)
                if start:
                    cp.start()
                else:
                    cp.wait()
                return c2

            def per_expert(e, carry):
                return lax.fori_loop(start_ref[e] + cnt_ref[e], start_ref[e] + padded_ref[e], row, carry)
            lax.fori_loop(0, MOE_EXPERTS, per_expert, 0)
            used_blocks = (start_ref[last] + padded_ref[last]) // (blk_tiles // SUBLANES)
            lax.fori_loop(used_blocks, n_slots * SUBLANES // blk_tiles, block, 0)

        fill(True)
        fill(False)


def _dispatch(dest8, counts, pad_start, padded, f3, n_slots):
    b, rows, _ = f3.shape
    n_t = rows // (TM * SUBLANES)
    return pl.pallas_call(
        _disp_kernel,
        grid_spec=pltpu.PrefetchScalarGridSpec(
            num_scalar_prefetch=4,
            grid=(b, n_t),
            in_specs=[pl.BlockSpec((TM * SUBLANES, LANES), lambda bb, i, *_: (bb * n_t + i, 0))],
            out_specs=pl.BlockSpec(memory_space=pl.ANY),
            scratch_shapes=[pltpu.VMEM((MOE_BLOCK * SUBLANES, LANES), F32), pltpu.SemaphoreType.DMA(()),
                            pltpu.SemaphoreType.DMA(()), pltpu.SemaphoreType.DMA(())],
        ),
        out_shape=jax.ShapeDtypeStruct((n_slots * SUBLANES, LANES), F32),
        compiler_params=_cparams(("arbitrary", "arbitrary")),
        name="moe_dispatch",
    )(dest8, counts, pad_start, padded, f3.reshape(b * rows, LANES))


def _moe_kernel(be_ref, nu_ref, xs_ref, w1_ref, w3_ref, w2_ref, y_ref, w1b, w3b, w2b):
    j = pl.program_id(0)
    n_used = nu_ref[0]
    blk = xs_ref.shape[0] // SUBLANES

    @pl.when(jnp.logical_and(j < n_used, jnp.logical_or(j == 0, be_ref[j] != be_ref[jnp.maximum(j - 1, 0)])))
    def _():
        w1b[...] = w1_ref[...].astype(BF16)
        w3b[...] = w3_ref[...].astype(BF16)
        w2b[...] = w2_ref[...].astype(BF16)

    @pl.when(j < n_used)
    def _():
        x = _rows_from_tiles(xs_ref, blk).astype(BF16)
        hidden = _silu(_dot(x, w1b[...])) * _dot(x, w3b[...])
        _rows_to_tiles(y_ref, _dot(hidden.astype(BF16), w2b[...]))

    @pl.when(j >= n_used)
    def _():
        y_ref[...] = jnp.zeros_like(y_ref)


def _moe(xs, block_e, n_used, w1, w3, w2, layer):
    n_blocks = block_e.shape[0]
    d, hid = w1.shape[-2:]
    wmap = lambda j, be, nu: (layer, be[j], 0, 0)
    tiles = MOE_BLOCK * SUBLANES
    return pl.pallas_call(
        _moe_kernel,
        grid_spec=pltpu.PrefetchScalarGridSpec(
            num_scalar_prefetch=2,
            grid=(n_blocks,),
            in_specs=[
                pl.BlockSpec((tiles, LANES), lambda j, be, nu: (jnp.minimum(j, jnp.maximum(nu[0] - 1, 0)), 0)),
                pl.BlockSpec((None, None, d, hid), wmap),
                pl.BlockSpec((None, None, d, hid), wmap),
                pl.BlockSpec((None, None, hid, d), wmap),
            ],
            out_specs=pl.BlockSpec((tiles, LANES), lambda j, be, nu: (j, 0)),
            scratch_shapes=[pltpu.VMEM((d, hid), BF16), pltpu.VMEM((d, hid), BF16), pltpu.VMEM((hid, d), BF16)],
        ),
        out_shape=jax.ShapeDtypeStruct((n_blocks * tiles, LANES), F32),
        compiler_params=_cparams(("arbitrary",)),
        name="moe_experts",
    )(block_e, n_used, xs, w1, w3, w2)


def _comb_kernel(dn_alpha, dest_ref, y_hbm, x1_ref, rw_ref, mod_ref, lng_ref, lnb_ref, o_ref, ybuf, sems):
    bb = pl.program_id(0)
    i = pl.program_id(1)
    tm = x1_ref.shape[0]
    n_tok = pl.num_programs(0) * pl.num_programs(1) * tm
    base = (bb * pl.num_programs(1) + i) * tm

    def gather(start):
        def body(r, carry):
            for kk in range(2):
                cp = _tile_copy(y_hbm, dest_ref[kk * n_tok + base + r], ybuf.at[kk], r * SUBLANES, sems.at[kk])
                if start:
                    cp.start(priority=kk)
                else:
                    cp.wait()
            return carry
        lax.fori_loop(0, tm, body, 0, unroll=8)

    gather(True)
    gather(False)
    rw = rw_ref[...]
    y = rw[:, 0:1] * _rows_from_tiles(ybuf.at[0], tm) + rw[:, 1:2] * _rows_from_tiles(ybuf.at[1], tm)
    gate2 = mod_ref[5:6, :]
    o_ref[...] = _ln(dn_alpha * x1_ref[...] + gate2 * y) * lng_ref[...] + lnb_ref[...]


def _combine(dest, y_slots, x1, rw, modtab, ln_g, ln_b, dn_alpha, seg_of_tile):
    b, n, d = x1.shape
    n_t = n // TM
    row = lambda bb, i, dr: (bb, i, 0)
    const2 = lambda bb, i, dr: (0, 0)
    return pl.pallas_call(
        functools.partial(_comb_kernel, dn_alpha),
        grid_spec=pltpu.PrefetchScalarGridSpec(
            num_scalar_prefetch=1,
            grid=(b, n_t),
            in_specs=[
                pl.BlockSpec(memory_space=pl.ANY),
                pl.BlockSpec((None, TM, d), row),
                pl.BlockSpec((None, TM, LANES), row),
                pl.BlockSpec((None, None, 6, d), lambda bb, i, dr: (bb, seg_of_tile(i), 0, 0)),
                pl.BlockSpec((1, d), const2), pl.BlockSpec((1, d), const2),
            ],
            out_specs=pl.BlockSpec((None, TM, d), row),
            scratch_shapes=[pltpu.VMEM((2, TM * SUBLANES, LANES), F32), pltpu.SemaphoreType.DMA((2,))],
        ),
        out_shape=jax.ShapeDtypeStruct((b, n, d), F32),
        compiler_params=_cparams(("arbitrary", "arbitrary")),
        name="moe_combine",
    )(dest, y_slots, x1, rw, modtab, ln_g, ln_b)


def _dispatch_tables(info, tile_counts, n_blocks):
    b, n_t, _, tm = info.shape
    info = info.astype(jnp.int32)
    expert = jnp.moveaxis(info[:, :, 0:2, :], 2, 0).reshape(2, b * n_t, tm)
    rank = jnp.moveaxis(info[:, :, 2:4, :], 2, 0).reshape(2, b * n_t, tm)
    counts = jnp.sum(tile_counts, axis=0)
    padded = (counts + MOE_BLOCK - 1) // MOE_BLOCK * MOE_BLOCK
    pad_end = jnp.cumsum(padded)
    pad_start = pad_end - padded
    tile_base = pad_start[None, :] + jnp.cumsum(tile_counts, axis=0) - tile_counts
    onehot = expert[..., None] == jnp.arange(MOE_EXPERTS, dtype=jnp.int32)
    dest = jnp.sum(jnp.where(onehot, tile_base[None, :, None, :], 0), axis=-1) + rank
    dest8 = (dest * SUBLANES).reshape(-1).astype(jnp.int32)
    block_start = jnp.arange(n_blocks, dtype=jnp.int32) * MOE_BLOCK
    block_e = jnp.minimum(jnp.sum(pad_end[None, :] <= block_start[:, None], axis=1), MOE_EXPERTS - 1).astype(jnp.int32)
    n_used = (pad_end[-1] // MOE_BLOCK).astype(jnp.int32).reshape(1)
    i32 = lambda v: v.astype(jnp.int32)
    return dest8, i32(counts), i32(pad_start), i32(padded), block_e, n_used


def _pad_heads(w, width):
    lead = w.shape[:-1]
    w = w.reshape(*lead, HEADS, width)
    w = jnp.pad(w, [(0, 0)] * len(lead) + [(0, 0), (0, LANES - width)])
    return w.reshape(*lead, PAD_W)


def _expand_matrix(width):
    e = np.zeros((HEADS * width, PAD_W), np.float32)
    for h in range(HEADS):
        e[h * width + np.arange(width), h * LANES + np.arange(width)] = 1.0
    return jnp.asarray(e, BF16)


def _rope_tables(n_ctx, n_lat):
    n_freq = DA_DH // 4
    inv_freq = ROPE_BASE ** (-jnp.arange(n_freq, dtype=F32) / n_freq)
    pos = jnp.arange(n_lat, dtype=jnp.int32)
    ang_r = (pos // GRID_W).astype(F32)[:, None] * inv_freq
    ang_c = (pos % GRID_W).astype(F32)[:, None] * inv_freq
    ang = jnp.concatenate([ang_r, ang_r, ang_c, ang_c] * 2, axis=-1)
    cos, sin = jnp.cos(ang), jnp.sin(ang)
    first = (jnp.arange(LANES) % 32) < 16
    sin_a = jnp.where(first, -sin, 0.0)
    sin_b = jnp.where(first, 0.0, sin)
    ctx_rows = lambda v: jnp.full((n_ctx, LANES), v, F32)
    return jnp.stack([jnp.concatenate([ctx_rows(1.0), cos]), jnp.concatenate([ctx_rows(0.0), sin_a]),
                      jnp.concatenate([ctx_rows(0.0), sin_b])])


def kernel(x, c, ctx, c_ctx, w_ada, b_ada, w_in, da_lambda, da_norm, ml_conv_w, ml_conv_b, ml_ib, ml_fb,
           ml_norm, gl_wa, gl_ba, gl_norm, w_out, ln_mix_g, ln_mix_b, ln_ffn_g, ln_ffn_b,
           moe_wg, moe_we, moe_w1, moe_w3, moe_w2):
    b, s, d = x.shape
    n_ctx = ctx.shape[1]
    n = n_ctx + s
    depth = w_in.shape[0]
    assert n_ctx % TM == 0 and s % TM == 0 and d % LANES == 0
    dn_alpha = (2 * depth) ** 0.25

    rows = -(-(b + 1) // SUBLANES) * SUBLANES
    cc = jnp.concatenate([c, c_ctx[None, :], jnp.zeros((rows - b - 1, d), F32)], axis=0)
    mod = _ada(cc, w_ada, b_ada).reshape(depth, rows, 6, d)

    xc = jnp.concatenate([ctx, x], axis=1)
    rope = _rope_tables(n_ctx, s)
    e64, e32 = _expand_matrix(64), _expand_matrix(32)

    for l in range(depth):
        last = l == depth - 1
        tile0 = n_ctx // TM if last else 0
        lam_init = 0.8 - 0.6 * math.exp(-0.3 * l)
        modtab = jnp.stack([jnp.broadcast_to(mod[l, b], (b, 6, d)), mod[l, :b]], axis=1)

        wi = w_in[l]
        o = np.cumsum([0, 512, 512, 512, 512, 256, 256, 8, 8, 128, 128, 256, 256, 32])
        misc_w = jnp.concatenate([wi[:, o[6]:o[8]], wi[:, o[12]:o[13]],
                                  jnp.zeros((d, LANES - 16 - 2 * GL_RANK), F32)], axis=1)
        w_in_p = jnp.concatenate([wi[:, :o[6]], wi[:, o[8]:o[12]], misc_w], axis=1).astype(BF16)

        (qa, ka, va, mq, mk, mv, mo, gq, gk, gv, gr, misc) = _in_proj(
            xc, modtab, w_in_p, rope, ml_conv_w[l], ml_conv_b[l].reshape(1, -1), e64, e32, n_ctx)

        da = _attention(qa, ka, va, da_lambda[l], da_norm[l].reshape(1, -1), lam_init, n_ctx, tile0)

        gate_bias = jnp.concatenate([ml_ib[l].reshape(-1), ml_fb[l].reshape(-1),
                                     jnp.zeros((LANES - 16,), F32)]).reshape(1, LANES)
        mhf, mhb = _mlstm(mq, mk, mv, misc, gate_bias, n_ctx)

        wa_p = jnp.zeros((2, LANES, PAD_W), F32)
        for dd in range(2):
            wa_p = wa_p.at[dd, M_GA + dd * GL_RANK:M_GA + (dd + 1) * GL_RANK].set(_pad_heads(gl_wa[l, dd], GL_DK))
        ba_p = _pad_heads(gl_ba[l], GL_DK).reshape(2, 1, PAD_W)
        ghf, ghb = _gla(gq, gk, gv, misc, wa_p, ba_p, n_ctx)

        wo = w_out[l]
        wo_p = jnp.concatenate([
            wo[:512],
            jnp.pad(wo[512:768].reshape(HEADS, 64, d), ((0, 0), (0, 64), (0, 0))).reshape(PAD_W, d),
            jnp.pad(wo[768:1024].reshape(HEADS, 64, d), ((0, 0), (0, 64), (0, 0))).reshape(PAD_W, d),
        ], axis=0).astype(BF16)
        w_route = jnp.concatenate([moe_wg[l], moe_we[l],
                                   jnp.zeros((d, LANES - MOE_GROUPS - MOE_EXPERTS), F32)], axis=1)
        w_route_top = _bf16_part(w_route)
        w_route = jnp.stack([w_route_top.astype(BF16), (w_route - w_route_top).astype(BF16)])
        ml_g = _pad_heads(jnp.tile(ml_norm[l], HEADS), ML_DH).reshape(1, PAD_W)
        gl_g = _pad_heads(jnp.tile(gl_norm[l], HEADS), GL_DV).reshape(1, PAD_W)
        x1, f3, info, cnt, rw = _out_proj(xc, da, mhf, mhb, mo, ghf, ghb, gr, modtab,
                                          ln_mix_g[l].reshape(1, d), ln_mix_b[l].reshape(1, d), ml_g, gl_g,
                                          wo_p, w_route, dn_alpha, n_ctx, tile0)

        n_rows = x1.shape[1]
        n_t = n_rows // TM
        a_all = b * n_rows * 2
        n_blocks = -(-(a_all + MOE_EXPERTS * (MOE_BLOCK - 1)) // MOE_BLOCK)
        tile_counts = cnt.reshape(b * n_t, SUBLANES, LANES)[:, 0, MOE_GROUPS:MOE_GROUPS + MOE_EXPERTS]
        dest8, counts, pad_start, padded, block_e, n_used = _dispatch_tables(
            info.reshape(b, n_t, SUBLANES, TM), tile_counts.astype(jnp.int32), n_blocks)
        xs = _dispatch(dest8, counts, pad_start, padded, f3, n_blocks * MOE_BLOCK)
        y_slots = _moe(xs, block_e, n_used, moe_w1, moe_w3, moe_w2, l)
        n_ctx_tiles_here = 0 if last else n_ctx // TM
        seg = lambda i: jnp.where(i >= n_ctx_tiles_here, 1, 0)
        xc = _combine(dest8, y_slots, x1, rw, modtab, ln_ffn_g[l].reshape(1, d), ln_ffn_b[l].reshape(1, d),
                      dn_alpha, seg)
    return xc
```

```python
import functools
import math

import numpy as np
import jax
import jax.numpy as jnp
from jax import lax
from jax.experimental import pallas as pl
from jax.experimental.pallas import tpu as pltpu

F32 = jnp.float32
BF16 = jnp.bfloat16
HI = lax.Precision.HIGHEST

GRID_W = 64
HEADS = 4
DA_DH = 64
ML_DH = 64
GL_DK = 32
GL_DV = 64
GL_RANK = 16
GL_NORMALIZER = 16.0
CHUNK = 64
SUB = 16
ROPE_BASE = 10000.0
MOE_GROUPS = 4
MOE_PER_GROUP = 8
MOE_EXPERTS = 32
MOE_HIDDEN = 512
LN_EPS = 1e-6

LANES = 128
SUBLANES = 8
VMEM_LIMIT = 48 * 1024 * 1024

TM = 256
MOE_BLOCK = 256
SCAN_NB = 2
PAD_W = HEADS * LANES

C_QA, C_KA, C_VA, C_MQK, C_MV, C_MO = 0, 512, 1024, 1536, 2048, 2304
C_GQ, C_GK, C_GV, C_GR, C_MISC, IN_WP = 2560, 2688, 2816, 3072, 3328, 3456
M_IG, M_FG, M_GA = 0, 8, 16


def _cparams(sem):
    return pltpu.CompilerParams(dimension_semantics=sem, vmem_limit_bytes=VMEM_LIMIT)


def _sigmoid(x):
    return 1.0 / (1.0 + jnp.exp(-x))


def _silu(x):
    return x * _sigmoid(x)


def _log_sigmoid(x):
    return jnp.minimum(x, 0.0) - jnp.log(1.0 + jnp.exp(-jnp.abs(x)))


def _ln(x):
    mu = jnp.mean(x, axis=-1, keepdims=True)
    xc = x - mu
    var = jnp.mean(xc * xc, axis=-1, keepdims=True)
    return xc * lax.rsqrt(var + LN_EPS)


def _bf16_part(x):
    bits = lax.bitcast_convert_type(x, jnp.int32) & jnp.int32(-65536)
    return lax.bitcast_convert_type(bits, F32)


def _dot(a, b):
    return jnp.dot(a, b, preferred_element_type=F32)


def _dot_hi(a, b):
    return jnp.dot(a, b, precision=HI, preferred_element_type=F32)


def _cumsum_dot(a, b, split_lhs=False):
    x = a if split_lhs else b
    x0 = _bf16_part(x)
    r1 = x - x0
    x1 = _bf16_part(r1)
    x2 = r1 - x1
    parts = [p.astype(BF16) for p in (x0, x1, x2)]
    if split_lhs:
        return _dot(parts[0], b) + (_dot(parts[1], b) + _dot(parts[2], b))
    return _dot(a, parts[0]) + (_dot(a, parts[1]) + _dot(a, parts[2]))


def _dot_nt(a, b):
    return lax.dot_general(a, b, (((1,), (1,)), ((), ())), preferred_element_type=F32)


def _dot_tn(a, b):
    return lax.dot_general(a, b, (((0,), (0,)), ((), ())), preferred_element_type=F32)


def _ada_kernel(c_ref, w_ref, b_ref, o_ref):
    o_ref[...] = _dot_hi(_silu(c_ref[...]), w_ref[...]) + b_ref[...]


def _ada(cc, w_ada, b_ada):
    depth, d, n6 = w_ada.shape
    tn = 1536
    rows = cc.shape[0]
    return pl.pallas_call(
        _ada_kernel,
        grid=(depth, n6 // tn),
        in_specs=[
            pl.BlockSpec((rows, d), lambda l, j: (0, 0)),
            pl.BlockSpec((None, d, tn), lambda l, j: (l, 0, j)),
            pl.BlockSpec((None, 1, tn), lambda l, j: (l, 0, j)),
        ],
        out_specs=pl.BlockSpec((None, rows, tn), lambda l, j: (l, 0, j)),
        out_shape=jax.ShapeDtypeStruct((depth, rows, n6), F32),
        compiler_params=_cparams(("arbitrary", "arbitrary")),
        name="ada_mod",
    )(cc, w_ada, b_ada.reshape(depth, 1, n6))


def _in_kernel(n_ctx_tiles, x_ref, xp_ref, xn_ref, mod_ref, w_ref, rope_ref, cw_ref, cb_ref,
               e64_ref, e32_ref,
               qa_ref, ka_ref, va_ref, mq_ref, mk_ref, mv_ref, mo_ref,
               gq_ref, gk_ref, gv_ref, gr_ref, misc_ref, pext_ref):
    i = pl.program_id(1)
    n_tiles = pl.num_programs(1)
    tm = x_ref.shape[0]
    shift, scale = mod_ref[0:1, :], mod_ref[1:2, :]

    def prenorm(x):
        return (_ln(x) * (1.0 + scale) + shift).astype(BF16)

    h = prenorm(x_ref[...])

    def proj(a, lo, hi):
        return _dot(a, w_ref[:, lo:hi])

    cos, sin_a, sin_b = rope_ref[0], rope_ref[1], rope_ref[2]
    for lo, out_ref, mult in ((C_QA, qa_ref, DA_DH ** -0.5), (C_KA, ka_ref, 1.0)):
        acc = proj(h, lo, lo + PAD_W)
        for hh in range(HEADS):
            xh = acc[:, hh * LANES:(hh + 1) * LANES]
            r = xh * cos + pltpu.roll(xh, LANES - 16, 1) * sin_a + pltpu.roll(xh, 16, 1) * sin_b
            out_ref[:, hh * LANES:(hh + 1) * LANES] = (r * mult).astype(BF16)
    va_ref[...] = proj(h, C_VA, C_VA + PAD_W).astype(BF16)

    seg_first = jnp.logical_or(i == 0, i == n_ctx_tiles)
    seg_last = jnp.logical_or(i == n_ctx_tiles - 1, i == n_tiles - 1)
    keep_prev = jnp.where(seg_first, 0.0, 1.0)
    keep_next = jnp.where(seg_last, 0.0, 1.0)
    p = proj(h, C_MQK, C_MQK + 512)
    pext_ref[0:SUBLANES, :] = proj(prenorm(xp_ref[...]), C_MQK, C_MQK + 512) * keep_prev
    pext_ref[SUBLANES:SUBLANES + tm, :] = p
    pext_ref[SUBLANES + tm:2 * SUBLANES + tm, :] = proj(prenorm(xn_ref[...]), C_MQK, C_MQK + 512) * keep_next
    y = (cw_ref[0:1, :] * pext_ref[SUBLANES - 1:SUBLANES - 1 + tm, :] + cw_ref[1:2, :] * p
         + cw_ref[2:3, :] * pext_ref[SUBLANES + 1:SUBLANES + 1 + tm, :] + cb_ref[...])
    y = _silu(y)
    e64 = e64_ref[...]
    e32 = e32_ref[...]
    mq_ref[...] = _dot((y[:, :256] * ML_DH ** -0.5).astype(BF16), e64).astype(BF16)
    mk_ref[...] = _dot(y[:, 256:].astype(BF16), e64).astype(BF16)
    lane = lax.broadcasted_iota(jnp.int32, (tm, PAD_W), 1)
    ones_col = jnp.where(lane % LANES >= ML_DH, 1.0, 0.0)
    mv_ref[...] = (_dot(proj(h, C_MV, C_MV + 256).astype(BF16), e64) + ones_col).astype(BF16)
    mo_ref[...] = _dot(proj(h, C_MO, C_MO + 256).astype(BF16), e64).astype(BF16)

    gq_ref[...] = _dot((proj(h, C_GQ, C_GQ + 128) * GL_DK ** -0.5).astype(BF16), e32).astype(BF16)
    gk_ref[...] = _dot(proj(h, C_GK, C_GK + 128).astype(BF16), e32).astype(BF16)
    gv_ref[...] = _dot(proj(h, C_GV, C_GV + 256).astype(BF16), e64).astype(BF16)
    gr_ref[...] = _dot(proj(h, C_GR, C_GR + 256).astype(BF16), e64).astype(BF16)
    misc_ref[...] = proj(h, C_MISC, C_MISC + LANES)


def _in_proj(xc, modtab, w_in_p, rope, conv_w, conv_b, e64, e32, n_ctx):
    b, n, d = xc.shape
    n_tiles = n // TM
    n_ctx_tiles = n_ctx // TM
    hb = TM // SUBLANES
    n_hb = n // SUBLANES
    row = lambda bb, i: (bb, i, 0)
    const2 = lambda bb, i: (0, 0)
    outs = [jax.ShapeDtypeStruct((b, n, PAD_W), BF16)] * 11 + [jax.ShapeDtypeStruct((b, n, LANES), F32)]
    out_specs = [pl.BlockSpec((None, TM, PAD_W), row)] * 11 + [pl.BlockSpec((None, TM, LANES), row)]
    return pl.pallas_call(
        functools.partial(_in_kernel, n_ctx_tiles),
        grid=(b, n_tiles),
        in_specs=[
            pl.BlockSpec((None, TM, d), row),
            pl.BlockSpec((None, SUBLANES, d), lambda bb, i: (bb, jnp.maximum(i * hb - 1, 0), 0)),
            pl.BlockSpec((None, SUBLANES, d), lambda bb, i: (bb, jnp.minimum((i + 1) * hb, n_hb - 1), 0)),
            pl.BlockSpec((None, None, 6, d), lambda bb, i: (bb, jnp.where(i >= n_ctx_tiles, 1, 0), 0, 0)),
            pl.BlockSpec((d, IN_WP), const2),
            pl.BlockSpec((3, TM, LANES), lambda bb, i: (0, i, 0)),
            pl.BlockSpec((3, 512), const2),
            pl.BlockSpec((1, 512), const2),
            pl.BlockSpec((256, PAD_W), const2),
            pl.BlockSpec((128, PAD_W), const2),
        ],
        out_specs=out_specs,
        out_shape=outs,
        scratch_shapes=[pltpu.VMEM((TM + 2 * SUBLANES, 512), F32)],
        compiler_params=_cparams(("parallel", "arbitrary")),
        name="in_proj",
    )(xc, xc, xc, modtab, w_in_p, rope, conv_w, conv_b, e64, e32)


def _attn_kernel(lam_init, n_ctx, tile0, q_ref, k_ref, v_ref, lamp_ref, g_ref, o_ref):
    i = pl.program_id(2) + tile0
    lp = lamp_ref[...]
    lam = (jnp.exp(jnp.sum(lp[0:1] * lp[1:2], axis=1, keepdims=True))
           - jnp.exp(jnp.sum(lp[2:3] * lp[3:4], axis=1, keepdims=True)) + lam_init)
    q = q_ref[...]
    lane = lax.broadcasted_iota(jnp.int32, q.shape, 1)
    zero = jnp.zeros_like(q)
    q1 = jnp.where(lane < DA_DH, q, zero)
    q2 = jnp.where(lane >= DA_DH, q, zero)

    def attend(nk):
        k = k_ref[0:nk, :]
        v = v_ref[0:nk, :]

        def softmax_parts(qm):
            s = _dot_nt(qm, k)
            e = jnp.exp(s - jnp.max(s, axis=1, keepdims=True))
            return e, 1.0 / jnp.sum(e, axis=1, keepdims=True)

        e1, r1 = softmax_parts(q1)
        e2, r2 = softmax_parts(q2)
        a = e1 * r1 - e2 * (r2 * lam)
        o = _dot(a.astype(BF16), v)
        y = o * lax.rsqrt(jnp.mean(o * o, axis=1, keepdims=True) + LN_EPS) * g_ref[...]
        o_ref[...] = (y * (1.0 - lam_init)).astype(BF16)

    n_ctx_tiles = n_ctx // q_ref.shape[0]

    @pl.when(i < n_ctx_tiles)
    def _():
        attend(n_ctx)

    @pl.when(i >= n_ctx_tiles)
    def _():
        attend(k_ref.shape[0])


def _attention(qa, ka, va, lam_p, norm_g, lam_init, n_ctx, tile0):
    b, n, _ = qa.shape
    tq = TM
    n_q = n // tq - tile0
    return pl.pallas_call(
        functools.partial(_attn_kernel, lam_init, n_ctx, tile0),
        grid=(b, HEADS, n_q),
        in_specs=[
            pl.BlockSpec((None, tq, LANES), lambda bb, h, i: (bb, i + tile0, h)),
            pl.BlockSpec((None, n, LANES), lambda bb, h, i: (bb, 0, h)),
            pl.BlockSpec((None, n, LANES), lambda bb, h, i: (bb, 0, h)),
            pl.BlockSpec((4, DA_DH), lambda bb, h, i: (0, 0)),
            pl.BlockSpec((1, LANES), lambda bb, h, i: (0, 0)),
        ],
        out_specs=pl.BlockSpec((None, tq, LANES), lambda bb, h, i: (bb, i, h)),
        out_shape=jax.ShapeDtypeStruct((b, n_q * tq, PAD_W), BF16),
        compiler_params=_cparams(("parallel", "parallel", "arbitrary")),
        name="diff_attn",
    )(qa, ka, va, lam_p, norm_g)


def _chunk_maps(n_ctx_chunks, n_chunks):
    def fwd(bb, t):
        return (bb, t, 0)

    def bwd(bb, t):
        return (bb, jnp.where(t < n_ctx_chunks, n_ctx_chunks - 1 - t, n_chunks - 1 + n_ctx_chunks - t), 0)

    return fwd, bwd


def _tri(direction):
    l = lax.broadcasted_iota(jnp.int32, (CHUNK, CHUNK), 0)
    s = lax.broadcasted_iota(jnp.int32, (CHUNK, CHUNK), 1)
    return (s <= l) if direction == 0 else (s >= l)


def _mlstm_kernel(qf_ref, kf_ref, vf_ref, gf_ref, qb_ref, kb_ref, vb_ref, gb_ref, bias_ref, sel_ref,
                  hf_ref, hb_ref, c_ref, m_ref):
    t = pl.program_id(1)

    @pl.when(t == 0)
    def _():
        c_ref[...] = jnp.zeros_like(c_ref)
        m_ref[...] = jnp.zeros_like(m_ref)

    lane = lax.broadcasted_iota(jnp.int32, (CHUNK, LANES), 1)
    is_fg = jnp.logical_and(lane >= M_FG, lane < M_GA)
    dirs = ((qf_ref, kf_ref, vf_ref, gf_ref, hf_ref), (qb_ref, kb_ref, vb_ref, gb_ref, hb_ref))
    n_b = qf_ref.shape[0]
    masks = [_tri(d) for d in range(2)]
    units = [(d, nb, h) for d in range(2) for nb in range(n_b) for h in range(HEADS)]

    pre = {}
    for d in range(2):
        tri = jnp.where(masks[d], 1.0, 0.0).astype(BF16)
        for nb in range(n_b):
            g = dirs[d][3][nb] + bias_ref[...]
            g = jnp.where(is_fg, _log_sigmoid(g), g)
            cum = _cumsum_dot(tri, g)
            pre[d, nb] = (_cumsum_dot(cum, sel_ref[d, 0], split_lhs=True),
                          _cumsum_dot(g, sel_ref[d, 1], split_lhs=True), g.T, cum.T)

    qk, qc = {}, {}
    for u in units:
        d, nb, h = u
        hs = slice(h * LANES, (h + 1) * LANES)
        q = dirs[d][0][nb, :, hs]
        qk[u] = _dot_nt(q, dirs[d][1][nb, :, hs])
        qc[u] = _dot(q, c_ref[nb, d, h].astype(BF16))

    gate = {}
    for u in units:
        d, nb, h = u
        hs = slice(h * LANES, (h + 1) * LANES)
        b_all, i_all, g_t, cum_t = pre[d, nb]
        ci, cf = M_IG + HEADS * d + h, M_FG + HEADS * d + h
        last = CHUNK - 1 if d == 0 else 0
        b_rep = b_all[:, hs]
        i_rep = i_all[:, hs]
        row = g_t[ci:ci + 1, :] - cum_t[cf:cf + 1, :]
        m_prev = m_ref[nb, d, h][0:1, :]
        log_d = jnp.where(masks[d], b_rep[:, 0:CHUNK] + row, -jnp.inf)
        log_a = b_rep + m_prev
        m_t = jnp.maximum(log_a, jnp.max(log_d, axis=1, keepdims=True))
        m_new = m_t[last:last + 1, :]
        b_last = b_rep[last:last + 1, :]
        gate[u] = (jnp.exp(log_d - m_t[:, 0:CHUNK]), jnp.exp(log_a - m_t), jnp.exp(-m_t),
                   jnp.exp(b_last - b_rep + i_rep - m_new), jnp.exp(b_last + m_prev - m_new), m_new)

    sv, kv = {}, {}
    for u in units:
        d, nb, h = u
        hs = slice(h * LANES, (h + 1) * LANES)
        v = dirs[d][2][nb, :, hs]
        dmat, _, _, w, _, _ = gate[u]
        sv[u] = _dot((qk[u] * dmat).astype(BF16), v)
        kv[u] = _dot_tn((dirs[d][1][nb, :, hs].astype(F32) * w).astype(BF16), v)

    for u in units:
        d, nb, h = u
        hs = slice(h * LANES, (h + 1) * LANES)
        _, a, floor, _, decay, m_new = gate[u]
        num = a * qc[u] + sv[u]
        qn = pltpu.roll(num, ML_DH, 1)
        den = jnp.maximum(jnp.abs(qn), floor)
        dirs[d][4][nb, :, hs] = jnp.where(lane < ML_DH, num / den, 0.0)
        c_ref[nb, d, h] = decay * c_ref[nb, d, h] + kv[u]
        m_ref[nb, d, h] = jnp.broadcast_to(m_new, (SUBLANES, LANES))


def _gate_select_matrices():
    sel = np.zeros((2, 2, LANES, PAD_W), np.float32)
    for d in range(2):
        for h in range(HEADS):
            sel[d, 0, M_FG + HEADS * d + h, h * LANES:(h + 1) * LANES] = 1.0
            sel[d, 1, M_IG + HEADS * d + h, h * LANES:(h + 1) * LANES] = 1.0
    return jnp.asarray(sel, BF16)


def _mlstm(mq, mk, mv, misc, gate_bias, n_ctx):
    b, n, _ = mq.shape
    n_chunks = n // CHUNK
    fwd, bwd = _chunk_maps(n_ctx // CHUNK, n_chunks)
    wide = lambda m: pl.BlockSpec((SCAN_NB, CHUNK, PAD_W), m)
    thin = lambda m: pl.BlockSpec((SCAN_NB, CHUNK, LANES), m)
    return pl.pallas_call(
        _mlstm_kernel,
        grid=(b // SCAN_NB, n_chunks),
        in_specs=[wide(fwd), wide(fwd), wide(fwd), thin(fwd), wide(bwd), wide(bwd), wide(bwd), thin(bwd),
                  pl.BlockSpec((1, LANES), lambda bb, t: (0, 0)),
                  pl.BlockSpec((2, 2, LANES, PAD_W), lambda bb, t: (0, 0, 0, 0))],
        out_specs=[wide(fwd), wide(bwd)],
        out_shape=[jax.ShapeDtypeStruct((b, n, PAD_W), F32)] * 2,
        scratch_shapes=[pltpu.VMEM((SCAN_NB, 2, HEADS, LANES, LANES), F32),
                        pltpu.VMEM((SCAN_NB, 2, HEADS, SUBLANES, LANES), F32)],
        compiler_params=_cparams(("parallel", "arbitrary")),
        name="mlstm_scan",
    )(mq, mk, mv, misc, mq, mk, mv, misc, gate_bias, _gate_select_matrices())


def _gla_kernel(qf_ref, kf_ref, vf_ref, gf_ref, qb_ref, kb_ref, vb_ref, gb_ref, wa_ref, ba_ref,
                hf_ref, hb_ref, st_ref):
    t = pl.program_id(1)

    @pl.when(t == 0)
    def _():
        st_ref[...] = jnp.zeros_like(st_ref)

    n_sub = CHUNK // SUB
    row_id = lax.broadcasted_iota(jnp.int32, (CHUNK, LANES), 0)
    row_blk = row_id // SUB
    col_id = lax.broadcasted_iota(jnp.int32, (CHUNK, LANES), 1)
    rows_w = lax.broadcasted_iota(jnp.int32, (CHUNK, PAD_W), 0)
    zeros = jnp.zeros((CHUNK, LANES), BF16)
    dirs = ((qf_ref, kf_ref, vf_ref, gf_ref, hf_ref), (qb_ref, kb_ref, vb_ref, gb_ref, hb_ref))
    n_b = qf_ref.shape[0]
    pairs = [(d, nb) for d in range(2) for nb in range(n_b)]
    units = [(d, nb, h) for d, nb in pairs for h in range(HEADS)]
    tris = [jnp.where(_tri(d), 1.0, 0.0).astype(BF16) for d in range(2)]
    causal = [col_id <= row_id, jnp.logical_and(col_id >= row_id, col_id < CHUNK)]

    las = {p: _log_sigmoid(_dot(dirs[p[0]][3][p[1]].astype(BF16), wa_ref[p[0]]) + ba_ref[p[0]])
           * (1.0 / GL_NORMALIZER) for p in pairs}
    cums = {p: _cumsum_dot(tris[p[0]], las[p]) for p in pairs}

    pre = {}
    for p in pairs:
        d, nb = p
        cum = cums[p]
        excl = cum - las[p]
        last = CHUNK - 1 if d == 0 else 0
        betas = [excl[r:r + 1, :] for r in (range(0, CHUNK, SUB) if d == 0 else range(SUB - 1, CHUNK, SUB))]
        beta_rows = jnp.concatenate([jnp.broadcast_to(bt, (SUB, PAD_W)) for bt in betas], axis=0)
        q = dirs[d][0][nb].astype(F32)
        k = dirs[d][1][nb].astype(F32)
        q_loc = (q * jnp.exp(cum - beta_rows)).astype(BF16)
        q_abs = (q * jnp.exp(cum)).astype(BF16)
        b_last = cum[last:last + 1, :]
        k_end = (k * jnp.exp(b_last - cum)).astype(BF16)
        k_subs = []
        for I in range(n_sub):
            seen = (rows_w < (I + 1) * SUB) if d == 0 else (rows_w >= I * SUB)
            k_subs.append((k * jnp.exp(jnp.where(seen, betas[I] - cum, -jnp.inf))).astype(BF16))
        pre[p] = (q_loc, q_abs, b_last, k_end, k_subs)

    scores, inter, kv = {}, {}, {}
    for u in units:
        d, nb, h = u
        hs = slice(h * LANES, (h + 1) * LANES)
        q_loc, q_abs, _, k_end, k_subs = pre[d, nb]
        kcat = jnp.concatenate([x for I in range(n_sub) for x in (k_subs[I][:, hs], zeros)], axis=0)
        scores[u] = _dot_nt(q_loc[:, hs], kcat)
        inter[u] = _dot_nt(q_abs[:, hs], st_ref[nb, d, h].astype(BF16))
        kv[u] = _dot_tn(dirs[d][2][nb, :, hs], k_end[:, hs])

    att = {}
    for u in units:
        r = scores[u]
        a = r[:, 0:LANES]
        for I in range(1, n_sub):
            a = jnp.where(row_blk == I, r[:, I * LANES:(I + 1) * LANES], a)
        att[u] = jnp.where(causal[u[0]], a, 0.0)[:, 0:CHUNK].astype(BF16)

    av = {u: _dot(att[u], dirs[u[0]][2][u[1], :, u[2] * LANES:(u[2] + 1) * LANES]) for u in units}
    for u in units:
        d, nb, h = u
        hs = slice(h * LANES, (h + 1) * LANES)
        dirs[d][4][nb, :, hs] = inter[u] + av[u]
        st_ref[nb, d, h] = st_ref[nb, d, h] * jnp.exp(pre[d, nb][2][:, hs]) + kv[u]


def _gla(gq, gk, gv, misc, wa_p, ba_p, n_ctx):
    b, n, _ = gq.shape
    n_chunks = n // CHUNK
    fwd, bwd = _chunk_maps(n_ctx // CHUNK, n_chunks)
    wide = lambda m: pl.BlockSpec((SCAN_NB, CHUNK, PAD_W), m)
    thin = lambda m: pl.BlockSpec((SCAN_NB, CHUNK, LANES), m)
    return pl.pallas_call(
        _gla_kernel,
        grid=(b // SCAN_NB, n_chunks),
        in_specs=[wide(fwd), wide(fwd), wide(fwd), thin(fwd), wide(bwd), wide(bwd), wide(bwd), thin(bwd),
                  pl.BlockSpec((2, LANES, PAD_W), lambda bb, t: (0, 0, 0)),
                  pl.BlockSpec((2, 1, PAD_W), lambda bb, t: (0, 0, 0))],
        out_specs=[wide(fwd), wide(bwd)],
        out_shape=[jax.ShapeDtypeStruct((b, n, PAD_W), F32)] * 2,
        scratch_shapes=[pltpu.VMEM((SCAN_NB, 2, HEADS, LANES, LANES), F32)],
        compiler_params=_cparams(("parallel", "arbitrary")),
        name="gla_scan",
    )(gq, gk, gv, misc, gq, gk, gv, misc, wa_p, ba_p)


def _head_rms_padded(x, width):
    parts = []
    for h in range(HEADS):
        xh = x[:, h * LANES:(h + 1) * LANES]
        ms = jnp.sum(xh * xh, axis=1, keepdims=True) * (1.0 / width)
        parts.append(xh * lax.rsqrt(ms + LN_EPS))
    return jnp.concatenate(parts, axis=1)


def _out_kernel(dn_alpha, x_ref, da_ref, mhf_ref, mhb_ref, mo_ref, ghf_ref, ghb_ref, gr_ref,
                mod_ref, lng_ref, lnb_ref, mlg_ref, glg_ref, wo_ref, wr_ref,
                x1_ref, f_ref, info_ref, cnt_ref, rw_ref):
    ml = _head_rms_padded(mhf_ref[...] + mhb_ref[...], ML_DH) * mlg_ref[...] * _sigmoid(mo_ref[...].astype(F32))
    gl = _head_rms_padded(ghf_ref[...] + ghb_ref[...], GL_DV) * glg_ref[...] * _silu(gr_ref[...].astype(F32))
    y = (_dot(da_ref[...], wo_ref[0:PAD_W, :]) + _dot(ml.astype(BF16), wo_ref[PAD_W:2 * PAD_W, :])
         + _dot(gl.astype(BF16), wo_ref[2 * PAD_W:3 * PAD_W, :]))
    gate1, shift2, scale2 = mod_ref[2:3, :], mod_ref[3:4, :], mod_ref[4:5, :]
    x1 = _ln(dn_alpha * x_ref[...] + gate1 * y) * lng_ref[...] + lnb_ref[...]
    x1_ref[...] = x1
    f = _ln(x1) * (1.0 + scale2) + shift2
    tm = f.shape[0]
    for s in range(SUBLANES):
        f_ref[pl.ds(s, tm, stride=SUBLANES), :] = f[:, s * LANES:(s + 1) * LANES]

    f_top = _bf16_part(f)
    f_hi, f_lo = f_top.astype(BF16), (f - f_top).astype(BF16)
    logits = _dot(f_hi, wr_ref[0]) + (_dot(f_lo, wr_ref[0]) + _dot(f_hi, wr_ref[1]) + _dot(f_lo, wr_ref[1]))
    lane = lax.broadcasted_iota(jnp.int32, logits.shape, 1)
    neg = -jnp.inf
    big = jnp.int32(LANES)

    def first_argmax(vals, vmax):
        return jnp.min(jnp.where(vals == vmax, lane, big), axis=1, keepdims=True)

    gl_ = jnp.where(lane < MOE_GROUPS, logits, neg)
    gmax = jnp.max(gl_, axis=1, keepdims=True)
    g_idx = first_argmax(gl_, gmax)
    g_gate = 1.0 / jnp.sum(jnp.exp(gl_ - gmax), axis=1, keepdims=True)
    in_group = jnp.logical_and(lane >= MOE_GROUPS, (lane - MOE_GROUPS) // MOE_PER_GROUP == g_idx)
    el = jnp.where(in_group, logits, neg)
    emax = jnp.max(el, axis=1, keepdims=True)
    e0 = first_argmax(el, emax)
    el2 = jnp.where(lane == e0, neg, el)
    emax2 = jnp.max(el2, axis=1, keepdims=True)
    e1 = first_argmax(el2, emax2)
    esum = jnp.sum(jnp.exp(el - emax), axis=1, keepdims=True)
    p0 = 1.0 / esum
    p1 = jnp.exp(emax2 - emax) / esum
    w0 = g_gate * p0 / (p0 + p1)
    w1 = g_gate * p1 / (p0 + p1)
    rw_ref[...] = jnp.where(lane == 0, w0, jnp.where(lane == 1, w1, 0.0))

    oh0 = jnp.where(lane == e0, 1.0, 0.0)
    oh1 = jnp.where(lane == e1, 1.0, 0.0)
    both = (oh0 + oh1).astype(BF16)
    r_id = lax.broadcasted_iota(jnp.int32, (tm, tm), 0)
    c_id = lax.broadcasted_iota(jnp.int32, (tm, tm), 1)
    earlier = jnp.where(c_id < r_id, 1.0, 0.0).astype(BF16)
    before = _dot(earlier, both)
    rank0 = jnp.sum(before * oh0, axis=1, keepdims=True)
    rank1 = jnp.sum(before * oh1, axis=1, keepdims=True)
    cnt_ref[...] = _dot(jnp.ones((SUBLANES, tm), BF16), both)
    info = jnp.where(lane == 0, (e0 - MOE_GROUPS).astype(F32),
                     jnp.where(lane == 1, (e1 - MOE_GROUPS).astype(F32),
                               jnp.where(lane == 2, rank0, jnp.where(lane == 3, rank1, 0.0))))
    info_ref[...] = info.T[0:SUBLANES, :]


def _out_proj(xc, da, mhf, mhb, mo, ghf, ghb, gr, modtab, ln_g, ln_b, ml_g, gl_g, wo_p, w_route,
              dn_alpha, n_ctx, tile0):
    b, n, d = xc.shape
    n_ctx_tiles = n_ctx // TM
    n_t = n // TM - tile0
    row = lambda bb, i: (bb, i + tile0, 0)
    orow = lambda bb, i: (bb, i, 0)
    const2 = lambda bb, i: (0, 0)
    wide = pl.BlockSpec((None, TM, PAD_W), row)
    n_out = n_t * TM
    return pl.pallas_call(
        functools.partial(_out_kernel, dn_alpha),
        grid=(b, n_t),
        in_specs=[
            pl.BlockSpec((None, TM, d), row), pl.BlockSpec((None, TM, PAD_W), orow),
            wide, wide, wide, wide, wide, wide,
            pl.BlockSpec((None, None, 6, d), lambda bb, i: (bb, jnp.where(i + tile0 >= n_ctx_tiles, 1, 0), 0, 0)),
            pl.BlockSpec((1, d), const2), pl.BlockSpec((1, d), const2),
            pl.BlockSpec((1, PAD_W), const2), pl.BlockSpec((1, PAD_W), const2),
            pl.BlockSpec((3 * PAD_W, d), const2),
            pl.BlockSpec((2, d, LANES), lambda bb, i: (0, 0, 0)),
        ],
        out_specs=[pl.BlockSpec((None, TM, d), orow),
                   pl.BlockSpec((None, TM * SUBLANES, LANES), orow),
                   pl.BlockSpec((None, SUBLANES, TM), orow),
                   pl.BlockSpec((None, SUBLANES, LANES), orow),
                   pl.BlockSpec((None, TM, LANES), orow)],
        out_shape=[jax.ShapeDtypeStruct((b, n_out, d), F32),
                   jax.ShapeDtypeStruct((b, n_out * SUBLANES, LANES), F32),
                   jax.ShapeDtypeStruct((b, n_t * SUBLANES, TM), F32),
                   jax.ShapeDtypeStruct((b, n_t * SUBLANES, LANES), F32),
                   jax.ShapeDtypeStruct((b, n_out, LANES), F32)],
        compiler_params=_cparams(("parallel", "arbitrary")),
        name="out_proj",
    )(xc, da, mhf, mhb, mo, ghf, ghb, gr, modtab, ln_g, ln_b, ml_g, gl_g, wo_p, w_route)


def _tile_copy(src, src_tile, dst, dst_tile, sem):
    aligned = lambda t: t if isinstance(t, int) else pl.multiple_of(t, SUBLANES)
    return pltpu.make_async_copy(src.at[pl.ds(aligned(src_tile), SUBLANES), :],
                                 dst.at[pl.ds(aligned(dst_tile), SUBLANES), :], sem)


def _rows_from_tiles(ref, n_rows):
    return jnp.concatenate([ref[pl.ds(s, n_rows, stride=SUBLANES), :] for s in range(SUBLANES)], axis=1)


def _rows_to_tiles(ref, val):
    n_rows = val.shape[0]
    for s in range(SUBLANES):
        ref[pl.ds(s, n_rows, stride=SUBLANES), :] = val[:, s * LANES:(s + 1) * LANES]


def _disp_kernel(dest_ref, cnt_ref, start_ref, padded_ref, f_ref, xs_hbm, zero_ref, sem, zsem, bsem):
    bb = pl.program_id(0)
    i = pl.program_id(1)
    tm = f_ref.shape[0] // SUBLANES
    n_tok = pl.num_programs(0) * pl.num_programs(1) * tm
    base = (bb * pl.num_programs(1) + i) * tm

    def scatter(start):
        def body(r, carry):
            for kk in range(2):
                cp = _tile_copy(f_ref, r * SUBLANES, xs_hbm, dest_ref[kk * n_tok + base + r], sem)
                if start:
                    cp.start(priority=kk)
                else:
                    cp.wait()
            return carry
        lax.fori_loop(0, tm, body, 0, unroll=8)

    scatter(True)
    scatter(False)

    @pl.when(jnp.logical_and(bb == pl.num_programs(0) - 1, i == pl.num_programs(1) - 1))
    def _():
        zero_ref[...] = jnp.zeros_like(zero_ref)
        n_slots = xs_hbm.shape[0] // SUBLANES
        last = MOE_EXPERTS - 1

        blk_tiles = zero_ref.shape[0]

        def fill(start):
            def row(r, c2):
                cp = _tile_copy(zero_ref, 0, xs_hbm, r * SUBLANES, zsem)
                if start:
                    cp.start()
                else:
                    cp.wait()
                return c2

            def block(j, c2):
                cp = pltpu.make_async_copy(zero_ref, xs_hbm.at[pl.ds(pl.multiple_of(j * blk_tiles, blk_tiles),
                                                                     blk_tiles), :], bsem)
                if start:
                    cp.start()
                else:
                    cp.wait()
                return c2

            def per_expert(e, carry):
                return lax.fori_loop(start_ref[e] + cnt_ref[e], start_ref[e] + padded_ref[e], row, carry)
            lax.fori_loop(0, MOE_EXPERTS, per_expert, 0)
            used_blocks = (start_ref[last] + padded_ref[last]) // (blk_tiles // SUBLANES)
            lax.fori_loop(used_blocks, n_slots * SUBLANES // blk_tiles, block, 0)

        fill(True)
        fill(False)


def _dispatch(dest8, counts, pad_start, padded, f3, n_slots):
    b, rows, _ = f3.shape
    n_t = rows // (TM * SUBLANES)
    return pl.pallas_call(
        _disp_kernel,
        grid_spec=pltpu.PrefetchScalarGridSpec(
            num_scalar_prefetch=4,
            grid=(b, n_t),
            in_specs=[pl.BlockSpec((TM * SUBLANES, LANES), lambda bb, i, *_: (bb * n_t + i, 0))],
            out_specs=pl.BlockSpec(memory_space=pl.ANY),
            scratch_shapes=[pltpu.VMEM((MOE_BLOCK * SUBLANES, LANES), F32), pltpu.SemaphoreType.DMA(()),
                            pltpu.SemaphoreType.DMA(()), pltpu.SemaphoreType.DMA(())],
        ),
        out_shape=jax.ShapeDtypeStruct((n_slots * SUBLANES, LANES), F32),
        compiler_params=_cparams(("arbitrary", "arbitrary")),
        name="moe_dispatch",
    )(dest8, counts, pad_start, padded, f3.reshape(b * rows, LANES))


def _moe_kernel(be_ref, nu_ref, xs_ref, w1_ref, w3_ref, w2_ref, y_ref, w1b, w3b, w2b):
    j = pl.program_id(0)
    n_used = nu_ref[0]
    blk = xs_ref.shape[0] // SUBLANES

    @pl.when(jnp.logical_and(j < n_used, jnp.logical_or(j == 0, be_ref[j] != be_ref[jnp.maximum(j - 1, 0)])))
    def _():
        w1b[...] = w1_ref[...].astype(BF16)
        w3b[...] = w3_ref[...].astype(BF16)
        w2b[...] = w2_ref[...].astype(BF16)

    @pl.when(j < n_used)
    def _():
        x = _rows_from_tiles(xs_ref, blk).astype(BF16)
        hidden = _silu(_dot(x, w1b[...])) * _dot(x, w3b[...])
        _rows_to_tiles(y_ref, _dot(hidden.astype(BF16), w2b[...]))

    @pl.when(j >= n_used)
    def _():
        y_ref[...] = jnp.zeros_like(y_ref)


def _moe(xs, block_e, n_used, w1, w3, w2, layer):
    n_blocks = block_e.shape[0]
    d, hid = w1.shape[-2:]
    wmap = lambda j, be, nu: (layer, be[j], 0, 0)
    tiles = MOE_BLOCK * SUBLANES
    return pl.pallas_call(
        _moe_kernel,
        grid_spec=pltpu.PrefetchScalarGridSpec(
            num_scalar_prefetch=2,
            grid=(n_blocks,),
            in_specs=[
                pl.BlockSpec((tiles, LANES), lambda j, be, nu: (jnp.minimum(j, jnp.maximum(nu[0] - 1, 0)), 0)),
                pl.BlockSpec((None, None, d, hid), wmap),
                pl.BlockSpec((None, None, d, hid), wmap),
                pl.BlockSpec((None, None, hid, d), wmap),
            ],
            out_specs=pl.BlockSpec((tiles, LANES), lambda j, be, nu: (j, 0)),
            scratch_shapes=[pltpu.VMEM((d, hid), BF16), pltpu.VMEM((d, hid), BF16), pltpu.VMEM((hid, d), BF16)],
        ),
        out_shape=jax.ShapeDtypeStruct((n_blocks * tiles, LANES), F32),
        compiler_params=_cparams(("arbitrary",)),
        name="moe_experts",
    )(block_e, n_used, xs, w1, w3, w2)


def _comb_kernel(dn_alpha, dest_ref, y_hbm, x1_ref, rw_ref, mod_ref, lng_ref, lnb_ref, o_ref, ybuf, sems):
    bb = pl.program_id(0)
    i = pl.program_id(1)
    tm = x1_ref.shape[0]
    n_tok = pl.num_programs(0) * pl.num_programs(1) * tm
    base = (bb * pl.num_programs(1) + i) * tm

    def gather(start):
        def body(r, carry):
            for kk in range(2):
                cp = _tile_copy(y_hbm, dest_ref[kk * n_tok + base + r], ybuf.at[kk], r * SUBLANES, sems.at[kk])
                if start:
                    cp.start(priority=kk)
                else:
                    cp.wait()
            return carry
        lax.fori_loop(0, tm, body, 0, unroll=8)

    gather(True)
    gather(False)
    rw = rw_ref[...]
    y = rw[:, 0:1] * _rows_from_tiles(ybuf.at[0], tm) + rw[:, 1:2] * _rows_from_tiles(ybuf.at[1], tm)
    gate2 = mod_ref[5:6, :]
    o_ref[...] = _ln(dn_alpha * x1_ref[...] + gate2 * y) * lng_ref[...] + lnb_ref[...]


def _combine(dest, y_slots, x1, rw, modtab, ln_g, ln_b, dn_alpha, seg_of_tile):
    b, n, d = x1.shape
    n_t = n // TM
    row = lambda bb, i, dr: (bb, i, 0)
    const2 = lambda bb, i, dr: (0, 0)
    return pl.pallas_call(
        functools.partial(_comb_kernel, dn_alpha),
        grid_spec=pltpu.PrefetchScalarGridSpec(
            num_scalar_prefetch=1,
            grid=(b, n_t),
            in_specs=[
                pl.BlockSpec(memory_space=pl.ANY),
                pl.BlockSpec((None, TM, d), row),
                pl.BlockSpec((None, TM, LANES), row),
                pl.BlockSpec((None, None, 6, d), lambda bb, i, dr: (bb, seg_of_tile(i), 0, 0)),
                pl.BlockSpec((1, d), const2), pl.BlockSpec((1, d), const2),
            ],
            out_specs=pl.BlockSpec((None, TM, d), row),
            scratch_shapes=[pltpu.VMEM((2, TM * SUBLANES, LANES), F32), pltpu.SemaphoreType.DMA((2,))],
        ),
        out_shape=jax.ShapeDtypeStruct((b, n, d), F32),
        compiler_params=_cparams(("arbitrary", "arbitrary")),
        name="moe_combine",
    )(dest, y_slots, x1, rw, modtab, ln_g, ln_b)


def _dispatch_tables(info, tile_counts, n_blocks):
    b, n_t, _, tm = info.shape
    info = info.astype(jnp.int32)
    expert = jnp.moveaxis(info[:, :, 0:2, :], 2, 0).reshape(2, b * n_t, tm)
    rank = jnp.moveaxis(info[:, :, 2:4, :], 2, 0).reshape(2, b * n_t, tm)
    counts = jnp.sum(tile_counts, axis=0)
    padded = (counts + MOE_BLOCK - 1) // MOE_BLOCK * MOE_BLOCK
    pad_end = jnp.cumsum(padded)
    pad_start = pad_end - padded
    tile_base = pad_start[None, :] + jnp.cumsum(tile_counts, axis=0) - tile_counts
    onehot = expert[..., None] == jnp.arange(MOE_EXPERTS, dtype=jnp.int32)
    dest = jnp.sum(jnp.where(onehot, tile_base[None, :, None, :], 0), axis=-1) + rank
    dest8 = (dest * SUBLANES).reshape(-1).astype(jnp.int32)
    block_start = jnp.arange(n_blocks, dtype=jnp.int32) * MOE_BLOCK
    block_e = jnp.minimum(jnp.sum(pad_end[None, :] <= block_start[:, None], axis=1), MOE_EXPERTS - 1).astype(jnp.int32)
    n_used = (pad_end[-1] // MOE_BLOCK).astype(jnp.int32).reshape(1)
    i32 = lambda v: v.astype(jnp.int32)
    return dest8, i32(counts), i32(pad_start), i32(padded), block_e, n_used


def _pad_heads(w, width):
    lead = w.shape[:-1]
    w = w.reshape(*lead, HEADS, width)
    w = jnp.pad(w, [(0, 0)] * len(lead) + [(0, 0), (0, LANES - width)])
    return w.reshape(*lead, PAD_W)


def _expand_matrix(width):
    e = np.zeros((HEADS * width, PAD_W), np.float32)
    for h in range(HEADS):
        e[h * width + np.arange(width), h * LANES + np.arange(width)] = 1.0
    return jnp.asarray(e, BF16)


def _rope_tables(n_ctx, n_lat):
    n_freq = DA_DH // 4
    inv_freq = ROPE_BASE ** (-jnp.arange(n_freq, dtype=F32) / n_freq)
    pos = jnp.arange(n_lat, dtype=jnp.int32)
    ang_r = (pos // GRID_W).astype(F32)[:, None] * inv_freq
    ang_c = (pos % GRID_W).astype(F32)[:, None] * inv_freq
    ang = jnp.concatenate([ang_r, ang_r, ang_c, ang_c] * 2, axis=-1)
    cos, sin = jnp.cos(ang), jnp.sin(ang)
    first = (jnp.arange(LANES) % 32) < 16
    sin_a = jnp.where(first, -sin, 0.0)
    sin_b = jnp.where(first, 0.0, sin)
    ctx_rows = lambda v: jnp.full((n_ctx, LANES), v, F32)
    return jnp.stack([jnp.concatenate([ctx_rows(1.0), cos]), jnp.concatenate([ctx_rows(0.0), sin_a]),
                      jnp.concatenate([ctx_rows(0.0), sin_b])])


def kernel(x, c, ctx, c_ctx, w_ada, b_ada, w_in, da_lambda, da_norm, ml_conv_w, ml_conv_b, ml_ib, ml_fb,
           ml_norm, gl_wa, gl_ba, gl_norm, w_out, ln_mix_g, ln_mix_b, ln_ffn_g, ln_ffn_b,
           moe_wg, moe_we, moe_w1, moe_w3, moe_w2):
    b, s, d = x.shape
    n_ctx = ctx.shape[1]
    n = n_ctx + s
    depth = w_in.shape[0]
    assert n_ctx % TM == 0 and s % TM == 0 and d % LANES == 0
    dn_alpha = (2 * depth) ** 0.25

    rows = -(-(b + 1) // SUBLANES) * SUBLANES
    cc = jnp.concatenate([c, c_ctx[None, :], jnp.zeros((rows - b - 1, d), F32)], axis=0)
    mod = _ada(cc, w_ada, b_ada).reshape(depth, rows, 6, d)

    xc = jnp.concatenate([ctx, x], axis=1)
    rope = _rope_tables(n_ctx, s)
    e64, e32 = _expand_matrix(64), _expand_matrix(32)

    for l in range(depth):
        last = l == depth - 1
        tile0 = n_ctx // TM if last else 0
        lam_init = 0.8 - 0.6 * math.exp(-0.3 * l)
        modtab = jnp.stack([jnp.broadcast_to(mod[l, b], (b, 6, d)), mod[l, :b]], axis=1)

        wi = w_in[l]
        o = np.cumsum([0, 512, 512, 512, 512, 256, 256, 8, 8, 128, 128, 256, 256, 32])
        misc_w = jnp.concatenate([wi[:, o[6]:o[8]], wi[:, o[12]:o[13]],
                                  jnp.zeros((d, LANES - 16 - 2 * GL_RANK), F32)], axis=1)
        w_in_p = jnp.concatenate([wi[:, :o[6]], wi[:, o[8]:o[12]], misc_w], axis=1).astype(BF16)

        (qa, ka, va, mq, mk, mv, mo, gq, gk, gv, gr, misc) = _in_proj(
            xc, modtab, w_in_p, rope, ml_conv_w[l], ml_conv_b[l].reshape(1, -1), e64, e32, n_ctx)

        da = _attention(qa, ka, va, da_lambda[l], da_norm[l].reshape(1, -1), lam_init, n_ctx, tile0)

        gate_bias = jnp.concatenate([ml_ib[l].reshape(-1), ml_fb[l].reshape(-1),
                                     jnp.zeros((LANES - 16,), F32)]).reshape(1, LANES)
        mhf, mhb = _mlstm(mq, mk, mv, misc, gate_bias, n_ctx)

        wa_p = jnp.zeros((2, LANES, PAD_W), F32)
        for dd in range(2):
            wa_p = wa_p.at[dd, M_GA + dd * GL_RANK:M_GA + (dd + 1) * GL_RANK].set(_pad_heads(gl_wa[l, dd], GL_DK))
        ba_p = _pad_heads(gl_ba[l], GL_DK).reshape(2, 1, PAD_W)
        ghf, ghb = _gla(gq, gk, gv, misc, wa_p.astype(BF16), ba_p, n_ctx)

        wo = w_out[l]
        wo_p = jnp.concatenate([
            wo[:512],
            jnp.pad(wo[512:768].reshape(HEADS, 64, d), ((0, 0), (0, 64), (0, 0))).reshape(PAD_W, d),
            jnp.pad(wo[768:1024].reshape(HEADS, 64, d), ((0, 0), (0, 64), (0, 0))).reshape(PAD_W, d),
        ], axis=0).astype(BF16)
        w_route = jnp.concatenate([moe_wg[l], moe_we[l],
                                   jnp.zeros((d, LANES - MOE_GROUPS - MOE_EXPERTS), F32)], axis=1)
        w_route_top = _bf16_part(w_route)
        w_route = jnp.stack([w_route_top.astype(BF16), (w_route - w_route_top).astype(BF16)])
        ml_g = _pad_heads(jnp.tile(ml_norm[l], HEADS), ML_DH).reshape(1, PAD_W)
        gl_g = _pad_heads(jnp.tile(gl_norm[l], HEADS), GL_DV).reshape(1, PAD_W)
        x1, f3, info, cnt, rw = _out_proj(xc, da, mhf, mhb, mo, ghf, ghb, gr, modtab,
                                          ln_mix_g[l].reshape(1, d), ln_mix_b[l].reshape(1, d), ml_g, gl_g,
                                          wo_p, w_route, dn_alpha, n_ctx, tile0)

        n_rows = x1.shape[1]
        n_t = n_rows // TM
        a_all = b * n_rows * 2
        n_blocks = -(-(a_all + MOE_EXPERTS * (MOE_BLOCK - 1)) // MOE_BLOCK)
        tile_counts = cnt.reshape(b * n_t, SUBLANES, LANES)[:, 0, MOE_GROUPS:MOE_GROUPS + MOE_EXPERTS]
        dest8, counts, pad_start, padded, block_e, n_used = _dispatch_tables(
            info.reshape(b, n_t, SUBLANES, TM), tile_counts.astype(jnp.int32), n_blocks)
        xs = _dispatch(dest8, counts, pad_start, padded, f3, n_blocks * MOE_BLOCK)
        y_slots = _moe(xs, block_e, n_used, moe_w1, moe_w3, moe_w2, l)
        n_ctx_tiles_here = 0 if last else n_ctx // TM
        seg = lambda i: jnp.where(i >= n_ctx_tiles_here, 1, 0)
        xc = _combine(dest8, y_slots, x1, rw, modtab, ln_ffn_g[l].reshape(1, d), ln_ffn_b[l].reshape(1, d),
                      dn_alpha, seg)
    return xc
```

```python
import functools
import math

import numpy as np
import jax
import jax.numpy as jnp
from jax import lax
from jax.experimental import pallas as pl
from jax.experimental.pallas import tpu as pltpu

F32 = jnp.float32
BF16 = jnp.bfloat16
HI = lax.Precision.HIGHEST

GRID_W = 64
HEADS = 4
DA_DH = 64
ML_DH = 64
GL_DK = 32
GL_DV = 64
GL_RANK = 16
GL_NORMALIZER = 16.0
CHUNK = 64
SUB = 16
ROPE_BASE = 10000.0
MOE_GROUPS = 4
MOE_PER_GROUP = 8
MOE_EXPERTS = 32
MOE_HIDDEN = 512
LN_EPS = 1e-6

LANES = 128
SUBLANES = 8
VMEM_LIMIT = 48 * 1024 * 1024

TM = 256
MOE_BLOCK = 256
SCAN_NB = 2
ATTN_NQ = 2
PAD_W = HEADS * LANES

C_QA, C_KA, C_VA, C_MQK, C_MV, C_MO = 0, 512, 1024, 1536, 2048, 2304
C_GQ, C_GK, C_GV, C_GR, C_MISC, IN_WP = 2560, 2688, 2816, 3072, 3328, 3456
M_IG, M_FG, M_GA = 0, 8, 16


def _cparams(sem):
    return pltpu.CompilerParams(dimension_semantics=sem, vmem_limit_bytes=VMEM_LIMIT)


def _sigmoid(x):
    return 1.0 / (1.0 + jnp.exp(-x))


def _silu(x):
    return x * _sigmoid(x)


def _log_sigmoid(x):
    return jnp.minimum(x, 0.0) - jnp.log(1.0 + jnp.exp(-jnp.abs(x)))


def _ln(x):
    mu = jnp.mean(x, axis=-1, keepdims=True)
    xc = x - mu
    var = jnp.mean(xc * xc, axis=-1, keepdims=True)
    return xc * lax.rsqrt(var + LN_EPS)


def _bf16_part(x):
    bits = lax.bitcast_convert_type(x, jnp.int32) & jnp.int32(-65536)
    return lax.bitcast_convert_type(bits, F32)


def _dot(a, b):
    return jnp.dot(a, b, preferred_element_type=F32)


def _dot_hi(a, b):
    return jnp.dot(a, b, precision=HI, preferred_element_type=F32)


def _cumsum_dot(a, b, split_lhs=False):
    x = a if split_lhs else b
    x0 = _bf16_part(x)
    r1 = x - x0
    x1 = _bf16_part(r1)
    x2 = r1 - x1
    parts = [p.astype(BF16) for p in (x0, x1, x2)]
    if split_lhs:
        return _dot(parts[0], b) + (_dot(parts[1], b) + _dot(parts[2], b))
    return _dot(a, parts[0]) + (_dot(a, parts[1]) + _dot(a, parts[2]))


def _dot_nt(a, b):
    return lax.dot_general(a, b, (((1,), (1,)), ((), ())), preferred_element_type=F32)


def _dot_tn(a, b):
    return lax.dot_general(a, b, (((0,), (0,)), ((), ())), preferred_element_type=F32)


def _ada_kernel(c_ref, w_ref, b_ref, o_ref):
    o_ref[...] = _dot_hi(_silu(c_ref[...]), w_ref[...]) + b_ref[...]


def _ada(cc, w_ada, b_ada):
    depth, d, n6 = w_ada.shape
    tn = 1536
    rows = cc.shape[0]
    return pl.pallas_call(
        _ada_kernel,
        grid=(depth, n6 // tn),
        in_specs=[
            pl.BlockSpec((rows, d), lambda l, j: (0, 0)),
            pl.BlockSpec((None, d, tn), lambda l, j: (l, 0, j)),
            pl.BlockSpec((None, 1, tn), lambda l, j: (l, 0, j)),
        ],
        out_specs=pl.BlockSpec((None, rows, tn), lambda l, j: (l, 0, j)),
        out_shape=jax.ShapeDtypeStruct((depth, rows, n6), F32),
        compiler_params=_cparams(("arbitrary", "arbitrary")),
        name="ada_mod",
    )(cc, w_ada, b_ada.reshape(depth, 1, n6))


def _in_kernel(n_ctx_tiles, x_ref, xp_ref, xn_ref, mod_ref, w_ref, rope_ref, cw_ref, cb_ref,
               e64_ref, e32_ref,
               qa_ref, ka_ref, va_ref, mq_ref, mk_ref, mv_ref, mo_ref,
               gq_ref, gk_ref, gv_ref, gr_ref, misc_ref, pext_ref):
    i = pl.program_id(1)
    n_tiles = pl.num_programs(1)
    tm = x_ref.shape[0]
    shift, scale = mod_ref[0:1, :], mod_ref[1:2, :]

    def prenorm(x):
        return (_ln(x) * (1.0 + scale) + shift).astype(BF16)

    h = prenorm(x_ref[...])
    halo = jnp.concatenate([prenorm(xp_ref[...]), prenorm(xn_ref[...])], axis=0)

    parts = {lo: _dot(h, w_ref[:, lo:hi]) for lo, hi in (
        (C_QA, C_KA), (C_KA, C_VA), (C_VA, C_MQK), (C_MQK, C_MV), (C_MV, C_MO), (C_MO, C_GQ),
        (C_GQ, C_GK), (C_GK, C_GV), (C_GV, C_GR), (C_GR, C_MISC), (C_MISC, IN_WP))}
    p_halo = _dot(halo, w_ref[:, C_MQK:C_MV])

    def proj(a, lo, hi):
        return parts[lo]

    cos, sin_a, sin_b = rope_ref[0], rope_ref[1], rope_ref[2]
    for lo, out_ref, mult in ((C_QA, qa_ref, DA_DH ** -0.5 * math.log2(math.e)), (C_KA, ka_ref, 1.0)):
        acc = proj(h, lo, lo + PAD_W)
        for hh in range(HEADS):
            xh = acc[:, hh * LANES:(hh + 1) * LANES]
            r = xh * cos + pltpu.roll(xh, LANES - 16, 1) * sin_a + pltpu.roll(xh, 16, 1) * sin_b
            out_ref[:, hh * LANES:(hh + 1) * LANES] = (r * mult).astype(BF16)
    va_ref[...] = proj(h, C_VA, C_VA + PAD_W).astype(BF16)

    seg_first = jnp.logical_or(i == 0, i == n_ctx_tiles)
    seg_last = jnp.logical_or(i == n_ctx_tiles - 1, i == n_tiles - 1)
    keep_prev = jnp.where(seg_first, 0.0, 1.0)
    keep_next = jnp.where(seg_last, 0.0, 1.0)
    p = proj(h, C_MQK, C_MQK + 512)
    pext_ref[0:SUBLANES, :] = p_halo[0:SUBLANES] * keep_prev
    pext_ref[SUBLANES:SUBLANES + tm, :] = p
    pext_ref[SUBLANES + tm:2 * SUBLANES + tm, :] = p_halo[SUBLANES:2 * SUBLANES] * keep_next
    y = (cw_ref[0:1, :] * pext_ref[SUBLANES - 1:SUBLANES - 1 + tm, :] + cw_ref[1:2, :] * p
         + cw_ref[2:3, :] * pext_ref[SUBLANES + 1:SUBLANES + 1 + tm, :] + cb_ref[...])
    y = _silu(y)
    e64 = e64_ref[...]
    e32 = e32_ref[...]
    mq_ref[...] = _dot((y[:, :256] * ML_DH ** -0.5).astype(BF16), e64).astype(BF16)
    mk_ref[...] = _dot(y[:, 256:].astype(BF16), e64).astype(BF16)
    lane = lax.broadcasted_iota(jnp.int32, (tm, PAD_W), 1)
    ones_col = jnp.where(lane % LANES >= ML_DH, 1.0, 0.0)
    mv_ref[...] = (_dot(proj(h, C_MV, C_MV + 256).astype(BF16), e64) + ones_col).astype(BF16)
    mo_ref[...] = _dot(proj(h, C_MO, C_MO + 256).astype(BF16), e64).astype(BF16)

    gq_ref[...] = _dot((proj(h, C_GQ, C_GQ + 128) * GL_DK ** -0.5).astype(BF16), e32).astype(BF16)
    gk_ref[...] = _dot(proj(h, C_GK, C_GK + 128).astype(BF16), e32).astype(BF16)
    gv_ref[...] = _dot(proj(h, C_GV, C_GV + 256).astype(BF16), e64).astype(BF16)
    gr_ref[...] = _dot(proj(h, C_GR, C_GR + 256).astype(BF16), e64).astype(BF16)
    misc_ref[...] = proj(h, C_MISC, C_MISC + LANES)


def _in_proj(xc, modtab, w_in_p, rope, conv_w, conv_b, e64, e32, n_ctx):
    b, n, d = xc.shape
    n_tiles = n // TM
    n_ctx_tiles = n_ctx // TM
    hb = TM // SUBLANES
    n_hb = n // SUBLANES
    row = lambda bb, i: (bb, i, 0)
    const2 = lambda bb, i: (0, 0)
    outs = [jax.ShapeDtypeStruct((b, n, PAD_W), BF16)] * 11 + [jax.ShapeDtypeStruct((b, n, LANES), F32)]
    out_specs = [pl.BlockSpec((None, TM, PAD_W), row)] * 11 + [pl.BlockSpec((None, TM, LANES), row)]
    return pl.pallas_call(
        functools.partial(_in_kernel, n_ctx_tiles),
        grid=(b, n_tiles),
        in_specs=[
            pl.BlockSpec((None, TM, d), row),
            pl.BlockSpec((None, SUBLANES, d), lambda bb, i: (bb, jnp.maximum(i * hb - 1, 0), 0)),
            pl.BlockSpec((None, SUBLANES, d), lambda bb, i: (bb, jnp.minimum((i + 1) * hb, n_hb - 1), 0)),
            pl.BlockSpec((None, None, 6, d), lambda bb, i: (bb, jnp.where(i >= n_ctx_tiles, 1, 0), 0, 0)),
            pl.BlockSpec((d, IN_WP), const2),
            pl.BlockSpec((3, TM, LANES), lambda bb, i: (0, i, 0)),
            pl.BlockSpec((3, 512), const2),
            pl.BlockSpec((1, 512), const2),
            pl.BlockSpec((256, PAD_W), const2),
            pl.BlockSpec((128, PAD_W), const2),
        ],
        out_specs=out_specs,
        out_shape=outs,
        scratch_shapes=[pltpu.VMEM((TM + 2 * SUBLANES, 512), F32)],
        compiler_params=_cparams(("parallel", "arbitrary")),
        name="in_proj",
    )(xc, xc, xc, modtab, w_in_p, rope, conv_w, conv_b, e64, e32)


def _attn_kernel(lam_init, n_q, *refs):
    q_refs = refs[:n_q]
    k_ref, v_ref, lamp_ref, g_ref, o_ref = refs[n_q:]
    lp = lamp_ref[...]
    lam = (jnp.exp(jnp.sum(lp[0:1] * lp[1:2], axis=1, keepdims=True))
           - jnp.exp(jnp.sum(lp[2:3] * lp[3:4], axis=1, keepdims=True)) + lam_init)
    k = k_ref[...]
    v = v_ref[...]
    tq = q_refs[0].shape[0]
    lane = lax.broadcasted_iota(jnp.int32, (tq, LANES), 1)
    scores = []
    for q_ref in q_refs:
        q = q_ref[...]
        zero = jnp.zeros_like(q)
        scores.append((_dot_nt(jnp.where(lane < DA_DH, q, zero), k), _dot_nt(jnp.where(lane >= DA_DH, q, zero), k)))
    for t, (s1, s2) in enumerate(scores):
        e1 = jnp.exp2(s1 - jnp.max(s1, axis=1, keepdims=True))
        e2 = jnp.exp2(s2 - jnp.max(s2, axis=1, keepdims=True))
        r1 = 1.0 / jnp.sum(e1, axis=1, keepdims=True)
        r2 = lam / jnp.sum(e2, axis=1, keepdims=True)
        o = _dot((e1 * r1 - e2 * r2).astype(BF16), v)
        y = o * lax.rsqrt(jnp.mean(o * o, axis=1, keepdims=True) + LN_EPS) * g_ref[...]
        o_ref[t * tq:(t + 1) * tq, :] = (y * (1.0 - lam_init)).astype(BF16)


def _attention(qa, ka, va, lam_p, norm_g, lam_init, tile0, n_tiles, n_keys, n_q):
    b = qa.shape[0]
    tq = TM
    q_spec = lambda t: pl.BlockSpec((None, tq, LANES), lambda bb, h, i: (bb, tile0 + i * n_q + t, h))
    kv_spec = pl.BlockSpec((None, n_keys, LANES), lambda bb, h, i: (bb, 0, h))
    return pl.pallas_call(
        functools.partial(_attn_kernel, lam_init, n_q),
        grid=(b, HEADS, n_tiles // n_q),
        in_specs=[q_spec(t) for t in range(n_q)] + [
            kv_spec, kv_spec,
            pl.BlockSpec((4, DA_DH), lambda bb, h, i: (0, 0)),
            pl.BlockSpec((1, LANES), lambda bb, h, i: (0, 0)),
        ],
        out_specs=pl.BlockSpec((None, n_q * tq, LANES), lambda bb, h, i: (bb, i, h)),
        out_shape=jax.ShapeDtypeStruct((b, n_tiles * tq, PAD_W), BF16),
        compiler_params=_cparams(("parallel", "parallel", "arbitrary")),
        name="diff_attn",
    )(*([qa] * n_q), ka, va, lam_p, norm_g)


def _chunk_maps(n_ctx_chunks, n_chunks):
    def fwd(bb, t):
        return (bb, t, 0)

    def bwd(bb, t):
        return (bb, jnp.where(t < n_ctx_chunks, n_ctx_chunks - 1 - t, n_chunks - 1 + n_ctx_chunks - t), 0)

    return fwd, bwd


def _tri(direction):
    l = lax.broadcasted_iota(jnp.int32, (CHUNK, CHUNK), 0)
    s = lax.broadcasted_iota(jnp.int32, (CHUNK, CHUNK), 1)
    return (s <= l) if direction == 0 else (s >= l)


def _mlstm_kernel(qf_ref, kf_ref, vf_ref, gf_ref, qb_ref, kb_ref, vb_ref, gb_ref, bias_ref, sel_ref,
                  hf_ref, hb_ref, c_ref, m_ref):
    t = pl.program_id(1)

    @pl.when(t == 0)
    def _():
        c_ref[...] = jnp.zeros_like(c_ref)
        m_ref[...] = jnp.zeros_like(m_ref)

    lane = lax.broadcasted_iota(jnp.int32, (CHUNK, LANES), 1)
    is_fg = jnp.logical_and(lane >= M_FG, lane < M_GA)
    dirs = ((qf_ref, kf_ref, vf_ref, gf_ref, hf_ref), (qb_ref, kb_ref, vb_ref, gb_ref, hb_ref))
    n_b = qf_ref.shape[0]
    masks = [_tri(d) for d in range(2)]
    units = [(d, nb, h) for d in range(2) for nb in range(n_b) for h in range(HEADS)]

    pre = {}
    for d in range(2):
        tri = jnp.where(masks[d], 1.0, 0.0).astype(BF16)
        for nb in range(n_b):
            g = dirs[d][3][nb] + bias_ref[...]
            g = jnp.where(is_fg, _log_sigmoid(g), g)
            cum = _cumsum_dot(tri, g)
            pre[d, nb] = (_cumsum_dot(cum, sel_ref[d, 0], split_lhs=True),
                          _cumsum_dot(g, sel_ref[d, 1], split_lhs=True), g.T, cum.T)

    qk, qc = {}, {}
    for u in units:
        d, nb, h = u
        hs = slice(h * LANES, (h + 1) * LANES)
        q = dirs[d][0][nb, :, hs]
        qk[u] = _dot_nt(q, dirs[d][1][nb, :, hs])
        qc[u] = _dot(q, c_ref[nb, d, h].astype(BF16))

    gate = {}
    for u in units:
        d, nb, h = u
        hs = slice(h * LANES, (h + 1) * LANES)
        b_all, i_all, g_t, cum_t = pre[d, nb]
        ci, cf = M_IG + HEADS * d + h, M_FG + HEADS * d + h
        last = CHUNK - 1 if d == 0 else 0
        b_rep = b_all[:, hs]
        i_rep = i_all[:, hs]
        row = g_t[ci:ci + 1, :] - cum_t[cf:cf + 1, :]
        m_prev = m_ref[nb, d, h][0:1, :]
        log_d = jnp.where(masks[d], b_rep[:, 0:CHUNK] + row, -jnp.inf)
        log_a = b_rep + m_prev
        m_t = jnp.maximum(log_a, jnp.max(log_d, axis=1, keepdims=True))
        m_new = m_t[last:last + 1, :]
        b_last = b_rep[last:last + 1, :]
        gate[u] = (jnp.exp(log_d - m_t[:, 0:CHUNK]), jnp.exp(log_a - m_t), jnp.exp(-m_t),
                   jnp.exp(b_last - b_rep + i_rep - m_new), jnp.exp(b_last + m_prev - m_new), m_new)

    sv, kv = {}, {}
    for u in units:
        d, nb, h = u
        hs = slice(h * LANES, (h + 1) * LANES)
        v = dirs[d][2][nb, :, hs]
        dmat, _, _, w, _, _ = gate[u]
        sv[u] = _dot((qk[u] * dmat).astype(BF16), v)
        kv[u] = _dot_tn((dirs[d][1][nb, :, hs].astype(F32) * w).astype(BF16), v)

    for u in units:
        d, nb, h = u
        hs = slice(h * LANES, (h + 1) * LANES)
        _, a, floor, _, decay, m_new = gate[u]
        num = a * qc[u] + sv[u]
        qn = pltpu.roll(num, ML_DH, 1)
        den = jnp.maximum(jnp.abs(qn), floor)
        dirs[d][4][nb, :, hs] = jnp.where(lane < ML_DH, num / den, 0.0)
        c_ref[nb, d, h] = decay * c_ref[nb, d, h] + kv[u]
        m_ref[nb, d, h] = jnp.broadcast_to(m_new, (SUBLANES, LANES))


def _gate_select_matrices():
    sel = np.zeros((2, 2, LANES, PAD_W), np.float32)
    for d in range(2):
        for h in range(HEADS):
            sel[d, 0, M_FG + HEADS * d + h, h * LANES:(h + 1) * LANES] = 1.0
            sel[d, 1, M_IG + HEADS * d + h, h * LANES:(h + 1) * LANES] = 1.0
    return jnp.asarray(sel, BF16)


def _mlstm(mq, mk, mv, misc, gate_bias, n_ctx):
    b, n, _ = mq.shape
    n_chunks = n // CHUNK
    fwd, bwd = _chunk_maps(n_ctx // CHUNK, n_chunks)
    wide = lambda m: pl.BlockSpec((SCAN_NB, CHUNK, PAD_W), m)
    thin = lambda m: pl.BlockSpec((SCAN_NB, CHUNK, LANES), m)
    return pl.pallas_call(
        _mlstm_kernel,
        grid=(b // SCAN_NB, n_chunks),
        in_specs=[wide(fwd), wide(fwd), wide(fwd), thin(fwd), wide(bwd), wide(bwd), wide(bwd), thin(bwd),
                  pl.BlockSpec((1, LANES), lambda bb, t: (0, 0)),
                  pl.BlockSpec((2, 2, LANES, PAD_W), lambda bb, t: (0, 0, 0, 0))],
        out_specs=[wide(fwd), wide(bwd)],
        out_shape=[jax.ShapeDtypeStruct((b, n, PAD_W), F32)] * 2,
        scratch_shapes=[pltpu.VMEM((SCAN_NB, 2, HEADS, LANES, LANES), F32),
                        pltpu.VMEM((SCAN_NB, 2, HEADS, SUBLANES, LANES), F32)],
        compiler_params=_cparams(("parallel", "arbitrary")),
        name="mlstm_scan",
    )(mq, mk, mv, misc, mq, mk, mv, misc, gate_bias, _gate_select_matrices())


def _gla_kernel(qf_ref, kf_ref, vf_ref, gf_ref, qb_ref, kb_ref, vb_ref, gb_ref, wa_ref, ba_ref,
                hf_ref, hb_ref, st_ref):
    t = pl.program_id(1)

    @pl.when(t == 0)
    def _():
        st_ref[...] = jnp.zeros_like(st_ref)

    n_sub = CHUNK // SUB
    row_id = lax.broadcasted_iota(jnp.int32, (CHUNK, LANES), 0)
    row_blk = row_id // SUB
    col_id = lax.broadcasted_iota(jnp.int32, (CHUNK, LANES), 1)
    rows_w = lax.broadcasted_iota(jnp.int32, (CHUNK, PAD_W), 0)
    zeros = jnp.zeros((CHUNK, LANES), BF16)
    dirs = ((qf_ref, kf_ref, vf_ref, gf_ref, hf_ref), (qb_ref, kb_ref, vb_ref, gb_ref, hb_ref))
    n_b = qf_ref.shape[0]
    pairs = [(d, nb) for d in range(2) for nb in range(n_b)]
    units = [(d, nb, h) for d, nb in pairs for h in range(HEADS)]
    tris = [jnp.where(_tri(d), 1.0, 0.0).astype(BF16) for d in range(2)]
    causal = [col_id <= row_id, jnp.logical_and(col_id >= row_id, col_id < CHUNK)]

    las = {p: _log_sigmoid(_dot(dirs[p[0]][3][p[1]].astype(BF16), wa_ref[p[0]]) + ba_ref[p[0]])
           * (1.0 / GL_NORMALIZER) for p in pairs}
    cums = {p: _cumsum_dot(tris[p[0]], las[p]) for p in pairs}

    pre = {}
    for p in pairs:
        d, nb = p
        cum = cums[p]
        excl = cum - las[p]
        last = CHUNK - 1 if d == 0 else 0
        betas = [excl[r:r + 1, :] for r in (range(0, CHUNK, SUB) if d == 0 else range(SUB - 1, CHUNK, SUB))]
        beta_rows = jnp.concatenate([jnp.broadcast_to(bt, (SUB, PAD_W)) for bt in betas], axis=0)
        q = dirs[d][0][nb].astype(F32)
        k = dirs[d][1][nb].astype(F32)
        q_loc = (q * jnp.exp(cum - beta_rows)).astype(BF16)
        q_abs = (q * jnp.exp(cum)).astype(BF16)
        b_last = cum[last:last + 1, :]
        k_end = (k * jnp.exp(b_last - cum)).astype(BF16)
        k_subs = []
        for I in range(n_sub):
            seen = (rows_w < (I + 1) * SUB) if d == 0 else (rows_w >= I * SUB)
            k_subs.append((k * jnp.exp(jnp.where(seen, betas[I] - cum, -jnp.inf))).astype(BF16))
        pre[p] = (q_loc, q_abs, b_last, k_end, k_subs)

    scores, inter, kv = {}, {}, {}
    for u in units:
        d, nb, h = u
        hs = slice(h * LANES, (h + 1) * LANES)
        q_loc, q_abs, _, k_end, k_subs = pre[d, nb]
        kcat = jnp.concatenate([x for I in range(n_sub) for x in (k_subs[I][:, hs], zeros)], axis=0)
        scores[u] = _dot_nt(q_loc[:, hs], kcat)
        inter[u] = _dot_nt(q_abs[:, hs], st_ref[nb, d, h].astype(BF16))
        kv[u] = _dot_tn(dirs[d][2][nb, :, hs], k_end[:, hs])

    att = {}
    for u in units:
        r = scores[u]
        a = r[:, 0:LANES]
        for I in range(1, n_sub):
            a = jnp.where(row_blk == I, r[:, I * LANES:(I + 1) * LANES], a)
        att[u] = jnp.where(causal[u[0]], a, 0.0)[:, 0:CHUNK].astype(BF16)

    av = {u: _dot(att[u], dirs[u[0]][2][u[1], :, u[2] * LANES:(u[2] + 1) * LANES]) for u in units}
    for u in units:
        d, nb, h = u
        hs = slice(h * LANES, (h + 1) * LANES)
        dirs[d][4][nb, :, hs] = inter[u] + av[u]
        st_ref[nb, d, h] = st_ref[nb, d, h] * jnp.exp(pre[d, nb][2][:, hs]) + kv[u]


def _gla(gq, gk, gv, misc, wa_p, ba_p, n_ctx):
    b, n, _ = gq.shape
    n_chunks = n // CHUNK
    fwd, bwd = _chunk_maps(n_ctx // CHUNK, n_chunks)
    wide = lambda m: pl.BlockSpec((SCAN_NB, CHUNK, PAD_W), m)
    thin = lambda m: pl.BlockSpec((SCAN_NB, CHUNK, LANES), m)
    return pl.pallas_call(
        _gla_kernel,
        grid=(b // SCAN_NB, n_chunks),
        in_specs=[wide(fwd), wide(fwd), wide(fwd), thin(fwd), wide(bwd), wide(bwd), wide(bwd), thin(bwd),
                  pl.BlockSpec((2, LANES, PAD_W), lambda bb, t: (0, 0, 0)),
                  pl.BlockSpec((2, 1, PAD_W), lambda bb, t: (0, 0, 0))],
        out_specs=[wide(fwd), wide(bwd)],
        out_shape=[jax.ShapeDtypeStruct((b, n, PAD_W), F32)] * 2,
        scratch_shapes=[pltpu.VMEM((SCAN_NB, 2, HEADS, LANES, LANES), F32)],
        compiler_params=_cparams(("parallel", "arbitrary")),
        name="gla_scan",
    )(gq, gk, gv, misc, gq, gk, gv, misc, wa_p, ba_p)


def _head_rms_padded(x, width):
    parts = []
    for h in range(HEADS):
        xh = x[:, h * LANES:(h + 1) * LANES]
        ms = jnp.sum(xh * xh, axis=1, keepdims=True) * (1.0 / width)
        parts.append(xh * lax.rsqrt(ms + LN_EPS))
    return jnp.concatenate(parts, axis=1)


def _out_kernel(dn_alpha, n_ctx_tiles, x_ref, dac_ref, dal_ref, mhf_ref, mhb_ref, mo_ref, ghf_ref, ghb_ref, gr_ref,
                mod_ref, lng_ref, lnb_ref, mlg_ref, glg_ref, wo_ref, wr_ref,
                x1_ref, f_ref, info_ref, cnt_ref, rw_ref):
    ml = _head_rms_padded(mhf_ref[...] + mhb_ref[...], ML_DH) * mlg_ref[...] * _sigmoid(mo_ref[...].astype(F32))
    gl = _head_rms_padded(ghf_ref[...] + ghb_ref[...], GL_DV) * glg_ref[...] * _silu(gr_ref[...].astype(F32))
    da = jnp.where(pl.program_id(1) < n_ctx_tiles, dac_ref[...], dal_ref[...])
    y = (_dot(da, wo_ref[0:PAD_W, :]) + _dot(ml.astype(BF16), wo_ref[PAD_W:2 * PAD_W, :])
         + _dot(gl.astype(BF16), wo_ref[2 * PAD_W:3 * PAD_W, :]))
    gate1, shift2, scale2 = mod_ref[2:3, :], mod_ref[3:4, :], mod_ref[4:5, :]
    x1 = _ln(dn_alpha * x_ref[...] + gate1 * y) * lng_ref[...] + lnb_ref[...]
    x1_ref[...] = x1
    f = _ln(x1) * (1.0 + scale2) + shift2
    tm = f.shape[0]
    for s in range(SUBLANES):
        f_ref[pl.ds(s, tm, stride=SUBLANES), :] = f[:, s * LANES:(s + 1) * LANES]

    f_top = _bf16_part(f)
    f_hi, f_lo = f_top.astype(BF16), (f - f_top).astype(BF16)
    logits = _dot(f_hi, wr_ref[0]) + (_dot(f_lo, wr_ref[0]) + _dot(f_hi, wr_ref[1]) + _dot(f_lo, wr_ref[1]))
    lane = lax.broadcasted_iota(jnp.int32, logits.shape, 1)
    neg = -jnp.inf
    big = jnp.int32(LANES)

    def first_argmax(vals, vmax):
        return jnp.min(jnp.where(vals == vmax, lane, big), axis=1, keepdims=True)

    gl_ = jnp.where(lane < MOE_GROUPS, logits, neg)
    gmax = jnp.max(gl_, axis=1, keepdims=True)
    g_idx = first_argmax(gl_, gmax)
    g_gate = 1.0 / jnp.sum(jnp.exp(gl_ - gmax), axis=1, keepdims=True)
    in_group = jnp.logical_and(lane >= MOE_GROUPS, (lane - MOE_GROUPS) // MOE_PER_GROUP == g_idx)
    el = jnp.where(in_group, logits, neg)
    emax = jnp.max(el, axis=1, keepdims=True)
    e0 = first_argmax(el, emax)
    el2 = jnp.where(lane == e0, neg, el)
    emax2 = jnp.max(el2, axis=1, keepdims=True)
    e1 = first_argmax(el2, emax2)
    esum = jnp.sum(jnp.exp(el - emax), axis=1, keepdims=True)
    p0 = 1.0 / esum
    p1 = jnp.exp(emax2 - emax) / esum
    w0 = g_gate * p0 / (p0 + p1)
    w1 = g_gate * p1 / (p0 + p1)
    rw_ref[...] = jnp.where(lane == 0, w0, jnp.where(lane == 1, w1, 0.0))

    oh0 = jnp.where(lane == e0, 1.0, 0.0)
    oh1 = jnp.where(lane == e1, 1.0, 0.0)
    both = (oh0 + oh1).astype(BF16)
    r_id = lax.broadcasted_iota(jnp.int32, (tm, tm), 0)
    c_id = lax.broadcasted_iota(jnp.int32, (tm, tm), 1)
    earlier = jnp.where(c_id < r_id, 1.0, 0.0).astype(BF16)
    before = _dot(earlier, both)
    rank0 = jnp.sum(before * oh0, axis=1, keepdims=True)
    rank1 = jnp.sum(before * oh1, axis=1, keepdims=True)
    cnt_ref[...] = _dot(jnp.ones((SUBLANES, tm), BF16), both)
    info = jnp.where(lane == 0, (e0 - MOE_GROUPS).astype(F32),
                     jnp.where(lane == 1, (e1 - MOE_GROUPS).astype(F32),
                               jnp.where(lane == 2, rank0, jnp.where(lane == 3, rank1, 0.0))))
    info_ref[...] = info.T[0:SUBLANES, :]


def _out_proj(xc, da_ctx, da_lat, mhf, mhb, mo, ghf, ghb, gr, modtab, ln_g, ln_b, ml_g, gl_g, wo_p, w_route,
              dn_alpha, n_ctx, tile0):
    b, n, d = xc.shape
    n_ctx_tiles = n_ctx // TM
    n_t = n // TM - tile0
    out_ctx_tiles = n_ctx_tiles - tile0
    row = lambda bb, i: (bb, i + tile0, 0)
    orow = lambda bb, i: (bb, i, 0)
    const2 = lambda bb, i: (0, 0)
    wide = pl.BlockSpec((None, TM, PAD_W), row)
    n_out = n_t * TM
    if da_ctx is None:
        da_ctx = da_lat
    ctx_rows = lambda bb, i: (bb, jnp.minimum(i, max(out_ctx_tiles - 1, 0)), 0)
    lat_rows = lambda bb, i: (bb, jnp.maximum(i - out_ctx_tiles, 0), 0)
    return pl.pallas_call(
        functools.partial(_out_kernel, dn_alpha, out_ctx_tiles),
        grid=(b, n_t),
        in_specs=[
            pl.BlockSpec((None, TM, d), row),
            pl.BlockSpec((None, TM, PAD_W), ctx_rows), pl.BlockSpec((None, TM, PAD_W), lat_rows),
            wide, wide, wide, wide, wide, wide,
            pl.BlockSpec((None, None, 6, d), lambda bb, i: (bb, jnp.where(i + tile0 >= n_ctx_tiles, 1, 0), 0, 0)),
            pl.BlockSpec((1, d), const2), pl.BlockSpec((1, d), const2),
            pl.BlockSpec((1, PAD_W), const2), pl.BlockSpec((1, PAD_W), const2),
            pl.BlockSpec((3 * PAD_W, d), const2),
            pl.BlockSpec((2, d, LANES), lambda bb, i: (0, 0, 0)),
        ],
        out_specs=[pl.BlockSpec((None, TM, d), orow),
                   pl.BlockSpec((None, TM * SUBLANES, LANES), orow),
                   pl.BlockSpec((None, SUBLANES, TM), orow),
                   pl.BlockSpec((None, SUBLANES, LANES), orow),
                   pl.BlockSpec((None, TM, LANES), orow)],
        out_shape=[jax.ShapeDtypeStruct((b, n_out, d), F32),
                   jax.ShapeDtypeStruct((b, n_out * SUBLANES, LANES), F32),
                   jax.ShapeDtypeStruct((b, n_t * SUBLANES, TM), F32),
                   jax.ShapeDtypeStruct((b, n_t * SUBLANES, LANES), F32),
                   jax.ShapeDtypeStruct((b, n_out, LANES), F32)],
        compiler_params=_cparams(("parallel", "arbitrary")),
        name="out_proj",
    )(xc, da_ctx, da_lat, mhf, mhb, mo, ghf, ghb, gr, modtab, ln_g, ln_b, ml_g, gl_g, wo_p, w_route)


def _tile_copy(src, src_tile, dst, dst_tile, sem):
    aligned = lambda t: t if isinstance(t, int) else pl.multiple_of(t, SUBLANES)
    return pltpu.make_async_copy(src.at[pl.ds(aligned(src_tile), SUBLANES), :],
                                 dst.at[pl.ds(aligned(dst_tile), SUBLANES), :], sem)


def _rows_from_tiles(ref, n_rows):
    return jnp.concatenate([ref[pl.ds(s, n_rows, stride=SUBLANES), :] for s in range(SUBLANES)], axis=1)


def _rows_to_tiles(ref, val):
    n_rows = val.shape[0]
    for s in range(SUBLANES):
        ref[pl.ds(s, n_rows, stride=SUBLANES), :] = val[:, s * LANES:(s + 1) * LANES]


def _disp_kernel(dest_ref, cnt_ref, start_ref, padded_ref, f_ref, xs_hbm, zero_ref, sem, zsem, bsem):
    bb = pl.program_id(0)
    i = pl.program_id(1)
    tm = f_ref.shape[0] // SUBLANES
    n_tok = pl.num_programs(0) * pl.num_programs(1) * tm
    base = (bb * pl.num_programs(1) + i) * tm

    def scatter(start):
        def body(r, carry):
            for kk in range(2):
                cp = _tile_copy(f_ref, r * SUBLANES, xs_hbm, dest_ref[kk * n_tok + base + r], sem)
                if start:
                    cp.start(priority=kk)
                else:
                    cp.wait()
            return carry
        lax.fori_loop(0, tm, body, 0, unroll=8)

    scatter(True)
    scatter(False)

    @pl.when(jnp.logical_and(bb == pl.num_programs(0) - 1, i == pl.num_programs(1) - 1))
    def _():
        zero_ref[...] = jnp.zeros_like(zero_ref)
        n_slots = xs_hbm.shape[0] // SUBLANES
        last = MOE_EXPERTS - 1

        blk_tiles = zero_ref.shape[0]

        def fill(start):
            def row(r, c2):
                cp = _tile_copy(zero_ref, 0, xs_hbm, r * SUBLANES, zsem)
                if start:
                    cp.start()
                else:
                    cp.wait()
                return c2

            def block(j, c2):
                cp = pltpu.make_async_copy(zero_ref, xs_hbm.at[pl.ds(pl.multiple_of(j * blk_tiles, blk_tiles),
                                                                     blk_tiles), :], bsem)
                if start:
                    cp.start()
                else:
                    cp.wait()
                return c2

            def per_expert(e, carry):
                return lax.fori_loop(start_ref[e] + cnt_ref[e], start_ref[e] + padded_ref[e], row, carry)
            lax.fori_loop(0, MOE_EXPERTS, per_expert, 0)
            used_blocks = (start_ref[last] + padded_ref[last]) // (blk_tiles // SUBLANES)
            lax.fori_loop(used_blocks, n_slots * SUBLANES // blk_tiles, block, 0)

        fill(True)
        fill(False)


def _dispatch(dest8, counts, pad_start, padded, f3, n_slots):
    b, rows, _ = f3.shape
    n_t = rows // (TM * SUBLANES)
    return pl.pallas_call(
        _disp_kernel,
        grid_spec=pltpu.PrefetchScalarGridSpec(
            num_scalar_prefetch=4,
            grid=(b, n_t),
            in_specs=[pl.BlockSpec((TM * SUBLANES, LANES), lambda bb, i, *_: (bb * n_t + i, 0))],
            out_specs=pl.BlockSpec(memory_space=pl.ANY),
            scratch_shapes=[pltpu.VMEM((MOE_BLOCK * SUBLANES, LANES), F32), pltpu.SemaphoreType.DMA(()),
                            pltpu.SemaphoreType.DMA(()), pltpu.SemaphoreType.DMA(())],
        ),
        out_shape=jax.ShapeDtypeStruct((n_slots * SUBLANES, LANES), F32),
        compiler_params=_cparams(("arbitrary", "arbitrary")),
        name="moe_dispatch",
    )(dest8, counts, pad_start, padded, f3.reshape(b * rows, LANES))


def _moe_kernel(be_ref, nu_ref, xs_ref, w1_ref, w3_ref, w2_ref, y_ref, w1b, w3b, w2b):
    j = pl.program_id(0)
    n_used = nu_ref[0]
    blk = xs_ref.shape[0] // SUBLANES

    @pl.when(jnp.logical_and(j < n_used, jnp.logical_or(j == 0, be_ref[j] != be_ref[jnp.maximum(j - 1, 0)])))
    def _():
        w1b[...] = w1_ref[...].astype(BF16)
        w3b[...] = w3_ref[...].astype(BF16)
        w2b[...] = w2_ref[...].astype(BF16)

    @pl.when(j < n_used)
    def _():
        x = _rows_from_tiles(xs_ref, blk).astype(BF16)
        hidden = _silu(_dot(x, w1b[...])) * _dot(x, w3b[...])
        _rows_to_tiles(y_ref, _dot(hidden.astype(BF16), w2b[...]))

    @pl.when(j >= n_used)
    def _():
        y_ref[...] = jnp.zeros_like(y_ref)


def _moe(xs, block_e, n_used, w1, w3, w2, layer):
    n_blocks = block_e.shape[0]
    d, hid = w1.shape[-2:]
    wmap = lambda j, be, nu: (layer, be[j], 0, 0)
    tiles = MOE_BLOCK * SUBLANES
    return pl.pallas_call(
        _moe_kernel,
        grid_spec=pltpu.PrefetchScalarGridSpec(
            num_scalar_prefetch=2,
            grid=(n_blocks,),
            in_specs=[
                pl.BlockSpec((tiles, LANES), lambda j, be, nu: (jnp.minimum(j, jnp.maximum(nu[0] - 1, 0)), 0)),
                pl.BlockSpec((None, None, d, hid), wmap),
                pl.BlockSpec((None, None, d, hid), wmap),
                pl.BlockSpec((None, None, hid, d), wmap),
            ],
            out_specs=pl.BlockSpec((tiles, LANES), lambda j, be, nu: (j, 0)),
            scratch_shapes=[pltpu.VMEM((d, hid), BF16), pltpu.VMEM((d, hid), BF16), pltpu.VMEM((hid, d), BF16)],
        ),
        out_shape=jax.ShapeDtypeStruct((n_blocks * tiles, LANES), F32),
        compiler_params=_cparams(("arbitrary",)),
        name="moe_experts",
    )(block_e, n_used, xs, w1, w3, w2)


def _comb_kernel(dn_alpha, dest_ref, y_hbm, x1_ref, rw_ref, mod_ref, lng_ref, lnb_ref, o_ref, ybuf, sems):
    bb = pl.program_id(0)
    i = pl.program_id(1)
    tm = x1_ref.shape[0]
    n_tok = pl.num_programs(0) * pl.num_programs(1) * tm
    base = (bb * pl.num_programs(1) + i) * tm

    def gather(start):
        def body(r, carry):
            for kk in range(2):
                cp = _tile_copy(y_hbm, dest_ref[kk * n_tok + base + r], ybuf.at[kk], r * SUBLANES, sems.at[kk])
                if start:
                    cp.start(priority=kk)
                else:
                    cp.wait()
            return carry
        lax.fori_loop(0, tm, body, 0, unroll=8)

    gather(True)
    gather(False)
    rw = rw_ref[...]
    y = rw[:, 0:1] * _rows_from_tiles(ybuf.at[0], tm) + rw[:, 1:2] * _rows_from_tiles(ybuf.at[1], tm)
    gate2 = mod_ref[5:6, :]
    o_ref[...] = _ln(dn_alpha * x1_ref[...] + gate2 * y) * lng_ref[...] + lnb_ref[...]


def _combine(dest, y_slots, x1, rw, modtab, ln_g, ln_b, dn_alpha, seg_of_tile):
    b, n, d = x1.shape
    n_t = n // TM
    row = lambda bb, i, dr: (bb, i, 0)
    const2 = lambda bb, i, dr: (0, 0)
    return pl.pallas_call(
        functools.partial(_comb_kernel, dn_alpha),
        grid_spec=pltpu.PrefetchScalarGridSpec(
            num_scalar_prefetch=1,
            grid=(b, n_t),
            in_specs=[
                pl.BlockSpec(memory_space=pl.ANY),
                pl.BlockSpec((None, TM, d), row),
                pl.BlockSpec((None, TM, LANES), row),
                pl.BlockSpec((None, None, 6, d), lambda bb, i, dr: (bb, seg_of_tile(i), 0, 0)),
                pl.BlockSpec((1, d), const2), pl.BlockSpec((1, d), const2),
            ],
            out_specs=pl.BlockSpec((None, TM, d), row),
            scratch_shapes=[pltpu.VMEM((2, TM * SUBLANES, LANES), F32), pltpu.SemaphoreType.DMA((2,))],
        ),
        out_shape=jax.ShapeDtypeStruct((b, n, d), F32),
        compiler_params=_cparams(("arbitrary", "arbitrary")),
        name="moe_combine",
    )(dest, y_slots, x1, rw, modtab, ln_g, ln_b)


def _dispatch_tables(info, tile_counts, n_blocks):
    b, n_t, _, tm = info.shape
    info = info.astype(jnp.int32)
    expert = jnp.moveaxis(info[:, :, 0:2, :], 2, 0).reshape(2, b * n_t, tm)
    rank = jnp.moveaxis(info[:, :, 2:4, :], 2, 0).reshape(2, b * n_t, tm)
    counts = jnp.sum(tile_counts, axis=0)
    padded = (counts + MOE_BLOCK - 1) // MOE_BLOCK * MOE_BLOCK
    pad_end = jnp.cumsum(padded)
    pad_start = pad_end - padded
    tile_base = pad_start[None, :] + jnp.cumsum(tile_counts, axis=0) - tile_counts
    onehot = expert[..., None] == jnp.arange(MOE_EXPERTS, dtype=jnp.int32)
    dest = jnp.sum(jnp.where(onehot, tile_base[None, :, None, :], 0), axis=-1) + rank
    dest8 = (dest * SUBLANES).reshape(-1).astype(jnp.int32)
    block_start = jnp.arange(n_blocks, dtype=jnp.int32) * MOE_BLOCK
    block_e = jnp.minimum(jnp.sum(pad_end[None, :] <= block_start[:, None], axis=1), MOE_EXPERTS - 1).astype(jnp.int32)
    n_used = (pad_end[-1] // MOE_BLOCK).astype(jnp.int32).reshape(1)
    i32 = lambda v: v.astype(jnp.int32)
    return dest8, i32(counts), i32(pad_start), i32(padded), block_e, n_used


def _pad_heads(w, width):
    lead = w.shape[:-1]
    w = w.reshape(*lead, HEADS, width)
    w = jnp.pad(w, [(0, 0)] * len(lead) + [(0, 0), (0, LANES - width)])
    return w.reshape(*lead, PAD_W)


def _expand_matrix(width):
    e = np.zeros((HEADS * width, PAD_W), np.float32)
    for h in range(HEADS):
        e[h * width + np.arange(width), h * LANES + np.arange(width)] = 1.0
    return jnp.asarray(e, BF16)


def _rope_tables(n_ctx, n_lat):
    n_freq = DA_DH // 4
    inv_freq = ROPE_BASE ** (-jnp.arange(n_freq, dtype=F32) / n_freq)
    pos = jnp.arange(n_lat, dtype=jnp.int32)
    ang_r = (pos // GRID_W).astype(F32)[:, None] * inv_freq
    ang_c = (pos % GRID_W).astype(F32)[:, None] * inv_freq
    ang = jnp.concatenate([ang_r, ang_r, ang_c, ang_c] * 2, axis=-1)
    cos, sin = jnp.cos(ang), jnp.sin(ang)
    first = (jnp.arange(LANES) % 32) < 16
    sin_a = jnp.where(first, -sin, 0.0)
    sin_b = jnp.where(first, 0.0, sin)
    ctx_rows = lambda v: jnp.full((n_ctx, LANES), v, F32)
    return jnp.stack([jnp.concatenate([ctx_rows(1.0), cos]), jnp.concatenate([ctx_rows(0.0), sin_a]),
                      jnp.concatenate([ctx_rows(0.0), sin_b])])


def kernel(x, c, ctx, c_ctx, w_ada, b_ada, w_in, da_lambda, da_norm, ml_conv_w, ml_conv_b, ml_ib, ml_fb,
           ml_norm, gl_wa, gl_ba, gl_norm, w_out, ln_mix_g, ln_mix_b, ln_ffn_g, ln_ffn_b,
           moe_wg, moe_we, moe_w1, moe_w3, moe_w2):
    b, s, d = x.shape
    n_ctx = ctx.shape[1]
    n = n_ctx + s
    depth = w_in.shape[0]
    assert n_ctx % TM == 0 and s % (TM * ATTN_NQ) == 0 and d % LANES == 0 and b % SCAN_NB == 0
    dn_alpha = (2 * depth) ** 0.25

    rows = -(-(b + 1) // SUBLANES) * SUBLANES
    cc = jnp.concatenate([c, c_ctx[None, :], jnp.zeros((rows - b - 1, d), F32)], axis=0)
    mod = _ada(cc, w_ada, b_ada).reshape(depth, rows, 6, d)

    xc = jnp.concatenate([ctx, x], axis=1)
    rope = _rope_tables(n_ctx, s)
    e64, e32 = _expand_matrix(64), _expand_matrix(32)

    for l in range(depth):
        last = l == depth - 1
        tile0 = n_ctx // TM if last else 0
        lam_init = 0.8 - 0.6 * math.exp(-0.3 * l)
        modtab = jnp.stack([jnp.broadcast_to(mod[l, b], (b, 6, d)), mod[l, :b]], axis=1)

        wi = w_in[l]
        o = np.cumsum([0, 512, 512, 512, 512, 256, 256, 8, 8, 128, 128, 256, 256, 32])
        misc_w = jnp.concatenate([wi[:, o[6]:o[8]], wi[:, o[12]:o[13]],
                                  jnp.zeros((d, LANES - 16 - 2 * GL_RANK), F32)], axis=1)
        w_in_p = jnp.concatenate([wi[:, :o[6]], wi[:, o[8]:o[12]], misc_w], axis=1).astype(BF16)

        (qa, ka, va, mq, mk, mv, mo, gq, gk, gv, gr, misc) = _in_proj(
            xc, modtab, w_in_p, rope, ml_conv_w[l], ml_conv_b[l].reshape(1, -1), e64, e32, n_ctx)

        n_ctx_tiles = n_ctx // TM
        attend = functools.partial(_attention, qa, ka, va, da_lambda[l], da_norm[l].reshape(1, -1), lam_init)
        da_ctx = None if last else attend(0, n_ctx_tiles, n_ctx, 1)
        da_lat = attend(n_ctx_tiles, s // TM, n, ATTN_NQ)

        gate_bias = jnp.concatenate([ml_ib[l].reshape(-1), ml_fb[l].reshape(-1),
                                     jnp.zeros((LANES - 16,), F32)]).reshape(1, LANES)
        mhf, mhb = _mlstm(mq, mk, mv, misc, gate_bias, n_ctx)

        wa_p = jnp.zeros((2, LANES, PAD_W), F32)
        for dd in range(2):
            wa_p = wa_p.at[dd, M_GA + dd * GL_RANK:M_GA + (dd + 1) * GL_RANK].set(_pad_heads(gl_wa[l, dd], GL_DK))
        ba_p = _pad_heads(gl_ba[l], GL_DK).reshape(2, 1, PAD_W)
        ghf, ghb = _gla(gq, gk, gv, misc, wa_p.astype(BF16), ba_p, n_ctx)

        wo = w_out[l]
        wo_p = jnp.concatenate([
            wo[:512],
            jnp.pad(wo[512:768].reshape(HEADS, 64, d), ((0, 0), (0, 64), (0, 0))).reshape(PAD_W, d),
            jnp.pad(wo[768:1024].reshape(HEADS, 64, d), ((0, 0), (0, 64), (0, 0))).reshape(PAD_W, d),
        ], axis=0).astype(BF16)
        w_route = jnp.concatenate([moe_wg[l], moe_we[l],
                                   jnp.zeros((d, LANES - MOE_GROUPS - MOE_EXPERTS), F32)], axis=1)
        w_route_top = _bf16_part(w_route)
        w_route = jnp.stack([w_route_top.astype(BF16), (w_route - w_route_top).astype(BF16)])
        ml_g = _pad_heads(jnp.tile(ml_norm[l], HEADS), ML_DH).reshape(1, PAD_W)
        gl_g = _pad_heads(jnp.tile(gl_norm[l], HEADS), GL_DV).reshape(1, PAD_W)
        x1, f3, info, cnt, rw = _out_proj(xc, da_ctx, da_lat, mhf, mhb, mo, ghf, ghb, gr, modtab,
                                          ln_mix_g[l].reshape(1, d), ln_mix_b[l].reshape(1, d), ml_g, gl_g,
                                          wo_p, w_route, dn_alpha, n_ctx, tile0)

        n_rows = x1.shape[1]
        n_t = n_rows // TM
        a_all = b * n_rows * 2
        n_blocks = -(-(a_all + MOE_EXPERTS * (MOE_BLOCK - 1)) // MOE_BLOCK)
        tile_counts = cnt.reshape(b * n_t, SUBLANES, LANES)[:, 0, MOE_GROUPS:MOE_GROUPS + MOE_EXPERTS]
        dest8, counts, pad_start, padded, block_e, n_used = _dispatch_tables(
            info.reshape(b, n_t, SUBLANES, TM), tile_counts.astype(jnp.int32), n_blocks)
        xs = _dispatch(dest8, counts, pad_start, padded, f3, n_blocks * MOE_BLOCK)
        y_slots = _moe(xs, block_e, n_used, moe_w1, moe_w3, moe_w2, l)
        n_ctx_tiles_here = 0 if last else n_ctx // TM
        seg = lambda i: jnp.where(i >= n_ctx_tiles_here, 1, 0)
        xc = _combine(dest8, y_slots, x1, rw, modtab, ln_ffn_g[l].reshape(1, d), ln_ffn_b[l].reshape(1, d),
                      dn_alpha, seg)
    return xc
```

```python
import functools
import math

import numpy as np
import jax
import jax.numpy as jnp
from jax import lax
from jax.experimental import pallas as pl
from jax.experimental.pallas import tpu as pltpu

F32 = jnp.float32
BF16 = jnp.bfloat16
HI = lax.Precision.HIGHEST

GRID_W = 64
HEADS = 4
DA_DH = 64
ML_DH = 64
GL_DK = 32
GL_DV = 64
GL_RANK = 16
GL_NORMALIZER = 16.0
CHUNK = 64
SUB = 16
ROPE_BASE = 10000.0
MOE_GROUPS = 4
MOE_PER_GROUP = 8
MOE_EXPERTS = 32
MOE_HIDDEN = 512
LN_EPS = 1e-6

LANES = 128
SUBLANES = 8
VMEM_LIMIT = 48 * 1024 * 1024

TM = 256
MOE_BLOCK = 256
SCAN_NB = 8
ATTN_NQ = 4
PAD_W = HEADS * LANES

C_QA, C_KA, C_VA, C_MQK, C_MV, C_MO = 0, 512, 1024, 1536, 2048, 2304
C_GQ, C_GK, C_GV, C_GR, C_MISC, IN_WP = 2560, 2688, 2816, 3072, 3328, 3456
M_IG, M_FG, M_GA = 0, 8, 16


def _cparams(sem):
    return pltpu.CompilerParams(dimension_semantics=sem, vmem_limit_bytes=VMEM_LIMIT)


def _sigmoid(x):
    return 1.0 / (1.0 + jnp.exp(-x))


def _silu(x):
    return x * _sigmoid(x)


def _log_sigmoid(x):
    return jnp.minimum(x, 0.0) - jnp.log(1.0 + jnp.exp(-jnp.abs(x)))


def _ln(x):
    mu = jnp.mean(x, axis=-1, keepdims=True)
    xc = x - mu
    var = jnp.mean(xc * xc, axis=-1, keepdims=True)
    return xc * lax.rsqrt(var + LN_EPS)


def _bf16_part(x):
    bits = lax.bitcast_convert_type(x, jnp.int32) & jnp.int32(-65536)
    return lax.bitcast_convert_type(bits, F32)


def _dot(a, b):
    return jnp.dot(a, b, preferred_element_type=F32)


def _dot_hi(a, b):
    return jnp.dot(a, b, precision=HI, preferred_element_type=F32)


def _cumsum_dot(a, b, split_lhs=False):
    x = a if split_lhs else b
    x0 = _bf16_part(x)
    r1 = x - x0
    x1 = _bf16_part(r1)
    x2 = r1 - x1
    parts = [p.astype(BF16) for p in (x0, x1, x2)]
    if split_lhs:
        return _dot(parts[0], b) + (_dot(parts[1], b) + _dot(parts[2], b))
    return _dot(a, parts[0]) + (_dot(a, parts[1]) + _dot(a, parts[2]))


def _dot_nt(a, b):
    return lax.dot_general(a, b, (((1,), (1,)), ((), ())), preferred_element_type=F32)


def _dot_tn(a, b):
    return lax.dot_general(a, b, (((0,), (0,)), ((), ())), preferred_element_type=F32)


def _ada_kernel(c_ref, w_ref, b_ref, o_ref):
    o_ref[...] = _dot_hi(_silu(c_ref[...]), w_ref[...]) + b_ref[...]


def _ada(cc, w_ada, b_ada):
    depth, d, n6 = w_ada.shape
    tn = 1536
    rows = cc.shape[0]
    return pl.pallas_call(
        _ada_kernel,
        grid=(depth, n6 // tn),
        in_specs=[
            pl.BlockSpec((rows, d), lambda l, j: (0, 0)),
            pl.BlockSpec((None, d, tn), lambda l, j: (l, 0, j)),
            pl.BlockSpec((None, 1, tn), lambda l, j: (l, 0, j)),
        ],
        out_specs=pl.BlockSpec((None, rows, tn), lambda l, j: (l, 0, j)),
        out_shape=jax.ShapeDtypeStruct((depth, rows, n6), F32),
        compiler_params=_cparams(("arbitrary", "arbitrary")),
        name="ada_mod",
    )(cc, w_ada, b_ada.reshape(depth, 1, n6))


def _in_kernel(n_ctx_tiles, x_ref, xp_ref, xn_ref, mod_ref, w_ref, rope_ref, cw_ref, cb_ref,
               e64_ref, e32_ref,
               qa_ref, ka_ref, va_ref, mq_ref, mk_ref, mv_ref, mo_ref,
               gq_ref, gk_ref, gv_ref, gr_ref, misc_ref, pext_ref):
    i = pl.program_id(1)
    n_tiles = pl.num_programs(1)
    tm = x_ref.shape[0]
    shift, scale = mod_ref[0:1, :], mod_ref[1:2, :]

    def prenorm(x):
        return (_ln(x) * (1.0 + scale) + shift).astype(BF16)

    h = prenorm(x_ref[...])
    halo = jnp.concatenate([prenorm(xp_ref[...]), prenorm(xn_ref[...])], axis=0)

    parts = {lo: _dot(h, w_ref[:, lo:hi]) for lo, hi in (
        (C_QA, C_KA), (C_KA, C_VA), (C_VA, C_MQK), (C_MQK, C_MV), (C_MV, C_MO), (C_MO, C_GQ),
        (C_GQ, C_GK), (C_GK, C_GV), (C_GV, C_GR), (C_GR, C_MISC), (C_MISC, IN_WP))}
    p_halo = _dot(halo, w_ref[:, C_MQK:C_MV])

    def proj(a, lo, hi):
        return parts[lo]

    cos, sin_a, sin_b = rope_ref[0], rope_ref[1], rope_ref[2]
    for lo, out_ref, mult in ((C_QA, qa_ref, DA_DH ** -0.5 * math.log2(math.e)), (C_KA, ka_ref, 1.0)):
        acc = proj(h, lo, lo + PAD_W)
        for hh in range(HEADS):
            xh = acc[:, hh * LANES:(hh + 1) * LANES]
            r = xh * cos + pltpu.roll(xh, LANES - 16, 1) * sin_a + pltpu.roll(xh, 16, 1) * sin_b
            out_ref[:, hh * LANES:(hh + 1) * LANES] = (r * mult).astype(BF16)
    va_ref[...] = proj(h, C_VA, C_VA + PAD_W).astype(BF16)

    seg_first = jnp.logical_or(i == 0, i == n_ctx_tiles)
    seg_last = jnp.logical_or(i == n_ctx_tiles - 1, i == n_tiles - 1)
    keep_prev = jnp.where(seg_first, 0.0, 1.0)
    keep_next = jnp.where(seg_last, 0.0, 1.0)
    p = proj(h, C_MQK, C_MQK + 512)
    pext_ref[0:SUBLANES, :] = p_halo[0:SUBLANES] * keep_prev
    pext_ref[SUBLANES:SUBLANES + tm, :] = p
    pext_ref[SUBLANES + tm:2 * SUBLANES + tm, :] = p_halo[SUBLANES:2 * SUBLANES] * keep_next
    y = (cw_ref[0:1, :] * pext_ref[SUBLANES - 1:SUBLANES - 1 + tm, :] + cw_ref[1:2, :] * p
         + cw_ref[2:3, :] * pext_ref[SUBLANES + 1:SUBLANES + 1 + tm, :] + cb_ref[...])
    y = _silu(y)
    e64 = e64_ref[...]
    e32 = e32_ref[...]
    mq_ref[...] = _dot((y[:, :256] * ML_DH ** -0.5).astype(BF16), e64).astype(BF16)
    mk_ref[...] = _dot(y[:, 256:].astype(BF16), e64).astype(BF16)
    lane = lax.broadcasted_iota(jnp.int32, (tm, PAD_W), 1)
    ones_col = jnp.where(lane % LANES >= ML_DH, 1.0, 0.0)
    mv_ref[...] = (_dot(proj(h, C_MV, C_MV + 256).astype(BF16), e64) + ones_col).astype(BF16)
    mo_ref[...] = _dot(proj(h, C_MO, C_MO + 256).astype(BF16), e64).astype(BF16)

    gq_ref[...] = _dot((proj(h, C_GQ, C_GQ + 128) * GL_DK ** -0.5).astype(BF16), e32).astype(BF16)
    gk_ref[...] = _dot(proj(h, C_GK, C_GK + 128).astype(BF16), e32).astype(BF16)
    gv_ref[...] = _dot(proj(h, C_GV, C_GV + 256).astype(BF16), e64).astype(BF16)
    gr_ref[...] = _dot(proj(h, C_GR, C_GR + 256).astype(BF16), e64).astype(BF16)
    misc_ref[...] = proj(h, C_MISC, C_MISC + LANES)


def _in_proj(xc, modtab, w_in_p, rope, conv_w, conv_b, e64, e32, n_ctx):
    b, n, d = xc.shape
    n_tiles = n // TM
    n_ctx_tiles = n_ctx // TM
    hb = TM // SUBLANES
    n_hb = n // SUBLANES
    row = lambda bb, i: (bb, i, 0)
    const2 = lambda bb, i: (0, 0)
    outs = [jax.ShapeDtypeStruct((b, n, PAD_W), BF16)] * 11 + [jax.ShapeDtypeStruct((b, n, LANES), F32)]
    out_specs = [pl.BlockSpec((None, TM, PAD_W), row)] * 11 + [pl.BlockSpec((None, TM, LANES), row)]
    return pl.pallas_call(
        functools.partial(_in_kernel, n_ctx_tiles),
        grid=(b, n_tiles),
        in_specs=[
            pl.BlockSpec((None, TM, d), row),
            pl.BlockSpec((None, SUBLANES, d), lambda bb, i: (bb, jnp.maximum(i * hb - 1, 0), 0)),
            pl.BlockSpec((None, SUBLANES, d), lambda bb, i: (bb, jnp.minimum((i + 1) * hb, n_hb - 1), 0)),
            pl.BlockSpec((None, None, 6, d), lambda bb, i: (bb, jnp.where(i >= n_ctx_tiles, 1, 0), 0, 0)),
            pl.BlockSpec((d, IN_WP), const2),
            pl.BlockSpec((3, TM, LANES), lambda bb, i: (0, i, 0)),
            pl.BlockSpec((3, 512), const2),
            pl.BlockSpec((1, 512), const2),
            pl.BlockSpec((256, PAD_W), const2),
            pl.BlockSpec((128, PAD_W), const2),
        ],
        out_specs=out_specs,
        out_shape=outs,
        scratch_shapes=[pltpu.VMEM((TM + 2 * SUBLANES, 512), F32)],
        compiler_params=_cparams(("parallel", "arbitrary")),
        name="in_proj",
    )(xc, xc, xc, modtab, w_in_p, rope, conv_w, conv_b, e64, e32)


def _attn_kernel(lam_init, n_q, *refs):
    q_refs = refs[:n_q]
    k_ref, v_ref, lamp_ref, g_ref, o_ref = refs[n_q:]
    lp = lamp_ref[...]
    lam = (jnp.exp(jnp.sum(lp[0:1] * lp[1:2], axis=1, keepdims=True))
           - jnp.exp(jnp.sum(lp[2:3] * lp[3:4], axis=1, keepdims=True)) + lam_init)
    k = k_ref[...]
    v = jnp.concatenate([v_ref[...], jnp.ones(v_ref.shape, BF16)], axis=1)
    tq = q_refs[0].shape[0]
    lane = lax.broadcasted_iota(jnp.int32, (tq, LANES), 1)
    scores = []
    for q_ref in q_refs:
        q = q_ref[...]
        zero = jnp.zeros_like(q)
        scores.append((_dot_nt(jnp.where(lane < DA_DH, q, zero), k), _dot_nt(jnp.where(lane >= DA_DH, q, zero), k)))
    for t, (s1, s2) in enumerate(scores):
        p1 = _dot(jnp.exp2(s1 - jnp.max(s1, axis=1, keepdims=True)).astype(BF16), v)
        p2 = _dot(jnp.exp2(s2 - jnp.max(s2, axis=1, keepdims=True)).astype(BF16), v)
        o = p1[:, 0:LANES] / p1[:, LANES:2 * LANES] - p2[:, 0:LANES] * (lam / p2[:, LANES:2 * LANES])
        y = o * lax.rsqrt(jnp.mean(o * o, axis=1, keepdims=True) + LN_EPS) * g_ref[...]
        o_ref[t * tq:(t + 1) * tq, :] = (y * (1.0 - lam_init)).astype(BF16)


def _attention(qa, ka, va, lam_p, norm_g, lam_init, tile0, n_tiles, n_keys, n_q):
    b = qa.shape[0]
    tq = TM
    q_spec = lambda t: pl.BlockSpec((None, tq, LANES), lambda bb, h, i: (bb, tile0 + i * n_q + t, h))
    kv_spec = pl.BlockSpec((None, n_keys, LANES), lambda bb, h, i: (bb, 0, h))
    return pl.pallas_call(
        functools.partial(_attn_kernel, lam_init, n_q),
        grid=(b, HEADS, n_tiles // n_q),
        in_specs=[q_spec(t) for t in range(n_q)] + [
            kv_spec, kv_spec,
            pl.BlockSpec((4, DA_DH), lambda bb, h, i: (0, 0)),
            pl.BlockSpec((1, LANES), lambda bb, h, i: (0, 0)),
        ],
        out_specs=pl.BlockSpec((None, n_q * tq, LANES), lambda bb, h, i: (bb, i, h)),
        out_shape=jax.ShapeDtypeStruct((b, n_tiles * tq, PAD_W), BF16),
        compiler_params=_cparams(("parallel", "parallel", "arbitrary")),
        name="diff_attn",
    )(*([qa] * n_q), ka, va, lam_p, norm_g)


def _chunk_maps(n_ctx_chunks, n_chunks):
    def fwd(bb, t):
        return (bb, t, 0)

    def bwd(bb, t):
        return (bb, jnp.where(t < n_ctx_chunks, n_ctx_chunks - 1 - t, n_chunks - 1 + n_ctx_chunks - t), 0)

    return fwd, bwd


def _tri(direction):
    l = lax.broadcasted_iota(jnp.int32, (CHUNK, CHUNK), 0)
    s = lax.broadcasted_iota(jnp.int32, (CHUNK, CHUNK), 1)
    return (s <= l) if direction == 0 else (s >= l)


def _mlstm_kernel(qf_ref, kf_ref, vf_ref, gf_ref, qb_ref, kb_ref, vb_ref, gb_ref, bias_ref, sel_ref,
                  hf_ref, hb_ref, c_ref, m_ref):
    t = pl.program_id(1)

    @pl.when(t == 0)
    def _():
        c_ref[...] = jnp.zeros_like(c_ref)
        m_ref[...] = jnp.zeros_like(m_ref)

    lane = lax.broadcasted_iota(jnp.int32, (CHUNK, LANES), 1)
    is_fg = jnp.logical_and(lane >= M_FG, lane < M_GA)
    dirs = ((qf_ref, kf_ref, vf_ref, gf_ref, hf_ref), (qb_ref, kb_ref, vb_ref, gb_ref, hb_ref))
    n_b = qf_ref.shape[0]
    masks = [_tri(d) for d in range(2)]
    units = [(d, nb, h) for d in range(2) for nb in range(n_b) for h in range(HEADS)]

    pre = {}
    for d in range(2):
        tri = jnp.where(masks[d], 1.0, 0.0).astype(BF16)
        for nb in range(n_b):
            g = dirs[d][3][nb] + bias_ref[...]
            g = jnp.where(is_fg, _log_sigmoid(g), g)
            cum = _cumsum_dot(tri, g)
            pre[d, nb] = (_cumsum_dot(cum, sel_ref[d, 0], split_lhs=True),
                          _cumsum_dot(g, sel_ref[d, 1], split_lhs=True), g.T, cum.T)

    qk, qc = {}, {}
    for u in units:
        d, nb, h = u
        hs = slice(h * LANES, (h + 1) * LANES)
        q = dirs[d][0][nb, :, hs]
        qk[u] = _dot_nt(q, dirs[d][1][nb, :, hs])
        qc[u] = _dot(q, c_ref[nb, d, h].astype(BF16))

    gate = {}
    for u in units:
        d, nb, h = u
        hs = slice(h * LANES, (h + 1) * LANES)
        b_all, i_all, g_t, cum_t = pre[d, nb]
        ci, cf = M_IG + HEADS * d + h, M_FG + HEADS * d + h
        last = CHUNK - 1 if d == 0 else 0
        b_rep = b_all[:, hs]
        i_rep = i_all[:, hs]
        row = g_t[ci:ci + 1, :] - cum_t[cf:cf + 1, :]
        m_prev = m_ref[nb, d, h][0:1, :]
        log_d = jnp.where(masks[d], b_rep[:, 0:CHUNK] + row, -jnp.inf)
        log_a = b_rep + m_prev
        m_t = jnp.maximum(log_a, jnp.max(log_d, axis=1, keepdims=True))
        m_new = m_t[last:last + 1, :]
        b_last = b_rep[last:last + 1, :]
        gate[u] = (jnp.exp(log_d - m_t[:, 0:CHUNK]), jnp.exp(log_a - m_t), jnp.exp(-m_t),
                   jnp.exp(b_last - b_rep + i_rep - m_new), jnp.exp(b_last + m_prev - m_new), m_new)

    sv, kv = {}, {}
    for u in units:
        d, nb, h = u
        hs = slice(h * LANES, (h + 1) * LANES)
        v = dirs[d][2][nb, :, hs]
        dmat, _, _, w, _, _ = gate[u]
        sv[u] = _dot((qk[u] * dmat).astype(BF16), v)
        kv[u] = _dot_tn((dirs[d][1][nb, :, hs].astype(F32) * w).astype(BF16), v)

    for u in units:
        d, nb, h = u
        hs = slice(h * LANES, (h + 1) * LANES)
        _, a, floor, _, decay, m_new = gate[u]
        num = a * qc[u] + sv[u]
        qn = pltpu.roll(num, ML_DH, 1)
        den = jnp.maximum(jnp.abs(qn), floor)
        dirs[d][4][nb, :, hs] = jnp.where(lane < ML_DH, num / den, 0.0)
        c_ref[nb, d, h] = decay * c_ref[nb, d, h] + kv[u]
        m_ref[nb, d, h] = jnp.broadcast_to(m_new, (SUBLANES, LANES))


def _gate_select_matrices():
    sel = np.zeros((2, 2, LANES, PAD_W), np.float32)
    for d in range(2):
        for h in range(HEADS):
            sel[d, 0, M_FG + HEADS * d + h, h * LANES:(h + 1) * LANES] = 1.0
            sel[d, 1, M_IG + HEADS * d + h, h * LANES:(h + 1) * LANES] = 1.0
    return jnp.asarray(sel, BF16)


def _mlstm(mq, mk, mv, misc, gate_bias, n_ctx):
    b, n, _ = mq.shape
    n_chunks = n // CHUNK
    fwd, bwd = _chunk_maps(n_ctx // CHUNK, n_chunks)
    wide = lambda m: pl.BlockSpec((SCAN_NB, CHUNK, PAD_W), m)
    thin = lambda m: pl.BlockSpec((SCAN_NB, CHUNK, LANES), m)
    return pl.pallas_call(
        _mlstm_kernel,
        grid=(b // SCAN_NB, n_chunks),
        in_specs=[wide(fwd), wide(fwd), wide(fwd), thin(fwd), wide(bwd), wide(bwd), wide(bwd), thin(bwd),
                  pl.BlockSpec((1, LANES), lambda bb, t: (0, 0)),
                  pl.BlockSpec((2, 2, LANES, PAD_W), lambda bb, t: (0, 0, 0, 0))],
        out_specs=[wide(fwd), wide(bwd)],
        out_shape=[jax.ShapeDtypeStruct((b, n, PAD_W), F32)] * 2,
        scratch_shapes=[pltpu.VMEM((SCAN_NB, 2, HEADS, LANES, LANES), F32),
                        pltpu.VMEM((SCAN_NB, 2, HEADS, SUBLANES, LANES), F32)],
        compiler_params=_cparams(("parallel", "arbitrary")),
        name="mlstm_scan",
    )(mq, mk, mv, misc, mq, mk, mv, misc, gate_bias, _gate_select_matrices())


def _gla_kernel(qf_ref, kf_ref, vf_ref, gf_ref, qb_ref, kb_ref, vb_ref, gb_ref, wa_ref, ba_ref,
                hf_ref, hb_ref, st_ref):
    t = pl.program_id(1)

    @pl.when(t == 0)
    def _():
        st_ref[...] = jnp.zeros_like(st_ref)

    n_sub = CHUNK // SUB
    row_id = lax.broadcasted_iota(jnp.int32, (CHUNK, LANES), 0)
    row_blk = row_id // SUB
    col_id = lax.broadcasted_iota(jnp.int32, (CHUNK, LANES), 1)
    rows_w = lax.broadcasted_iota(jnp.int32, (CHUNK, PAD_W), 0)
    zeros = jnp.zeros((CHUNK, LANES), BF16)
    dirs = ((qf_ref, kf_ref, vf_ref, gf_ref, hf_ref), (qb_ref, kb_ref, vb_ref, gb_ref, hb_ref))
    n_b = qf_ref.shape[0]
    pairs = [(d, nb) for d in range(2) for nb in range(n_b)]
    units = [(d, nb, h) for d, nb in pairs for h in range(HEADS)]
    tris = [jnp.where(_tri(d), 1.0, 0.0).astype(BF16) for d in range(2)]
    causal = [col_id <= row_id, jnp.logical_and(col_id >= row_id, col_id < CHUNK)]

    las = {p: _log_sigmoid(_dot(dirs[p[0]][3][p[1]].astype(BF16), wa_ref[p[0]]) + ba_ref[p[0]])
           * (1.0 / GL_NORMALIZER) for p in pairs}
    cums = {p: _cumsum_dot(tris[p[0]], las[p]) for p in pairs}

    pre = {}
    for p in pairs:
        d, nb = p
        cum = cums[p]
        excl = cum - las[p]
        last = CHUNK - 1 if d == 0 else 0
        betas = [excl[r:r + 1, :] for r in (range(0, CHUNK, SUB) if d == 0 else range(SUB - 1, CHUNK, SUB))]
        beta_rows = jnp.concatenate([jnp.broadcast_to(bt, (SUB, PAD_W)) for bt in betas], axis=0)
        q = dirs[d][0][nb].astype(F32)
        k = dirs[d][1][nb].astype(F32)
        q_loc = (q * jnp.exp(cum - beta_rows)).astype(BF16)
        q_abs = (q * jnp.exp(cum)).astype(BF16)
        b_last = cum[last:last + 1, :]
        k_end = (k * jnp.exp(b_last - cum)).astype(BF16)
        k_subs = []
        for I in range(n_sub):
            seen = (rows_w < (I + 1) * SUB) if d == 0 else (rows_w >= I * SUB)
            k_subs.append((k * jnp.exp(jnp.where(seen, betas[I] - cum, -jnp.inf))).astype(BF16))
        pre[p] = (q_loc, q_abs, b_last, k_end, k_subs)

    scores, inter, kv = {}, {}, {}
    for u in units:
        d, nb, h = u
        hs = slice(h * LANES, (h + 1) * LANES)
        q_loc, q_abs, _, k_end, k_subs = pre[d, nb]
        kcat = jnp.concatenate([x for I in range(n_sub) for x in (k_subs[I][:, hs], zeros)], axis=0)
        scores[u] = _dot_nt(q_loc[:, hs], kcat)
        inter[u] = _dot_nt(q_abs[:, hs], st_ref[nb, d, h].astype(BF16))
        kv[u] = _dot_tn(dirs[d][2][nb, :, hs], k_end[:, hs])

    att = {}
    for u in units:
        r = scores[u]
        a = r[:, 0:LANES]
        for I in range(1, n_sub):
            a = jnp.where(row_blk == I, r[:, I * LANES:(I + 1) * LANES], a)
        att[u] = jnp.where(causal[u[0]], a, 0.0)[:, 0:CHUNK].astype(BF16)

    av = {u: _dot(att[u], dirs[u[0]][2][u[1], :, u[2] * LANES:(u[2] + 1) * LANES]) for u in units}
    for u in units:
        d, nb, h = u
        hs = slice(h * LANES, (h + 1) * LANES)
        dirs[d][4][nb, :, hs] = inter[u] + av[u]
        st_ref[nb, d, h] = st_ref[nb, d, h] * jnp.exp(pre[d, nb][2][:, hs]) + kv[u]


def _gla(gq, gk, gv, misc, wa_p, ba_p, n_ctx):
    b, n, _ = gq.shape
    n_chunks = n // CHUNK
    fwd, bwd = _chunk_maps(n_ctx // CHUNK, n_chunks)
    wide = lambda m: pl.BlockSpec((SCAN_NB, CHUNK, PAD_W), m)
    thin = lambda m: pl.BlockSpec((SCAN_NB, CHUNK, LANES), m)
    return pl.pallas_call(
        _gla_kernel,
        grid=(b // SCAN_NB, n_chunks),
        in_specs=[wide(fwd), wide(fwd), wide(fwd), thin(fwd), wide(bwd), wide(bwd), wide(bwd), thin(bwd),
                  pl.BlockSpec((2, LANES, PAD_W), lambda bb, t: (0, 0, 0)),
                  pl.BlockSpec((2, 1, PAD_W), lambda bb, t: (0, 0, 0))],
        out_specs=[wide(fwd), wide(bwd)],
        out_shape=[jax.ShapeDtypeStruct((b, n, PAD_W), F32)] * 2,
        scratch_shapes=[pltpu.VMEM((SCAN_NB, 2, HEADS, LANES, LANES), F32)],
        compiler_params=_cparams(("parallel", "arbitrary")),
        name="gla_scan",
    )(gq, gk, gv, misc, gq, gk, gv, misc, wa_p, ba_p)


def _head_rms_padded(x, width):
    parts = []
    for h in range(HEADS):
        xh = x[:, h * LANES:(h + 1) * LANES]
        ms = jnp.sum(xh * xh, axis=1, keepdims=True) * (1.0 / width)
        parts.append(xh * lax.rsqrt(ms + LN_EPS))
    return jnp.concatenate(parts, axis=1)


def _out_kernel(dn_alpha, n_ctx_tiles, x_ref, dac_ref, dal_ref, mhf_ref, mhb_ref, mo_ref, ghf_ref, ghb_ref, gr_ref,
                mod_ref, lng_ref, lnb_ref, mlg_ref, glg_ref, wo_ref, wr_ref,
                x1_ref, f_ref, info_ref, cnt_ref, rw_ref):
    ml = _head_rms_padded(mhf_ref[...] + mhb_ref[...], ML_DH) * mlg_ref[...] * _sigmoid(mo_ref[...].astype(F32))
    gl = _head_rms_padded(ghf_ref[...] + ghb_ref[...], GL_DV) * glg_ref[...] * _silu(gr_ref[...].astype(F32))
    da = jnp.where(pl.program_id(1) < n_ctx_tiles, dac_ref[...], dal_ref[...])
    y = (_dot(da, wo_ref[0:PAD_W, :]) + _dot(ml.astype(BF16), wo_ref[PAD_W:2 * PAD_W, :])
         + _dot(gl.astype(BF16), wo_ref[2 * PAD_W:3 * PAD_W, :]))
    gate1, shift2, scale2 = mod_ref[2:3, :], mod_ref[3:4, :], mod_ref[4:5, :]
    x1 = _ln(dn_alpha * x_ref[...] + gate1 * y) * lng_ref[...] + lnb_ref[...]
    x1_ref[...] = x1
    f = _ln(x1) * (1.0 + scale2) + shift2
    tm = f.shape[0]
    for s in range(SUBLANES):
        f_ref[pl.ds(s, tm, stride=SUBLANES), :] = f[:, s * LANES:(s + 1) * LANES]

    f_top = _bf16_part(f)
    f_hi, f_lo = f_top.astype(BF16), (f - f_top).astype(BF16)
    logits = _dot(f_hi, wr_ref[0]) + (_dot(f_lo, wr_ref[0]) + _dot(f_hi, wr_ref[1]) + _dot(f_lo, wr_ref[1]))
    lane = lax.broadcasted_iota(jnp.int32, logits.shape, 1)
    neg = -jnp.inf
    big = jnp.int32(LANES)

    def first_argmax(vals, vmax):
        return jnp.min(jnp.where(vals == vmax, lane, big), axis=1, keepdims=True)

    gl_ = jnp.where(lane < MOE_GROUPS, logits, neg)
    gmax = jnp.max(gl_, axis=1, keepdims=True)
    g_idx = first_argmax(gl_, gmax)
    g_gate = 1.0 / jnp.sum(jnp.exp(gl_ - gmax), axis=1, keepdims=True)
    in_group = jnp.logical_and(lane >= MOE_GROUPS, (lane - MOE_GROUPS) // MOE_PER_GROUP == g_idx)
    el = jnp.where(in_group, logits, neg)
    emax = jnp.max(el, axis=1, keepdims=True)
    e0 = first_argmax(el, emax)
    el2 = jnp.where(lane == e0, neg, el)
    emax2 = jnp.max(el2, axis=1, keepdims=True)
    e1 = first_argmax(el2, emax2)
    esum = jnp.sum(jnp.exp(el - emax), axis=1, keepdims=True)
    p0 = 1.0 / esum
    p1 = jnp.exp(emax2 - emax) / esum
    w0 = g_gate * p0 / (p0 + p1)
    w1 = g_gate * p1 / (p0 + p1)
    rw_ref[...] = jnp.where(lane == 0, w0, jnp.where(lane == 1, w1, 0.0))

    oh0 = jnp.where(lane == e0, 1.0, 0.0)
    oh1 = jnp.where(lane == e1, 1.0, 0.0)
    both = (oh0 + oh1).astype(BF16)
    r_id = lax.broadcasted_iota(jnp.int32, (tm, tm), 0)
    c_id = lax.broadcasted_iota(jnp.int32, (tm, tm), 1)
    earlier = jnp.where(c_id < r_id, 1.0, 0.0).astype(BF16)
    before = _dot(earlier, both)
    rank0 = jnp.sum(before * oh0, axis=1, keepdims=True)
    rank1 = jnp.sum(before * oh1, axis=1, keepdims=True)
    cnt_ref[...] = _dot(jnp.ones((SUBLANES, tm), BF16), both)
    info = jnp.where(lane == 0, (e0 - MOE_GROUPS).astype(F32),
                     jnp.where(lane == 1, (e1 - MOE_GROUPS).astype(F32),
                               jnp.where(lane == 2, rank0, jnp.where(lane == 3, rank1, 0.0))))
    info_ref[...] = info.T[0:SUBLANES, :]


def _out_proj(xc, da_ctx, da_lat, mhf, mhb, mo, ghf, ghb, gr, modtab, ln_g, ln_b, ml_g, gl_g, wo_p, w_route,
              dn_alpha, n_ctx, tile0):
    b, n, d = xc.shape
    n_ctx_tiles = n_ctx // TM
    n_t = n // TM - tile0
    out_ctx_tiles = n_ctx_tiles - tile0
    row = lambda bb, i: (bb, i + tile0, 0)
    orow = lambda bb, i: (bb, i, 0)
    const2 = lambda bb, i: (0, 0)
    wide = pl.BlockSpec((None, TM, PAD_W), row)
    n_out = n_t * TM
    if da_ctx is None:
        da_ctx = da_lat
    ctx_rows = lambda bb, i: (bb, jnp.minimum(i, max(out_ctx_tiles - 1, 0)), 0)
    lat_rows = lambda bb, i: (bb, jnp.maximum(i - out_ctx_tiles, 0), 0)
    return pl.pallas_call(
        functools.partial(_out_kernel, dn_alpha, out_ctx_tiles),
        grid=(b, n_t),
        in_specs=[
            pl.BlockSpec((None, TM, d), row),
            pl.BlockSpec((None, TM, PAD_W), ctx_rows), pl.BlockSpec((None, TM, PAD_W), lat_rows),
            wide, wide, wide, wide, wide, wide,
            pl.BlockSpec((None, None, 6, d), lambda bb, i: (bb, jnp.where(i + tile0 >= n_ctx_tiles, 1, 0), 0, 0)),
            pl.BlockSpec((1, d), const2), pl.BlockSpec((1, d), const2),
            pl.BlockSpec((1, PAD_W), const2), pl.BlockSpec((1, PAD_W), const2),
            pl.BlockSpec((3 * PAD_W, d), const2),
            pl.BlockSpec((2, d, LANES), lambda bb, i: (0, 0, 0)),
        ],
        out_specs=[pl.BlockSpec((None, TM, d), orow),
                   pl.BlockSpec((None, TM * SUBLANES, LANES), orow),
                   pl.BlockSpec((None, SUBLANES, TM), orow),
                   pl.BlockSpec((None, SUBLANES, LANES), orow),
                   pl.BlockSpec((None, TM, LANES), orow)],
        out_shape=[jax.ShapeDtypeStruct((b, n_out, d), F32),
                   jax.ShapeDtypeStruct((b, n_out * SUBLANES, LANES), F32),
                   jax.ShapeDtypeStruct((b, n_t * SUBLANES, TM), F32),
                   jax.ShapeDtypeStruct((b, n_t * SUBLANES, LANES), F32),
                   jax.ShapeDtypeStruct((b, n_out, LANES), F32)],
        compiler_params=_cparams(("parallel", "arbitrary")),
        name="out_proj",
    )(xc, da_ctx, da_lat, mhf, mhb, mo, ghf, ghb, gr, modtab, ln_g, ln_b, ml_g, gl_g, wo_p, w_route)


def _tile_copy(src, src_tile, dst, dst_tile, sem):
    aligned = lambda t: t if isinstance(t, int) else pl.multiple_of(t, SUBLANES)
    return pltpu.make_async_copy(src.at[pl.ds(aligned(src_tile), SUBLANES), :],
                                 dst.at[pl.ds(aligned(dst_tile), SUBLANES), :], sem)


def _rows_from_tiles(ref, n_rows):
    return jnp.concatenate([ref[pl.ds(s, n_rows, stride=SUBLANES), :] for s in range(SUBLANES)], axis=1)


def _disp_kernel(dest_ref, cnt_ref, start_ref, padded_ref, f_ref, xs_hbm, zero_ref, sem, zsem, bsem):
    bb = pl.program_id(0)
    i = pl.program_id(1)
    tm = f_ref.shape[0] // SUBLANES
    n_tok = pl.num_programs(0) * pl.num_programs(1) * tm
    base = (bb * pl.num_programs(1) + i) * tm

    def scatter(start):
        def body(r, carry):
            for kk in range(2):
                cp = _tile_copy(f_ref, r * SUBLANES, xs_hbm, dest_ref[kk * n_tok + base + r], sem)
                if start:
                    cp.start(priority=kk)
                else:
                    cp.wait()
            return carry
        lax.fori_loop(0, tm, body, 0, unroll=8)

    scatter(True)
    scatter(False)

    @pl.when(jnp.logical_and(bb == pl.num_programs(0) - 1, i == pl.num_programs(1) - 1))
    def _():
        zero_ref[...] = jnp.zeros_like(zero_ref)
        n_slots = xs_hbm.shape[0] // SUBLANES
        last = MOE_EXPERTS - 1

        blk_tiles = zero_ref.shape[0]

        def fill(start):
            def row(r, c2):
                cp = _tile_copy(zero_ref, 0, xs_hbm, r * SUBLANES, zsem)
                if start:
                    cp.start()
                else:
                    cp.wait()
                return c2

            def block(j, c2):
                cp = pltpu.make_async_copy(zero_ref, xs_hbm.at[pl.ds(pl.multiple_of(j * blk_tiles, blk_tiles),
                                                                     blk_tiles), :], bsem)
                if start:
                    cp.start()
                else:
                    cp.wait()
                return c2

            def per_expert(e, carry):
                return lax.fori_loop(start_ref[e] + cnt_ref[e], start_ref[e] + padded_ref[e], row, carry)
            lax.fori_loop(0, MOE_EXPERTS, per_expert, 0)
            used_blocks = (start_ref[last] + padded_ref[last]) // (blk_tiles // SUBLANES)
            lax.fori_loop(used_blocks, n_slots * SUBLANES // blk_tiles, block, 0)

        fill(True)
        fill(False)


def _dispatch(dest8, counts, pad_start, padded, f3, n_slots):
    b, rows, _ = f3.shape
    n_t = rows // (TM * SUBLANES)
    return pl.pallas_call(
        _disp_kernel,
        grid_spec=pltpu.PrefetchScalarGridSpec(
            num_scalar_prefetch=4,
            grid=(b, n_t),
            in_specs=[pl.BlockSpec((TM * SUBLANES, LANES), lambda bb, i, *_: (bb * n_t + i, 0))],
            out_specs=pl.BlockSpec(memory_space=pl.ANY),
            scratch_shapes=[pltpu.VMEM((MOE_BLOCK * SUBLANES, LANES), F32), pltpu.SemaphoreType.DMA(()),
                            pltpu.SemaphoreType.DMA(()), pltpu.SemaphoreType.DMA(())],
        ),
        out_shape=jax.ShapeDtypeStruct((n_slots * SUBLANES, LANES), F32),
        compiler_params=_cparams(("arbitrary", "arbitrary")),
        name="moe_dispatch",
    )(dest8, counts, pad_start, padded, f3.reshape(b * rows, LANES))


def _moe_kernel(be_ref, nu_ref, xs_ref, w1_ref, w3_ref, w2_ref, y_ref, w1b, w3b, w2b):
    j = pl.program_id(0)
    n_used = nu_ref[0]
    blk = xs_ref.shape[0] // SUBLANES

    @pl.when(jnp.logical_and(j < n_used, jnp.logical_or(j == 0, be_ref[j] != be_ref[jnp.maximum(j - 1, 0)])))
    def _():
        w1b[...] = w1_ref[...].astype(BF16)
        w3b[...] = w3_ref[...].astype(BF16)
        w2b[...] = w2_ref[...].astype(BF16)

    @pl.when(j < n_used)
    def _():
        wide = 2 * LANES
        h1 = h3 = None
        for c in range(w1b.shape[0] // wide):
            xc = jnp.concatenate([xs_ref[pl.ds(2 * c + s, blk, stride=SUBLANES), :] for s in range(2)],
                                 axis=1).astype(BF16)
            d1 = _dot(xc, w1b[c * wide:(c + 1) * wide, :])
            d3 = _dot(xc, w3b[c * wide:(c + 1) * wide, :])
            h1, h3 = (d1, d3) if c == 0 else (h1 + d1, h3 + d3)
        hidden = (_silu(h1) * h3).astype(BF16)
        for c in range(w2b.shape[1] // wide):
            yc = _dot(hidden, w2b[:, c * wide:(c + 1) * wide])
            for s in range(2):
                y_ref[pl.ds(2 * c + s, blk, stride=SUBLANES), :] = yc[:, s * LANES:(s + 1) * LANES]

    @pl.when(j >= n_used)
    def _():
        y_ref[...] = jnp.zeros_like(y_ref)


def _moe(xs, block_e, n_used, w1, w3, w2, layer):
    n_blocks = block_e.shape[0]
    d, hid = w1.shape[-2:]
    wmap = lambda j, be, nu: (layer, be[j], 0, 0)
    tiles = MOE_BLOCK * SUBLANES
    return pl.pallas_call(
        _moe_kernel,
        grid_spec=pltpu.PrefetchScalarGridSpec(
            num_scalar_prefetch=2,
            grid=(n_blocks,),
            in_specs=[
                pl.BlockSpec((tiles, LANES), lambda j, be, nu: (jnp.minimum(j, jnp.maximum(nu[0] - 1, 0)), 0)),
                pl.BlockSpec((None, None, d, hid), wmap),
                pl.BlockSpec((None, None, d, hid), wmap),
                pl.BlockSpec((None, None, hid, d), wmap),
            ],
            out_specs=pl.BlockSpec((tiles, LANES), lambda j, be, nu: (j, 0)),
            scratch_shapes=[pltpu.VMEM((d, hid), BF16), pltpu.VMEM((d, hid), BF16), pltpu.VMEM((hid, d), BF16)],
        ),
        out_shape=jax.ShapeDtypeStruct((n_blocks * tiles, LANES), F32),
        compiler_params=_cparams(("arbitrary",)),
        name="moe_experts",
    )(block_e, n_used, xs, w1, w3, w2)


def _comb_kernel(dn_alpha, dest_ref, y_hbm, x1_ref, rw_ref, mod_ref, lng_ref, lnb_ref, o_ref, ybuf, sems):
    bb = pl.program_id(0)
    i = pl.program_id(1)
    tm = x1_ref.shape[0]
    n_tok = pl.num_programs(0) * pl.num_programs(1) * tm
    base = (bb * pl.num_programs(1) + i) * tm

    def gather(start):
        def body(r, carry):
            for kk in range(2):
                cp = _tile_copy(y_hbm, dest_ref[kk * n_tok + base + r], ybuf.at[kk], r * SUBLANES, sems.at[kk])
                if start:
                    cp.start(priority=kk)
                else:
                    cp.wait()
            return carry
        lax.fori_loop(0, tm, body, 0, unroll=8)

    gather(True)
    gather(False)
    rw = rw_ref[...]
    y = rw[:, 0:1] * _rows_from_tiles(ybuf.at[0], tm) + rw[:, 1:2] * _rows_from_tiles(ybuf.at[1], tm)
    gate2 = mod_ref[5:6, :]
    o_ref[...] = _ln(dn_alpha * x1_ref[...] + gate2 * y) * lng_ref[...] + lnb_ref[...]


def _combine(dest, y_slots, x1, rw, modtab, ln_g, ln_b, dn_alpha, seg_of_tile):
    b, n, d = x1.shape
    n_t = n // TM
    row = lambda bb, i, dr: (bb, i, 0)
    const2 = lambda bb, i, dr: (0, 0)
    return pl.pallas_call(
        functools.partial(_comb_kernel, dn_alpha),
        grid_spec=pltpu.PrefetchScalarGridSpec(
            num_scalar_prefetch=1,
            grid=(b, n_t),
            in_specs=[
                pl.BlockSpec(memory_space=pl.ANY),
                pl.BlockSpec((None, TM, d), row),
                pl.BlockSpec((None, TM, LANES), row),
                pl.BlockSpec((None, None, 6, d), lambda bb, i, dr: (bb, seg_of_tile(i), 0, 0)),
                pl.BlockSpec((1, d), const2), pl.BlockSpec((1, d), const2),
            ],
            out_specs=pl.BlockSpec((None, TM, d), row),
            scratch_shapes=[pltpu.VMEM((2, TM * SUBLANES, LANES), F32), pltpu.SemaphoreType.DMA((2,))],
        ),
        out_shape=jax.ShapeDtypeStruct((b, n, d), F32),
        compiler_params=_cparams(("arbitrary", "arbitrary")),
        name="moe_combine",
    )(dest, y_slots, x1, rw, modtab, ln_g, ln_b)


def _dispatch_tables(info, tile_counts, n_blocks):
    b, n_t, _, tm = info.shape
    info = info.astype(jnp.int32)
    expert = jnp.moveaxis(info[:, :, 0:2, :], 2, 0).reshape(2, b * n_t, tm)
    rank = jnp.moveaxis(info[:, :, 2:4, :], 2, 0).reshape(2, b * n_t, tm)
    counts = jnp.sum(tile_counts, axis=0)
    padded = (counts + MOE_BLOCK - 1) // MOE_BLOCK * MOE_BLOCK
    pad_end = jnp.cumsum(padded)
    pad_start = pad_end - padded
    tile_base = pad_start[None, :] + jnp.cumsum(tile_counts, axis=0) - tile_counts
    onehot = expert[..., None] == jnp.arange(MOE_EXPERTS, dtype=jnp.int32)
    dest = jnp.sum(jnp.where(onehot, tile_base[None, :, None, :], 0), axis=-1) + rank
    dest8 = (dest * SUBLANES).reshape(-1).astype(jnp.int32)
    block_start = jnp.arange(n_blocks, dtype=jnp.int32) * MOE_BLOCK
    block_e = jnp.minimum(jnp.sum(pad_end[None, :] <= block_start[:, None], axis=1), MOE_EXPERTS - 1).astype(jnp.int32)
    n_used = (pad_end[-1] // MOE_BLOCK).astype(jnp.int32).reshape(1)
    i32 = lambda v: v.astype(jnp.int32)
    return dest8, i32(counts), i32(pad_start), i32(padded), block_e, n_used


def _pad_heads(w, width):
    lead = w.shape[:-1]
    w = w.reshape(*lead, HEADS, width)
    w = jnp.pad(w, [(0, 0)] * len(lead) + [(0, 0), (0, LANES - width)])
    return w.reshape(*lead, PAD_W)


def _expand_matrix(width):
    e = np.zeros((HEADS * width, PAD_W), np.float32)
    for h in range(HEADS):
        e[h * width + np.arange(width), h * LANES + np.arange(width)] = 1.0
    return jnp.asarray(e, BF16)


def _rope_tables(n_ctx, n_lat):
    n_freq = DA_DH // 4
    inv_freq = ROPE_BASE ** (-jnp.arange(n_freq, dtype=F32) / n_freq)
    pos = jnp.arange(n_lat, dtype=jnp.int32)
    ang_r = (pos // GRID_W).astype(F32)[:, None] * inv_freq
    ang_c = (pos % GRID_W).astype(F32)[:, None] * inv_freq
    ang = jnp.concatenate([ang_r, ang_r, ang_c, ang_c] * 2, axis=-1)
    cos, sin = jnp.cos(ang), jnp.sin(ang)
    first = (jnp.arange(LANES) % 32) < 16
    sin_a = jnp.where(first, -sin, 0.0)
    sin_b = jnp.where(first, 0.0, sin)
    ctx_rows = lambda v: jnp.full((n_ctx, LANES), v, F32)
    return jnp.stack([jnp.concatenate([ctx_rows(1.0), cos]), jnp.concatenate([ctx_rows(0.0), sin_a]),
                      jnp.concatenate([ctx_rows(0.0), sin_b])])


def kernel(x, c, ctx, c_ctx, w_ada, b_ada, w_in, da_lambda, da_norm, ml_conv_w, ml_conv_b, ml_ib, ml_fb,
           ml_norm, gl_wa, gl_ba, gl_norm, w_out, ln_mix_g, ln_mix_b, ln_ffn_g, ln_ffn_b,
           moe_wg, moe_we, moe_w1, moe_w3, moe_w2):
    b, s, d = x.shape
    n_ctx = ctx.shape[1]
    n = n_ctx + s
    depth = w_in.shape[0]
    assert n_ctx % TM == 0 and s % (TM * ATTN_NQ) == 0 and d % LANES == 0 and b % SCAN_NB == 0
    dn_alpha = (2 * depth) ** 0.25

    rows = -(-(b + 1) // SUBLANES) * SUBLANES
    cc = jnp.concatenate([c, c_ctx[None, :], jnp.zeros((rows - b - 1, d), F32)], axis=0)
    mod = _ada(cc, w_ada, b_ada).reshape(depth, rows, 6, d)

    xc = jnp.concatenate([ctx, x], axis=1)
    rope = _rope_tables(n_ctx, s)
    e64, e32 = _expand_matrix(64), _expand_matrix(32)

    for l in range(depth):
        last = l == depth - 1
        tile0 = n_ctx // TM if last else 0
        lam_init = 0.8 - 0.6 * math.exp(-0.3 * l)
        modtab = jnp.stack([jnp.broadcast_to(mod[l, b], (b, 6, d)), mod[l, :b]], axis=1)

        wi = w_in[l]
        o = np.cumsum([0, 512, 512, 512, 512, 256, 256, 8, 8, 128, 128, 256, 256, 32])
        misc_w = jnp.concatenate([wi[:, o[6]:o[8]], wi[:, o[12]:o[13]],
                                  jnp.zeros((d, LANES - 16 - 2 * GL_RANK), F32)], axis=1)
        w_in_p = jnp.concatenate([wi[:, :o[6]], wi[:, o[8]:o[12]], misc_w], axis=1).astype(BF16)

        (qa, ka, va, mq, mk, mv, mo, gq, gk, gv, gr, misc) = _in_proj(
            xc, modtab, w_in_p, rope, ml_conv_w[l], ml_conv_b[l].reshape(1, -1), e64, e32, n_ctx)

        n_ctx_tiles = n_ctx // TM
        attend = functools.partial(_attention, qa, ka, va, da_lambda[l], da_norm[l].reshape(1, -1), lam_init)
        da_ctx = None if last else attend(0, n_ctx_tiles, n_ctx, 1)
        da_lat = attend(n_ctx_tiles, s // TM, n, ATTN_NQ)

        gate_bias = jnp.concatenate([ml_ib[l].reshape(-1), ml_fb[l].reshape(-1),
                                     jnp.zeros((LANES - 16,), F32)]).reshape(1, LANES)
        mhf, mhb = _mlstm(mq, mk, mv, misc, gate_bias, n_ctx)

        wa_p = jnp.stack([jnp.pad(_pad_heads(gl_wa[l, dd], GL_DK),
                                  ((M_GA + dd * GL_RANK, LANES - M_GA - (dd + 1) * GL_RANK), (0, 0)))
                          for dd in range(2)])
        ba_p = _pad_heads(gl_ba[l], GL_DK).reshape(2, 1, PAD_W)
        ghf, ghb = _gla(gq, gk, gv, misc, wa_p.astype(BF16), ba_p, n_ctx)

        wo = w_out[l]
        wo_p = jnp.concatenate([
            wo[:512],
            jnp.pad(wo[512:768].reshape(HEADS, 64, d), ((0, 0), (0, 64), (0, 0))).reshape(PAD_W, d),
            jnp.pad(wo[768:1024].reshape(HEADS, 64, d), ((0, 0), (0, 64), (0, 0))).reshape(PAD_W, d),
        ], axis=0).astype(BF16)
        w_route = jnp.concatenate([moe_wg[l], moe_we[l],
                                   jnp.zeros((d, LANES - MOE_GROUPS - MOE_EXPERTS), F32)], axis=1)
        w_route_top = _bf16_part(w_route)
        w_route = jnp.stack([w_route_top.astype(BF16), (w_route - w_route_top).astype(BF16)])
        ml_g = _pad_heads(jnp.tile(ml_norm[l], HEADS), ML_DH).reshape(1, PAD_W)
        gl_g = _pad_heads(jnp.tile(gl_norm[l], HEADS), GL_DV).reshape(1, PAD_W)
        x1, f3, info, cnt, rw = _out_proj(xc, da_ctx, da_lat, mhf, mhb, mo, ghf, ghb, gr, modtab,
                                          ln_mix_g[l].reshape(1, d), ln_mix_b[l].reshape(1, d), ml_g, gl_g,
                                          wo_p, w_route, dn_alpha, n_ctx, tile0)

        n_rows = x1.shape[1]
        n_t = n_rows // TM
        a_all = b * n_rows * 2
        n_blocks = -(-(a_all + MOE_EXPERTS * (MOE_BLOCK - 1)) // MOE_BLOCK)
        tile_counts = cnt.reshape(b * n_t, SUBLANES, LANES)[:, 0, MOE_GROUPS:MOE_GROUPS + MOE_EXPERTS]
        dest8, counts, pad_start, padded, block_e, n_used = _dispatch_tables(
            info.reshape(b, n_t, SUBLANES, TM), tile_counts.astype(jnp.int32), n_blocks)
        xs = _dispatch(dest8, counts, pad_start, padded, f3, n_blocks * MOE_BLOCK)
        y_slots = _moe(xs, block_e, n_used, moe_w1, moe_w3, moe_w2, l)
        n_ctx_tiles_here = 0 if last else n_ctx // TM
        seg = lambda i: jnp.where(i >= n_ctx_tiles_here, 1, 0)
        xc = _combine(dest8, y_slots, x1, rw, modtab, ln_ffn_g[l].reshape(1, d), ln_ffn_b[l].reshape(1, d),
                      dn_alpha, seg)
    return xc
```

```python
import functools
import math

import numpy as np
import jax
import jax.numpy as jnp
from jax import lax
from jax.experimental import pallas as pl
from jax.experimental.pallas import tpu as pltpu

F32 = jnp.float32
BF16 = jnp.bfloat16
HI = lax.Precision.HIGHEST

GRID_W = 64
HEADS = 4
DA_DH = 64
ML_DH = 64
GL_DK = 32
GL_DV = 64
GL_RANK = 16
GL_NORMALIZER = 16.0
CHUNK = 64
SUB = 16
ROPE_BASE = 10000.0
MOE_GROUPS = 4
MOE_PER_GROUP = 8
MOE_EXPERTS = 32
MOE_HIDDEN = 512
LN_EPS = 1e-6

LANES = 128
SUBLANES = 8
VMEM_LIMIT = 48 * 1024 * 1024

TM = 256
MOE_BLOCK = 256
SCAN_NB = 8
ATTN_NQ = 4
PAD_W = HEADS * LANES

C_QA, C_KA, C_VA, C_MQK, C_MV, C_MO = 0, 512, 1024, 1536, 2048, 2304
C_GQ, C_GK, C_GV, C_GR, C_MISC, IN_WP = 2560, 2688, 2816, 3072, 3328, 3456
M_IG, M_FG, M_GA = 0, 8, 16


def _cparams(sem):
    return pltpu.CompilerParams(dimension_semantics=sem, vmem_limit_bytes=VMEM_LIMIT)


def _sigmoid(x):
    return 1.0 / (1.0 + jnp.exp(-x))


def _silu(x):
    return x * _sigmoid(x)


def _log_sigmoid(x):
    return jnp.minimum(x, 0.0) - jnp.log(1.0 + jnp.exp(-jnp.abs(x)))


def _ln(x):
    mu = jnp.mean(x, axis=-1, keepdims=True)
    xc = x - mu
    var = jnp.mean(xc * xc, axis=-1, keepdims=True)
    return xc * lax.rsqrt(var + LN_EPS)


def _bf16_part(x):
    bits = lax.bitcast_convert_type(x, jnp.int32) & jnp.int32(-65536)
    return lax.bitcast_convert_type(bits, F32)


def _dot(a, b):
    return jnp.dot(a, b, preferred_element_type=F32)


def _dot_hi(a, b):
    return jnp.dot(a, b, precision=HI, preferred_element_type=F32)


def _cumsum_dot(a, b, split_lhs=False):
    x = a if split_lhs else b
    x0 = _bf16_part(x)
    r1 = x - x0
    x1 = _bf16_part(r1)
    x2 = r1 - x1
    parts = [p.astype(BF16) for p in (x0, x1, x2)]
    if split_lhs:
        return _dot(parts[0], b) + (_dot(parts[1], b) + _dot(parts[2], b))
    return _dot(a, parts[0]) + (_dot(a, parts[1]) + _dot(a, parts[2]))


def _dot_nt(a, b):
    return lax.dot_general(a, b, (((1,), (1,)), ((), ())), preferred_element_type=F32)


def _dot_tn(a, b):
    return lax.dot_general(a, b, (((0,), (0,)), ((), ())), preferred_element_type=F32)


def _ada_kernel(c_ref, w_ref, b_ref, o_ref):
    o_ref[...] = _dot_hi(_silu(c_ref[...]), w_ref[...]) + b_ref[...]


def _ada(cc, w_ada, b_ada):
    depth, d, n6 = w_ada.shape
    tn = 1536
    rows = cc.shape[0]
    return pl.pallas_call(
        _ada_kernel,
        grid=(depth, n6 // tn),
        in_specs=[
            pl.BlockSpec((rows, d), lambda l, j: (0, 0)),
            pl.BlockSpec((None, d, tn), lambda l, j: (l, 0, j)),
            pl.BlockSpec((None, 1, tn), lambda l, j: (l, 0, j)),
        ],
        out_specs=pl.BlockSpec((None, rows, tn), lambda l, j: (l, 0, j)),
        out_shape=jax.ShapeDtypeStruct((depth, rows, n6), F32),
        compiler_params=_cparams(("arbitrary", "arbitrary")),
        name="ada_mod",
    )(cc, w_ada, b_ada.reshape(depth, 1, n6))


def _in_kernel(n_ctx_tiles, x_ref, xp_ref, xn_ref, mod_ref, w_ref, rope_ref, cw_ref, cb_ref,
               e64_ref, e32_ref,
               qa_ref, ka_ref, va_ref, mq_ref, mk_ref, mv_ref, mo_ref,
               gq_ref, gk_ref, gv_ref, gr_ref, misc_ref, pext_ref):
    i = pl.program_id(1)
    n_tiles = pl.num_programs(1)
    tm = x_ref.shape[0]
    shift, scale = mod_ref[0:1, :], mod_ref[1:2, :]

    def prenorm(x):
        return (_ln(x) * (1.0 + scale) + shift).astype(BF16)

    half = tm // 2
    halves = [slice(0, half), slice(half, tm)]
    groups = ((C_QA, C_KA), (C_KA, C_VA), (C_VA, C_MQK), (C_MQK, C_MV), (C_MV, C_MO), (C_MO, C_GQ),
              (C_GQ, C_GK), (C_GK, C_GV), (C_GV, C_GR), (C_GR, C_MISC), (C_MISC, IN_WP))
    e64 = e64_ref[...]
    e32 = e32_ref[...]
    lane = lax.broadcasted_iota(jnp.int32, (half, PAD_W), 1)
    ones_col = jnp.where(lane % LANES >= ML_DH, 1.0, 0.0)
    seg_first = jnp.logical_or(i == 0, i == n_ctx_tiles)
    seg_last = jnp.logical_or(i == n_ctx_tiles - 1, i == n_tiles - 1)
    keep_prev = jnp.where(seg_first, 0.0, 1.0)
    keep_next = jnp.where(seg_last, 0.0, 1.0)

    hs = [prenorm(x_ref[r, :]) for r in halves]
    halo = jnp.concatenate([prenorm(xp_ref[...]), prenorm(xn_ref[...])], axis=0)
    parts = [{lo: _dot(h, w_ref[:, lo:hi]) for lo, hi in groups} for h in hs]
    p_halo = _dot(halo, w_ref[:, C_MQK:C_MV])
    pext_ref[0:SUBLANES, :] = p_halo[0:SUBLANES] * keep_prev
    pext_ref[SUBLANES + tm:2 * SUBLANES + tm, :] = p_halo[SUBLANES:2 * SUBLANES] * keep_next
    for r, part in zip(halves, parts):
        pext_ref[SUBLANES + r.start:SUBLANES + r.stop, :] = part[C_MQK]

    for r, part in zip(halves, parts):
        cos, sin_a, sin_b = rope_ref[0, r, :], rope_ref[1, r, :], rope_ref[2, r, :]
        for lo, out_ref, mult in ((C_QA, qa_ref, DA_DH ** -0.5 * math.log2(math.e)), (C_KA, ka_ref, 1.0)):
            for hh in range(HEADS):
                xh = part[lo][:, hh * LANES:(hh + 1) * LANES]
                rot = xh * cos + pltpu.roll(xh, LANES - 16, 1) * sin_a + pltpu.roll(xh, 16, 1) * sin_b
                out_ref[r, hh * LANES:(hh + 1) * LANES] = (rot * mult).astype(BF16)
        va_ref[r, :] = part[C_VA].astype(BF16)
        mv_ref[r, :] = (_dot(part[C_MV].astype(BF16), e64) + ones_col).astype(BF16)
        mo_ref[r, :] = _dot(part[C_MO].astype(BF16), e64).astype(BF16)
        gq_ref[r, :] = _dot((part[C_GQ] * GL_DK ** -0.5).astype(BF16), e32).astype(BF16)
        gk_ref[r, :] = _dot(part[C_GK].astype(BF16), e32).astype(BF16)
        gv_ref[r, :] = _dot(part[C_GV].astype(BF16), e64).astype(BF16)
        gr_ref[r, :] = _dot(part[C_GR].astype(BF16), e64).astype(BF16)
        misc_ref[r, :] = part[C_MISC]

    for r, part in zip(halves, parts):
        y = (cw_ref[0:1, :] * pext_ref[SUBLANES - 1 + r.start:SUBLANES - 1 + r.stop, :] + cw_ref[1:2, :] * part[C_MQK]
             + cw_ref[2:3, :] * pext_ref[SUBLANES + 1 + r.start:SUBLANES + 1 + r.stop, :] + cb_ref[...])
        y = _silu(y)
        mq_ref[r, :] = _dot((y[:, :256] * ML_DH ** -0.5).astype(BF16), e64).astype(BF16)
        mk_ref[r, :] = _dot(y[:, 256:].astype(BF16), e64).astype(BF16)


def _in_proj(xc, modtab, w_in_p, rope, conv_w, conv_b, e64, e32, n_ctx):
    b, n, d = xc.shape
    n_tiles = n // TM
    n_ctx_tiles = n_ctx // TM
    hb = TM // SUBLANES
    n_hb = n // SUBLANES
    row = lambda bb, i: (bb, i, 0)
    const2 = lambda bb, i: (0, 0)
    outs = [jax.ShapeDtypeStruct((b, n, PAD_W), BF16)] * 11 + [jax.ShapeDtypeStruct((b, n, LANES), F32)]
    out_specs = [pl.BlockSpec((None, TM, PAD_W), row)] * 11 + [pl.BlockSpec((None, TM, LANES), row)]
    return pl.pallas_call(
        functools.partial(_in_kernel, n_ctx_tiles),
        grid=(b, n_tiles),
        in_specs=[
            pl.BlockSpec((None, TM, d), row),
            pl.BlockSpec((None, SUBLANES, d), lambda bb, i: (bb, jnp.maximum(i * hb - 1, 0), 0)),
            pl.BlockSpec((None, SUBLANES, d), lambda bb, i: (bb, jnp.minimum((i + 1) * hb, n_hb - 1), 0)),
            pl.BlockSpec((None, None, 6, d), lambda bb, i: (bb, jnp.where(i >= n_ctx_tiles, 1, 0), 0, 0)),
            pl.BlockSpec((d, IN_WP), const2),
            pl.BlockSpec((3, TM, LANES), lambda bb, i: (0, i, 0)),
            pl.BlockSpec((3, 512), const2),
            pl.BlockSpec((1, 512), const2),
            pl.BlockSpec((256, PAD_W), const2),
            pl.BlockSpec((128, PAD_W), const2),
        ],
        out_specs=out_specs,
        out_shape=outs,
        scratch_shapes=[pltpu.VMEM((TM + 2 * SUBLANES, 512), F32)],
        compiler_params=_cparams(("parallel", "arbitrary")),
        name="in_proj",
    )(xc, xc, xc, modtab, w_in_p, rope, conv_w, conv_b, e64, e32)


def _attn_kernel(lam_init, n_q, *refs):
    q_refs = refs[:n_q]
    k_ref, v_ref, lamp_ref, g_ref, o_ref = refs[n_q:]
    lp = lamp_ref[...]
    lam = (jnp.exp(jnp.sum(lp[0:1] * lp[1:2], axis=1, keepdims=True))
           - jnp.exp(jnp.sum(lp[2:3] * lp[3:4], axis=1, keepdims=True)) + lam_init)
    k = k_ref[...]
    v = jnp.concatenate([v_ref[...], jnp.ones(v_ref.shape, BF16)], axis=1)
    tq = q_refs[0].shape[0]
    lane = lax.broadcasted_iota(jnp.int32, (tq, LANES), 1)
    scores = []
    for q_ref in q_refs:
        q = q_ref[...]
        zero = jnp.zeros_like(q)
        scores.append((_dot_nt(jnp.where(lane < DA_DH, q, zero), k), _dot_nt(jnp.where(lane >= DA_DH, q, zero), k)))
    for t, (s1, s2) in enumerate(scores):
        p1 = _dot(jnp.exp2(s1 - jnp.max(s1, axis=1, keepdims=True)).astype(BF16), v)
        p2 = _dot(jnp.exp2(s2 - jnp.max(s2, axis=1, keepdims=True)).astype(BF16), v)
        o = p1[:, 0:LANES] / p1[:, LANES:2 * LANES] - p2[:, 0:LANES] * (lam / p2[:, LANES:2 * LANES])
        y = o * lax.rsqrt(jnp.mean(o * o, axis=1, keepdims=True) + LN_EPS) * g_ref[...]
        o_ref[t * tq:(t + 1) * tq, :] = (y * (1.0 - lam_init)).astype(BF16)


def _attention(qa, ka, va, lam_p, norm_g, lam_init, tile0, n_tiles, n_keys, n_q):
    b = qa.shape[0]
    tq = TM
    q_spec = lambda t: pl.BlockSpec((None, tq, LANES), lambda bb, h, i: (bb, tile0 + i * n_q + t, h))
    kv_spec = pl.BlockSpec((None, n_keys, LANES), lambda bb, h, i: (bb, 0, h))
    return pl.pallas_call(
        functools.partial(_attn_kernel, lam_init, n_q),
        grid=(b, HEADS, n_tiles // n_q),
        in_specs=[q_spec(t) for t in range(n_q)] + [
            kv_spec, kv_spec,
            pl.BlockSpec((4, DA_DH), lambda bb, h, i: (0, 0)),
            pl.BlockSpec((1, LANES), lambda bb, h, i: (0, 0)),
        ],
        out_specs=pl.BlockSpec((None, n_q * tq, LANES), lambda bb, h, i: (bb, i, h)),
        out_shape=jax.ShapeDtypeStruct((b, n_tiles * tq, PAD_W), BF16),
        compiler_params=_cparams(("parallel", "parallel", "arbitrary")),
        name="diff_attn",
    )(*([qa] * n_q), ka, va, lam_p, norm_g)


def _chunk_maps(n_ctx_chunks, n_chunks):
    def fwd(bb, t):
        return (bb, t, 0)

    def bwd(bb, t):
        return (bb, jnp.where(t < n_ctx_chunks, n_ctx_chunks - 1 - t, n_chunks - 1 + n_ctx_chunks - t), 0)

    return fwd, bwd


def _tri(direction):
    l = lax.broadcasted_iota(jnp.int32, (CHUNK, CHUNK), 0)
    s = lax.broadcasted_iota(jnp.int32, (CHUNK, CHUNK), 1)
    return (s <= l) if direction == 0 else (s >= l)


def _mlstm_kernel(qf_ref, kf_ref, vf_ref, gf_ref, qb_ref, kb_ref, vb_ref, gb_ref, bias_ref, sel_ref,
                  hf_ref, hb_ref, c_ref, m_ref):
    t = pl.program_id(1)

    @pl.when(t == 0)
    def _():
        c_ref[...] = jnp.zeros_like(c_ref)
        m_ref[...] = jnp.zeros_like(m_ref)

    lane = lax.broadcasted_iota(jnp.int32, (CHUNK, LANES), 1)
    is_fg = jnp.logical_and(lane >= M_FG, lane < M_GA)
    dirs = ((qf_ref, kf_ref, vf_ref, gf_ref, hf_ref), (qb_ref, kb_ref, vb_ref, gb_ref, hb_ref))
    n_b = qf_ref.shape[0]
    masks = [_tri(d) for d in range(2)]
    units = [(d, nb, h) for d in range(2) for nb in range(n_b) for h in range(HEADS)]

    pre = {}
    for d in range(2):
        tri = jnp.where(masks[d], 1.0, 0.0).astype(BF16)
        for nb in range(n_b):
            g = dirs[d][3][nb] + bias_ref[...]
            g = jnp.where(is_fg, _log_sigmoid(g), g)
            cum = _cumsum_dot(tri, g)
            pre[d, nb] = (_cumsum_dot(cum, sel_ref[d, 0], split_lhs=True),
                          _cumsum_dot(g, sel_ref[d, 1], split_lhs=True), g.T, cum.T)

    qk, qc = {}, {}
    for u in units:
        d, nb, h = u
        hs = slice(h * LANES, (h + 1) * LANES)
        q = dirs[d][0][nb, :, hs]
        qk[u] = _dot_nt(q, dirs[d][1][nb, :, hs])
        qc[u] = _dot(q, c_ref[nb, d, h].astype(BF16))

    gate = {}
    for u in units:
        d, nb, h = u
        hs = slice(h * LANES, (h + 1) * LANES)
        b_all, i_all, g_t, cum_t = pre[d, nb]
        ci, cf = M_IG + HEADS * d + h, M_FG + HEADS * d + h
        last = CHUNK - 1 if d == 0 else 0
        b_rep = b_all[:, hs]
        i_rep = i_all[:, hs]
        row = g_t[ci:ci + 1, :] - cum_t[cf:cf + 1, :]
        m_prev = m_ref[nb, d, h][0:1, :]
        log_d = jnp.where(masks[d], b_rep[:, 0:CHUNK] + row, -jnp.inf)
        log_a = b_rep + m_prev
        m_t = jnp.maximum(log_a, jnp.max(log_d, axis=1, keepdims=True))
        m_new = m_t[last:last + 1, :]
        b_last = b_rep[last:last + 1, :]
        gate[u] = (jnp.exp(log_d - m_t[:, 0:CHUNK]), jnp.exp(log_a - m_t), jnp.exp(-m_t),
                   jnp.exp(b_last - b_rep + i_rep - m_new), jnp.exp(b_last + m_prev - m_new), m_new)

    sv, kv = {}, {}
    for u in units:
        d, nb, h = u
        hs = slice(h * LANES, (h + 1) * LANES)
        v = dirs[d][2][nb, :, hs]
        dmat, _, _, w, _, _ = gate[u]
        sv[u] = _dot((qk[u] * dmat).astype(BF16), v)
        kv[u] = _dot_tn((dirs[d][1][nb, :, hs].astype(F32) * w).astype(BF16), v)

    for u in units:
        d, nb, h = u
        hs = slice(h * LANES, (h + 1) * LANES)
        _, a, floor, _, decay, m_new = gate[u]
        num = a * qc[u] + sv[u]
        qn = pltpu.roll(num, ML_DH, 1)
        den = jnp.maximum(jnp.abs(qn), floor)
        dirs[d][4][nb, :, hs] = jnp.where(lane < ML_DH, num / den, 0.0).astype(BF16)
        c_ref[nb, d, h] = decay * c_ref[nb, d, h] + kv[u]
        m_ref[nb, d, h] = jnp.broadcast_to(m_new, (SUBLANES, LANES))


def _gate_select_matrices():
    sel = np.zeros((2, 2, LANES, PAD_W), np.float32)
    for d in range(2):
        for h in range(HEADS):
            sel[d, 0, M_FG + HEADS * d + h, h * LANES:(h + 1) * LANES] = 1.0
            sel[d, 1, M_IG + HEADS * d + h, h * LANES:(h + 1) * LANES] = 1.0
    return jnp.asarray(sel, BF16)


def _mlstm(mq, mk, mv, misc, gate_bias, n_ctx):
    b, n, _ = mq.shape
    n_chunks = n // CHUNK
    fwd, bwd = _chunk_maps(n_ctx // CHUNK, n_chunks)
    wide = lambda m: pl.BlockSpec((SCAN_NB, CHUNK, PAD_W), m)
    thin = lambda m: pl.BlockSpec((SCAN_NB, CHUNK, LANES), m)
    return pl.pallas_call(
        _mlstm_kernel,
        grid=(b // SCAN_NB, n_chunks),
        in_specs=[wide(fwd), wide(fwd), wide(fwd), thin(fwd), wide(bwd), wide(bwd), wide(bwd), thin(bwd),
                  pl.BlockSpec((1, LANES), lambda bb, t: (0, 0)),
                  pl.BlockSpec((2, 2, LANES, PAD_W), lambda bb, t: (0, 0, 0, 0))],
        out_specs=[wide(fwd), wide(bwd)],
        out_shape=[jax.ShapeDtypeStruct((b, n, PAD_W), BF16)] * 2,
        scratch_shapes=[pltpu.VMEM((SCAN_NB, 2, HEADS, LANES, LANES), F32),
                        pltpu.VMEM((SCAN_NB, 2, HEADS, SUBLANES, LANES), F32)],
        compiler_params=_cparams(("parallel", "arbitrary")),
        name="mlstm_scan",
    )(mq, mk, mv, misc, mq, mk, mv, misc, gate_bias, _gate_select_matrices())


def _gla_kernel(qf_ref, kf_ref, vf_ref, gf_ref, qb_ref, kb_ref, vb_ref, gb_ref, wa_ref, ba_ref,
                hf_ref, hb_ref, st_ref):
    t = pl.program_id(1)

    @pl.when(t == 0)
    def _():
        st_ref[...] = jnp.zeros_like(st_ref)

    n_sub = CHUNK // SUB
    row_id = lax.broadcasted_iota(jnp.int32, (CHUNK, LANES), 0)
    row_blk = row_id // SUB
    col_id = lax.broadcasted_iota(jnp.int32, (CHUNK, LANES), 1)
    rows_w = lax.broadcasted_iota(jnp.int32, (CHUNK, PAD_W), 0)
    zeros = jnp.zeros((CHUNK, LANES), BF16)
    dirs = ((qf_ref, kf_ref, vf_ref, gf_ref, hf_ref), (qb_ref, kb_ref, vb_ref, gb_ref, hb_ref))
    n_b = qf_ref.shape[0]
    pairs = [(d, nb) for d in range(2) for nb in range(n_b)]
    units = [(d, nb, h) for d, nb in pairs for h in range(HEADS)]
    tris = [jnp.where(_tri(d), 1.0, 0.0).astype(BF16) for d in range(2)]
    causal = [col_id <= row_id, jnp.logical_and(col_id >= row_id, col_id < CHUNK)]

    las = {p: _log_sigmoid(_dot(dirs[p[0]][3][p[1]].astype(BF16), wa_ref[p[0]]) + ba_ref[p[0]])
           * (1.0 / GL_NORMALIZER) for p in pairs}
    cums = {p: _cumsum_dot(tris[p[0]], las[p]) for p in pairs}

    pre = {}
    for p in pairs:
        d, nb = p
        cum = cums[p]
        excl = cum - las[p]
        last = CHUNK - 1 if d == 0 else 0
        betas = [excl[r:r + 1, :] for r in (range(0, CHUNK, SUB) if d == 0 else range(SUB - 1, CHUNK, SUB))]
        beta_rows = jnp.concatenate([jnp.broadcast_to(bt, (SUB, PAD_W)) for bt in betas], axis=0)
        q = dirs[d][0][nb].astype(F32)
        k = dirs[d][1][nb].astype(F32)
        q_loc = (q * jnp.exp(cum - beta_rows)).astype(BF16)
        q_abs = (q * jnp.exp(cum)).astype(BF16)
        b_last = cum[last:last + 1, :]
        k_end = (k * jnp.exp(b_last - cum)).astype(BF16)
        k_subs = []
        for I in range(n_sub):
            seen = (rows_w < (I + 1) * SUB) if d == 0 else (rows_w >= I * SUB)
            k_subs.append((k * jnp.exp(jnp.where(seen, betas[I] - cum, -jnp.inf))).astype(BF16))
        pre[p] = (q_loc, q_abs, b_last, k_end, k_subs)

    scores, inter, kv = {}, {}, {}
    for u in units:
        d, nb, h = u
        hs = slice(h * LANES, (h + 1) * LANES)
        q_loc, q_abs, _, k_end, k_subs = pre[d, nb]
        kcat = jnp.concatenate([x for I in range(n_sub) for x in (k_subs[I][:, hs], zeros)], axis=0)
        scores[u] = _dot_nt(q_loc[:, hs], kcat)
        inter[u] = _dot_nt(q_abs[:, hs], st_ref[nb, d, h].astype(BF16))
        kv[u] = _dot_tn(dirs[d][2][nb, :, hs], k_end[:, hs])

    att = {}
    for u in units:
        r = scores[u]
        a = r[:, 0:LANES]
        for I in range(1, n_sub):
            a = jnp.where(row_blk == I, r[:, I * LANES:(I + 1) * LANES], a)
        att[u] = jnp.where(causal[u[0]], a, 0.0)[:, 0:CHUNK].astype(BF16)

    av = {u: _dot(att[u], dirs[u[0]][2][u[1], :, u[2] * LANES:(u[2] + 1) * LANES]) for u in units}
    for u in units:
        d, nb, h = u
        hs = slice(h * LANES, (h + 1) * LANES)
        dirs[d][4][nb, :, hs] = (inter[u] + av[u]).astype(BF16)
        st_ref[nb, d, h] = st_ref[nb, d, h] * jnp.exp(pre[d, nb][2][:, hs]) + kv[u]


def _gla(gq, gk, gv, misc, wa_p, ba_p, n_ctx):
    b, n, _ = gq.shape
    n_chunks = n // CHUNK
    fwd, bwd = _chunk_maps(n_ctx // CHUNK, n_chunks)
    wide = lambda m: pl.BlockSpec((SCAN_NB, CHUNK, PAD_W), m)
    thin = lambda m: pl.BlockSpec((SCAN_NB, CHUNK, LANES), m)
    return pl.pallas_call(
        _gla_kernel,
        grid=(b // SCAN_NB, n_chunks),
        in_specs=[wide(fwd), wide(fwd), wide(fwd), thin(fwd), wide(bwd), wide(bwd), wide(bwd), thin(bwd),
                  pl.BlockSpec((2, LANES, PAD_W), lambda bb, t: (0, 0, 0)),
                  pl.BlockSpec((2, 1, PAD_W), lambda bb, t: (0, 0, 0))],
        out_specs=[wide(fwd), wide(bwd)],
        out_shape=[jax.ShapeDtypeStruct((b, n, PAD_W), BF16)] * 2,
        scratch_shapes=[pltpu.VMEM((SCAN_NB, 2, HEADS, LANES, LANES), F32)],
        compiler_params=_cparams(("parallel", "arbitrary")),
        name="gla_scan",
    )(gq, gk, gv, misc, gq, gk, gv, misc, wa_p, ba_p)


def _head_rms_padded(x, width):
    parts = []
    for h in range(HEADS):
        xh = x[:, h * LANES:(h + 1) * LANES]
        ms = jnp.sum(xh * xh, axis=1, keepdims=True) * (1.0 / width)
        parts.append(xh * lax.rsqrt(ms + LN_EPS))
    return jnp.concatenate(parts, axis=1)


def _out_kernel(dn_alpha, n_ctx_tiles, x_ref, dac_ref, dal_ref, mhf_ref, mhb_ref, mo_ref, ghf_ref, ghb_ref, gr_ref,
                mod_ref, lng_ref, lnb_ref, mlg_ref, glg_ref, wo_ref, wr_ref,
                x1_ref, f_ref, info_ref, cnt_ref, rw_ref):
    tm = x_ref.shape[0]
    half = tm // 2
    halves = [slice(0, half), slice(half, tm)]
    gate1, shift2, scale2 = mod_ref[2:3, :], mod_ref[3:4, :], mod_ref[4:5, :]
    is_ctx = pl.program_id(1) < n_ctx_tiles
    lane = lax.broadcasted_iota(jnp.int32, (half, LANES), 1)
    neg = -jnp.inf
    big = jnp.int32(LANES)

    def first_argmax(vals, vmax):
        return jnp.min(jnp.where(vals == vmax, lane, big), axis=1, keepdims=True)

    mixed = []
    for r in halves:
        ml = (_head_rms_padded(mhf_ref[r, :].astype(F32) + mhb_ref[r, :].astype(F32), ML_DH) * mlg_ref[...]
              * _sigmoid(mo_ref[r, :].astype(F32)))
        gl = (_head_rms_padded(ghf_ref[r, :].astype(F32) + ghb_ref[r, :].astype(F32), GL_DV) * glg_ref[...]
              * _silu(gr_ref[r, :].astype(F32)))
        mixed.append((jnp.where(is_ctx, dac_ref[r, :], dal_ref[r, :]), ml.astype(BF16), gl.astype(BF16)))
    ys = [_dot(da, wo_ref[0:PAD_W, :]) + _dot(ml, wo_ref[PAD_W:2 * PAD_W, :])
          + _dot(gl, wo_ref[2 * PAD_W:3 * PAD_W, :]) for da, ml, gl in mixed]
    fs = []
    for r, y in zip(halves, ys):
        x1 = _ln(dn_alpha * x_ref[r, :] + gate1 * y) * lng_ref[...] + lnb_ref[...]
        x1_ref[r, :] = x1
        f = _ln(x1) * (1.0 + scale2) + shift2
        for s in range(SUBLANES):
            f_ref[pl.ds(r.start * SUBLANES + s, half, stride=SUBLANES), :] = f[:, s * LANES:(s + 1) * LANES]
        fs.append(f)

    all_logits = []
    for f in fs:
        f_top = _bf16_part(f)
        f_hi, f_lo = f_top.astype(BF16), (f - f_top).astype(BF16)
        all_logits.append(_dot(f_hi, wr_ref[0]) + (_dot(f_lo, wr_ref[0]) + _dot(f_hi, wr_ref[1])
                                                   + _dot(f_lo, wr_ref[1])))
    routed = []
    for r, logits in zip(halves, all_logits):
        gl_ = jnp.where(lane < MOE_GROUPS, logits, neg)
        gmax = jnp.max(gl_, axis=1, keepdims=True)
        g_idx = first_argmax(gl_, gmax)
        g_gate = 1.0 / jnp.sum(jnp.exp(gl_ - gmax), axis=1, keepdims=True)
        in_group = jnp.logical_and(lane >= MOE_GROUPS, (lane - MOE_GROUPS) // MOE_PER_GROUP == g_idx)
        el = jnp.where(in_group, logits, neg)
        emax = jnp.max(el, axis=1, keepdims=True)
        e0 = first_argmax(el, emax)
        el2 = jnp.where(lane == e0, neg, el)
        emax2 = jnp.max(el2, axis=1, keepdims=True)
        e1 = first_argmax(el2, emax2)
        esum = jnp.sum(jnp.exp(el - emax), axis=1, keepdims=True)
        p0 = 1.0 / esum
        p1 = jnp.exp(emax2 - emax) / esum
        w0 = g_gate * p0 / (p0 + p1)
        w1 = g_gate * p1 / (p0 + p1)
        rw_ref[r, :] = jnp.where(lane == 0, w0, jnp.where(lane == 1, w1, 0.0))
        routed.append((e0, e1))

    e0 = jnp.concatenate([e[0] for e in routed], axis=0)
    e1 = jnp.concatenate([e[1] for e in routed], axis=0)
    lane = lax.broadcasted_iota(jnp.int32, (tm, LANES), 1)
    oh0 = jnp.where(lane == e0, 1.0, 0.0)
    oh1 = jnp.where(lane == e1, 1.0, 0.0)
    both = (oh0 + oh1).astype(BF16)
    r_id = lax.broadcasted_iota(jnp.int32, (tm, tm), 0)
    c_id = lax.broadcasted_iota(jnp.int32, (tm, tm), 1)
    earlier = jnp.where(c_id < r_id, 1.0, 0.0).astype(BF16)
    before = _dot(earlier, both)
    rank0 = jnp.sum(before * oh0, axis=1, keepdims=True)
    rank1 = jnp.sum(before * oh1, axis=1, keepdims=True)
    cnt_ref[...] = _dot(jnp.ones((SUBLANES, tm), BF16), both)
    info = jnp.where(lane == 0, (e0 - MOE_GROUPS).astype(F32),
                     jnp.where(lane == 1, (e1 - MOE_GROUPS).astype(F32),
                               jnp.where(lane == 2, rank0, jnp.where(lane == 3, rank1, 0.0))))
    info_ref[...] = info.T[0:SUBLANES, :]


def _out_proj(xc, da_ctx, da_lat, mhf, mhb, mo, ghf, ghb, gr, modtab, ln_g, ln_b, ml_g, gl_g, wo_p, w_route,
              dn_alpha, n_ctx, tile0):
    b, n, d = xc.shape
    n_ctx_tiles = n_ctx // TM
    n_t = n // TM - tile0
    out_ctx_tiles = n_ctx_tiles - tile0
    row = lambda bb, i: (bb, i + tile0, 0)
    orow = lambda bb, i: (bb, i, 0)
    const2 = lambda bb, i: (0, 0)
    wide = pl.BlockSpec((None, TM, PAD_W), row)
    n_out = n_t * TM
    if da_ctx is None:
        da_ctx = da_lat
    ctx_rows = lambda bb, i: (bb, jnp.minimum(i, max(out_ctx_tiles - 1, 0)), 0)
    lat_rows = lambda bb, i: (bb, jnp.maximum(i - out_ctx_tiles, 0), 0)
    return pl.pallas_call(
        functools.partial(_out_kernel, dn_alpha, out_ctx_tiles),
        grid=(b, n_t),
        in_specs=[
            pl.BlockSpec((None, TM, d), row),
            pl.BlockSpec((None, TM, PAD_W), ctx_rows), pl.BlockSpec((None, TM, PAD_W), lat_rows),
            wide, wide, wide, wide, wide, wide,
            pl.BlockSpec((None, None, 6, d), lambda bb, i: (bb, jnp.where(i + tile0 >= n_ctx_tiles, 1, 0), 0, 0)),
            pl.BlockSpec((1, d), const2), pl.BlockSpec((1, d), const2),
            pl.BlockSpec((1, PAD_W), const2), pl.BlockSpec((1, PAD_W), const2),
            pl.BlockSpec((3 * PAD_W, d), const2),
            pl.BlockSpec((2, d, LANES), lambda bb, i: (0, 0, 0)),
        ],
        out_specs=[pl.BlockSpec((None, TM, d), orow),
                   pl.BlockSpec((None, TM * SUBLANES, LANES), orow),
                   pl.BlockSpec((None, SUBLANES, TM), orow),
                   pl.BlockSpec((None, SUBLANES, LANES), orow),
                   pl.BlockSpec((None, TM, LANES), orow)],
        out_shape=[jax.ShapeDtypeStruct((b, n_out, d), F32),
                   jax.ShapeDtypeStruct((b, n_out * SUBLANES, LANES), F32),
                   jax.ShapeDtypeStruct((b, n_t * SUBLANES, TM), F32),
                   jax.ShapeDtypeStruct((b, n_t * SUBLANES, LANES), F32),
                   jax.ShapeDtypeStruct((b, n_out, LANES), F32)],
        compiler_params=_cparams(("parallel", "arbitrary")),
        name="out_proj",
    )(xc, da_ctx, da_lat, mhf, mhb, mo, ghf, ghb, gr, modtab, ln_g, ln_b, ml_g, gl_g, wo_p, w_route)


def _tile_copy(src, src_tile, dst, dst_tile, sem):
    aligned = lambda t: t if isinstance(t, int) else pl.multiple_of(t, SUBLANES)
    return pltpu.make_async_copy(src.at[pl.ds(aligned(src_tile), SUBLANES), :],
                                 dst.at[pl.ds(aligned(dst_tile), SUBLANES), :], sem)


def _rows_from_tiles(ref, n_rows):
    return jnp.concatenate([ref[pl.ds(s, n_rows, stride=SUBLANES), :] for s in range(SUBLANES)], axis=1)


def _disp_kernel(dest_ref, cnt_ref, start_ref, padded_ref, f_ref, xs_hbm, zero_ref, sem, zsem, bsem):
    bb = pl.program_id(0)
    i = pl.program_id(1)
    tm = f_ref.shape[0] // SUBLANES
    n_tok = pl.num_programs(0) * pl.num_programs(1) * tm
    base = (bb * pl.num_programs(1) + i) * tm

    def scatter(start):
        def body(r, carry):
            for kk in range(2):
                cp = _tile_copy(f_ref, r * SUBLANES, xs_hbm, dest_ref[kk * n_tok + base + r], sem)
                if start:
                    cp.start(priority=kk)
                else:
                    cp.wait()
            return carry
        lax.fori_loop(0, tm, body, 0, unroll=8)

    scatter(True)
    scatter(False)

    @pl.when(jnp.logical_and(bb == pl.num_programs(0) - 1, i == pl.num_programs(1) - 1))
    def _():
        zero_ref[...] = jnp.zeros_like(zero_ref)
        n_slots = xs_hbm.shape[0] // SUBLANES
        last = MOE_EXPERTS - 1

        blk_tiles = zero_ref.shape[0]

        def fill(start):
            def row(r, c2):
                cp = _tile_copy(zero_ref, 0, xs_hbm, r * SUBLANES, zsem)
                if start:
                    cp.start()
                else:
                    cp.wait()
                return c2

            def block(j, c2):
                cp = pltpu.make_async_copy(zero_ref, xs_hbm.at[pl.ds(pl.multiple_of(j * blk_tiles, blk_tiles),
                                                                     blk_tiles), :], bsem)
                if start:
                    cp.start()
                else:
                    cp.wait()
                return c2

            def per_expert(e, carry):
                return lax.fori_loop(start_ref[e] + cnt_ref[e], start_ref[e] + padded_ref[e], row, carry)
            lax.fori_loop(0, MOE_EXPERTS, per_expert, 0)
            used_blocks = (start_ref[last] + padded_ref[last]) // (blk_tiles // SUBLANES)
            lax.fori_loop(used_blocks, n_slots * SUBLANES // blk_tiles, block, 0)

        fill(True)
        fill(False)


def _dispatch(dest8, counts, pad_start, padded, f3, n_slots):
    b, rows, _ = f3.shape
    n_t = rows // (TM * SUBLANES)
    return pl.pallas_call(
        _disp_kernel,
        grid_spec=pltpu.PrefetchScalarGridSpec(
            num_scalar_prefetch=4,
            grid=(b, n_t),
            in_specs=[pl.BlockSpec((TM * SUBLANES, LANES), lambda bb, i, *_: (bb * n_t + i, 0))],
            out_specs=pl.BlockSpec(memory_space=pl.ANY),
            scratch_shapes=[pltpu.VMEM((MOE_BLOCK * SUBLANES, LANES), F32), pltpu.SemaphoreType.DMA(()),
                            pltpu.SemaphoreType.DMA(()), pltpu.SemaphoreType.DMA(())],
        ),
        out_shape=jax.ShapeDtypeStruct((n_slots * SUBLANES, LANES), F32),
        compiler_params=_cparams(("arbitrary", "arbitrary")),
        name="moe_dispatch",
    )(dest8, counts, pad_start, padded, f3.reshape(b * rows, LANES))


def _moe_kernel(be_ref, nu_ref, xs_ref, w1_ref, w3_ref, w2_ref, y_ref, w1b, w3b, w2b):
    j = pl.program_id(0)
    n_used = nu_ref[0]
    blk = xs_ref.shape[0] // SUBLANES

    @pl.when(jnp.logical_and(j < n_used, jnp.logical_or(j == 0, be_ref[j] != be_ref[jnp.maximum(j - 1, 0)])))
    def _():
        w1b[...] = w1_ref[...].astype(BF16)
        w3b[...] = w3_ref[...].astype(BF16)
        w2b[...] = w2_ref[...].astype(BF16)

    @pl.when(j < n_used)
    def _():
        wide = 2 * LANES
        h1 = h3 = None
        for c in range(w1b.shape[0] // wide):
            xc = jnp.concatenate([xs_ref[pl.ds(2 * c + s, blk, stride=SUBLANES), :] for s in range(2)],
                                 axis=1).astype(BF16)
            d1 = _dot(xc, w1b[c * wide:(c + 1) * wide, :])
            d3 = _dot(xc, w3b[c * wide:(c + 1) * wide, :])
            h1, h3 = (d1, d3) if c == 0 else (h1 + d1, h3 + d3)
        hidden = (_silu(h1) * h3).astype(BF16)
        for c in range(w2b.shape[1] // wide):
            yc = _dot(hidden, w2b[:, c * wide:(c + 1) * wide])
            for s in range(2):
                y_ref[pl.ds(2 * c + s, blk, stride=SUBLANES), :] = yc[:, s * LANES:(s + 1) * LANES]

    @pl.when(j >= n_used)
    def _():
        y_ref[...] = jnp.zeros_like(y_ref)


def _moe(xs, block_e, n_used, w1, w3, w2, layer):
    n_blocks = block_e.shape[0]
    d, hid = w1.shape[-2:]
    wmap = lambda j, be, nu: (layer, be[j], 0, 0)
    tiles = MOE_BLOCK * SUBLANES
    return pl.pallas_call(
        _moe_kernel,
        grid_spec=pltpu.PrefetchScalarGridSpec(
            num_scalar_prefetch=2,
            grid=(n_blocks,),
            in_specs=[
                pl.BlockSpec((tiles, LANES), lambda j, be, nu: (jnp.minimum(j, jnp.maximum(nu[0] - 1, 0)), 0)),
                pl.BlockSpec((None, None, d, hid), wmap),
                pl.BlockSpec((None, None, d, hid), wmap),
                pl.BlockSpec((None, None, hid, d), wmap),
            ],
            out_specs=pl.BlockSpec((tiles, LANES), lambda j, be, nu: (j, 0)),
            scratch_shapes=[pltpu.VMEM((d, hid), BF16), pltpu.VMEM((d, hid), BF16), pltpu.VMEM((hid, d), BF16)],
        ),
        out_shape=jax.ShapeDtypeStruct((n_blocks * tiles, LANES), F32),
        compiler_params=_cparams(("arbitrary",)),
        name="moe_experts",
    )(block_e, n_used, xs, w1, w3, w2)


def _comb_kernel(dn_alpha, dest_ref, y_hbm, x1_ref, rw_ref, mod_ref, lng_ref, lnb_ref, o_ref, ybuf, sems):
    bb = pl.program_id(0)
    i = pl.program_id(1)
    tm = x1_ref.shape[0]
    n_tok = pl.num_programs(0) * pl.num_programs(1) * tm
    base = (bb * pl.num_programs(1) + i) * tm

    def gather(start):
        def body(r, carry):
            for kk in range(2):
                cp = _tile_copy(y_hbm, dest_ref[kk * n_tok + base + r], ybuf.at[kk], r * SUBLANES, sems.at[kk])
                if start:
                    cp.start(priority=kk)
                else:
                    cp.wait()
            return carry
        lax.fori_loop(0, tm, body, 0, unroll=8)

    gather(True)
    gather(False)
    rw = rw_ref[...]
    y = rw[:, 0:1] * _rows_from_tiles(ybuf.at[0], tm) + rw[:, 1:2] * _rows_from_tiles(ybuf.at[1], tm)
    gate2 = mod_ref[5:6, :]
    o_ref[...] = _ln(dn_alpha * x1_ref[...] + gate2 * y) * lng_ref[...] + lnb_ref[...]


def _combine(dest, y_slots, x1, rw, modtab, ln_g, ln_b, dn_alpha, seg_of_tile):
    b, n, d = x1.shape
    n_t = n // TM
    row = lambda bb, i, dr: (bb, i, 0)
    const2 = lambda bb, i, dr: (0, 0)
    return pl.pallas_call(
        functools.partial(_comb_kernel, dn_alpha),
        grid_spec=pltpu.PrefetchScalarGridSpec(
            num_scalar_prefetch=1,
            grid=(b, n_t),
            in_specs=[
                pl.BlockSpec(memory_space=pl.ANY),
                pl.BlockSpec((None, TM, d), row),
                pl.BlockSpec((None, TM, LANES), row),
                pl.BlockSpec((None, None, 6, d), lambda bb, i, dr: (bb, seg_of_tile(i), 0, 0)),
                pl.BlockSpec((1, d), const2), pl.BlockSpec((1, d), const2),
            ],
            out_specs=pl.BlockSpec((None, TM, d), row),
            scratch_shapes=[pltpu.VMEM((2, TM * SUBLANES, LANES), F32), pltpu.SemaphoreType.DMA((2,))],
        ),
        out_shape=jax.ShapeDtypeStruct((b, n, d), F32),
        compiler_params=_cparams(("arbitrary", "arbitrary")),
        name="moe_combine",
    )(dest, y_slots, x1, rw, modtab, ln_g, ln_b)


def _dispatch_tables(info, tile_counts, n_blocks):
    b, n_t, _, tm = info.shape
    info = info.astype(jnp.int32)
    expert = jnp.moveaxis(info[:, :, 0:2, :], 2, 0).reshape(2, b * n_t, tm)
    rank = jnp.moveaxis(info[:, :, 2:4, :], 2, 0).reshape(2, b * n_t, tm)
    counts = jnp.sum(tile_counts, axis=0)
    padded = (counts + MOE_BLOCK - 1) // MOE_BLOCK * MOE_BLOCK
    pad_end = jnp.cumsum(padded)
    pad_start = pad_end - padded
    tile_base = pad_start[None, :] + jnp.cumsum(tile_counts, axis=0) - tile_counts
    onehot = expert[..., None] == jnp.arange(MOE_EXPERTS, dtype=jnp.int32)
    dest = jnp.sum(jnp.where(onehot, tile_base[None, :, None, :], 0), axis=-1) + rank
    dest8 = (dest * SUBLANES).reshape(-1).astype(jnp.int32)
    block_start = jnp.arange(n_blocks, dtype=jnp.int32) * MOE_BLOCK
    block_e = jnp.minimum(jnp.sum(pad_end[None, :] <= block_start[:, None], axis=1), MOE_EXPERTS - 1).astype(jnp.int32)
    n_used = (pad_end[-1] // MOE_BLOCK).astype(jnp.int32).reshape(1)
    i32 = lambda v: v.astype(jnp.int32)
    return dest8, i32(counts), i32(pad_start), i32(padded), block_e, n_used


def _pad_heads(w, width):
    lead = w.shape[:-1]
    w = w.reshape(*lead, HEADS, width)
    w = jnp.pad(w, [(0, 0)] * len(lead) + [(0, 0), (0, LANES - width)])
    return w.reshape(*lead, PAD_W)


def _expand_matrix(width):
    e = np.zeros((HEADS * width, PAD_W), np.float32)
    for h in range(HEADS):
        e[h * width + np.arange(width), h * LANES + np.arange(width)] = 1.0
    return jnp.asarray(e, BF16)


def _rope_tables(n_ctx, n_lat):
    n_freq = DA_DH // 4
    inv_freq = ROPE_BASE ** (-jnp.arange(n_freq, dtype=F32) / n_freq)
    pos = jnp.arange(n_lat, dtype=jnp.int32)
    ang_r = (pos // GRID_W).astype(F32)[:, None] * inv_freq
    ang_c = (pos % GRID_W).astype(F32)[:, None] * inv_freq
    ang = jnp.concatenate([ang_r, ang_r, ang_c, ang_c] * 2, axis=-1)
    cos, sin = jnp.cos(ang), jnp.sin(ang)
    first = (jnp.arange(LANES) % 32) < 16
    sin_a = jnp.where(first, -sin, 0.0)
    sin_b = jnp.where(first, 0.0, sin)
    ctx_rows = lambda v: jnp.full((n_ctx, LANES), v, F32)
    return jnp.stack([jnp.concatenate([ctx_rows(1.0), cos]), jnp.concatenate([ctx_rows(0.0), sin_a]),
                      jnp.concatenate([ctx_rows(0.0), sin_b])])


def kernel(x, c, ctx, c_ctx, w_ada, b_ada, w_in, da_lambda, da_norm, ml_conv_w, ml_conv_b, ml_ib, ml_fb,
           ml_norm, gl_wa, gl_ba, gl_norm, w_out, ln_mix_g, ln_mix_b, ln_ffn_g, ln_ffn_b,
           moe_wg, moe_we, moe_w1, moe_w3, moe_w2):
    b, s, d = x.shape
    n_ctx = ctx.shape[1]
    n = n_ctx + s
    depth = w_in.shape[0]
    assert n_ctx % TM == 0 and s % (TM * ATTN_NQ) == 0 and d % LANES == 0 and b % SCAN_NB == 0
    dn_alpha = (2 * depth) ** 0.25

    rows = -(-(b + 1) // SUBLANES) * SUBLANES
    cc = jnp.concatenate([c, c_ctx[None, :], jnp.zeros((rows - b - 1, d), F32)], axis=0)
    mod = _ada(cc, w_ada, b_ada).reshape(depth, rows, 6, d)

    xc = jnp.concatenate([ctx, x], axis=1)
    rope = _rope_tables(n_ctx, s)
    e64, e32 = _expand_matrix(64), _expand_matrix(32)

    for l in range(depth):
        last = l == depth - 1
        tile0 = n_ctx // TM if last else 0
        lam_init = 0.8 - 0.6 * math.exp(-0.3 * l)
        modtab = jnp.stack([jnp.broadcast_to(mod[l, b], (b, 6, d)), mod[l, :b]], axis=1)

        wi = w_in[l]
        o = np.cumsum([0, 512, 512, 512, 512, 256, 256, 8, 8, 128, 128, 256, 256, 32])
        misc_w = jnp.concatenate([wi[:, o[6]:o[8]], wi[:, o[12]:o[13]],
                                  jnp.zeros((d, LANES - 16 - 2 * GL_RANK), F32)], axis=1)
        w_in_p = jnp.concatenate([wi[:, :o[6]], wi[:, o[8]:o[12]], misc_w], axis=1).astype(BF16)

        (qa, ka, va, mq, mk, mv, mo, gq, gk, gv, gr, misc) = _in_proj(
            xc, modtab, w_in_p, rope, ml_conv_w[l], ml_conv_b[l].reshape(1, -1), e64, e32, n_ctx)

        n_ctx_tiles = n_ctx // TM
        attend = functools.partial(_attention, qa, ka, va, da_lambda[l], da_norm[l].reshape(1, -1), lam_init)
        da_ctx = None if last else attend(0, n_ctx_tiles, n_ctx, 1)
        da_lat = attend(n_ctx_tiles, s // TM, n, ATTN_NQ)

        gate_bias = jnp.concatenate([ml_ib[l].reshape(-1), ml_fb[l].reshape(-1),
                                     jnp.zeros((LANES - 16,), F32)]).reshape(1, LANES)
        mhf, mhb = _mlstm(mq, mk, mv, misc, gate_bias, n_ctx)

        wa_p = jnp.stack([jnp.pad(_pad_heads(gl_wa[l, dd], GL_DK),
                                  ((M_GA + dd * GL_RANK, LANES - M_GA - (dd + 1) * GL_RANK), (0, 0)))
                          for dd in range(2)])
        ba_p = _pad_heads(gl_ba[l], GL_DK).reshape(2, 1, PAD_W)
        ghf, ghb = _gla(gq, gk, gv, misc, wa_p.astype(BF16), ba_p, n_ctx)

        wo = w_out[l]
        wo_p = jnp.concatenate([
            wo[:512],
            jnp.pad(wo[512:768].reshape(HEADS, 64, d), ((0, 0), (0, 64), (0, 0))).reshape(PAD_W, d),
            jnp.pad(wo[768:1024].reshape(HEADS, 64, d), ((0, 0), (0, 64), (0, 0))).reshape(PAD_W, d),
        ], axis=0).astype(BF16)
        w_route = jnp.concatenate([moe_wg[l], moe_we[l],
                                   jnp.zeros((d, LANES - MOE_GROUPS - MOE_EXPERTS), F32)], axis=1)
        w_route_top = _bf16_part(w_route)
        w_route = jnp.stack([w_route_top.astype(BF16), (w_route - w_route_top).astype(BF16)])
        ml_g = _pad_heads(jnp.tile(ml_norm[l], HEADS), ML_DH).reshape(1, PAD_W)
        gl_g = _pad_heads(jnp.tile(gl_norm[l], HEADS), GL_DV).reshape(1, PAD_W)
        x1, f3, info, cnt, rw = _out_proj(xc, da_ctx, da_lat, mhf, mhb, mo, ghf, ghb, gr, modtab,
                                          ln_mix_g[l].reshape(1, d), ln_mix_b[l].reshape(1, d), ml_g, gl_g,
                                          wo_p, w_route, dn_alpha, n_ctx, tile0)

        n_rows = x1.shape[1]
        n_t = n_rows // TM
        a_all = b * n_rows * 2
        n_blocks = -(-(a_all + MOE_EXPERTS * (MOE_BLOCK - 1)) // MOE_BLOCK)
        tile_counts = cnt.reshape(b * n_t, SUBLANES, LANES)[:, 0, MOE_GROUPS:MOE_GROUPS + MOE_EXPERTS]
        dest8, counts, pad_start, padded, block_e, n_used = _dispatch_tables(
            info.reshape(b, n_t, SUBLANES, TM), tile_counts.astype(jnp.int32), n_blocks)
        xs = _dispatch(dest8, counts, pad_start, padded, f3, n_blocks * MOE_BLOCK)
        y_slots = _moe(xs, block_e, n_used, moe_w1, moe_w3, moe_w2, l)
        n_ctx_tiles_here = 0 if last else n_ctx // TM
        seg = lambda i: jnp.where(i >= n_ctx_tiles_here, 1, 0)
        xc = _combine(dest8, y_slots, x1, rw, modtab, ln_ffn_g[l].reshape(1, d), ln_ffn_b[l].reshape(1, d),
                      dn_alpha, seg)
    return xc
```

```python
import functools
import math

import numpy as np
import jax
import jax.numpy as jnp
from jax import lax
from jax.experimental import pallas as pl
from jax.experimental.pallas import tpu as pltpu

F32 = jnp.float32
BF16 = jnp.bfloat16
HI = lax.Precision.HIGHEST

GRID_W = 64
HEADS = 4
DA_DH = 64
ML_DH = 64
GL_DK = 32
GL_DV = 64
GL_RANK = 16
GL_NORMALIZER = 16.0
CHUNK = 64
SUB = 16
ROPE_BASE = 10000.0
MOE_GROUPS = 4
MOE_PER_GROUP = 8
MOE_EXPERTS = 32
MOE_HIDDEN = 512
LN_EPS = 1e-6

LANES = 128
SUBLANES = 8
VMEM_LIMIT = 48 * 1024 * 1024

TM = 256
MOE_BLOCK = 256
SCAN_NB = 8
ATTN_NQ = 4
PAD_W = HEADS * LANES

C_QA, C_KA, C_VA, C_MQK, C_MV, C_MO = 0, 512, 1024, 1536, 2048, 2304
C_GQ, C_GK, C_GV, C_GR, C_MISC, IN_WP = 2560, 2688, 2816, 3072, 3328, 3456
M_IG, M_FG, M_GA = 0, 8, 16


def _cparams(sem):
    return pltpu.CompilerParams(dimension_semantics=sem, vmem_limit_bytes=VMEM_LIMIT)


def _sigmoid(x):
    return 1.0 / (1.0 + jnp.exp(-x))


def _silu(x):
    return x * _sigmoid(x)


def _log_sigmoid(x):
    return jnp.minimum(x, 0.0) - jnp.log(1.0 + jnp.exp(-jnp.abs(x)))


def _ln(x):
    mu = jnp.mean(x, axis=-1, keepdims=True)
    xc = x - mu
    var = jnp.mean(xc * xc, axis=-1, keepdims=True)
    return xc * lax.rsqrt(var + LN_EPS)


def _bf16_part(x):
    bits = lax.bitcast_convert_type(x, jnp.int32) & jnp.int32(-65536)
    return lax.bitcast_convert_type(bits, F32)


def _dot(a, b):
    return jnp.dot(a, b, preferred_element_type=F32)


def _dot_hi(a, b):
    return jnp.dot(a, b, precision=HI, preferred_element_type=F32)


def _cumsum_dot(a, b, split_lhs=False):
    x = a if split_lhs else b
    x0 = _bf16_part(x)
    r1 = x - x0
    x1 = _bf16_part(r1)
    x2 = r1 - x1
    parts = [p.astype(BF16) for p in (x0, x1, x2)]
    if split_lhs:
        return _dot(parts[0], b) + (_dot(parts[1], b) + _dot(parts[2], b))
    return _dot(a, parts[0]) + (_dot(a, parts[1]) + _dot(a, parts[2]))


def _dot_nt(a, b):
    return lax.dot_general(a, b, (((1,), (1,)), ((), ())), preferred_element_type=F32)


def _dot_tn(a, b):
    return lax.dot_general(a, b, (((0,), (0,)), ((), ())), preferred_element_type=F32)


def _ada_kernel(c_ref, w_ref, b_ref, o_ref):
    o_ref[...] = _dot_hi(_silu(c_ref[...]), w_ref[...]) + b_ref[...]


def _ada(cc, w_ada, b_ada):
    depth, d, n6 = w_ada.shape
    tn = 1536
    rows = cc.shape[0]
    return pl.pallas_call(
        _ada_kernel,
        grid=(depth, n6 // tn),
        in_specs=[
            pl.BlockSpec((rows, d), lambda l, j: (0, 0)),
            pl.BlockSpec((None, d, tn), lambda l, j: (l, 0, j)),
            pl.BlockSpec((None, 1, tn), lambda l, j: (l, 0, j)),
        ],
        out_specs=pl.BlockSpec((None, rows, tn), lambda l, j: (l, 0, j)),
        out_shape=jax.ShapeDtypeStruct((depth, rows, n6), F32),
        compiler_params=_cparams(("arbitrary", "arbitrary")),
        name="ada_mod",
    )(cc, w_ada, b_ada.reshape(depth, 1, n6))


def _in_kernel(n_ctx_tiles, x_ref, xp_ref, xn_ref, mod_ref, w_ref, rope_ref, cw_ref, cb_ref,
               e64_ref, e32_ref,
               qa_ref, ka_ref, va_ref, mq_ref, mk_ref, mv_ref, mo_ref,
               gq_ref, gk_ref, gv_ref, gr_ref, misc_ref, pext_ref):
    i = pl.program_id(1)
    n_tiles = pl.num_programs(1)
    tm = x_ref.shape[0]
    shift, scale = mod_ref[0:1, :], mod_ref[1:2, :]

    def prenorm(x):
        return (_ln(x) * (1.0 + scale) + shift).astype(BF16)

    half = tm // 2
    halves = [slice(0, half), slice(half, tm)]
    groups = ((C_QA, C_KA), (C_KA, C_VA), (C_VA, C_MQK), (C_MQK, C_MV), (C_MV, C_MO), (C_MO, C_GQ),
              (C_GQ, C_GK), (C_GK, C_GV), (C_GV, C_GR), (C_GR, C_MISC), (C_MISC, IN_WP))
    e64 = e64_ref[...]
    e32 = e32_ref[...]
    lane = lax.broadcasted_iota(jnp.int32, (half, PAD_W), 1)
    ones_col = jnp.where(lane % LANES >= ML_DH, 1.0, 0.0)
    seg_first = jnp.logical_or(i == 0, i == n_ctx_tiles)
    seg_last = jnp.logical_or(i == n_ctx_tiles - 1, i == n_tiles - 1)
    keep_prev = jnp.where(seg_first, 0.0, 1.0)
    keep_next = jnp.where(seg_last, 0.0, 1.0)

    hs = [prenorm(x_ref[r, :]) for r in halves]
    halo = jnp.concatenate([prenorm(xp_ref[...]), prenorm(xn_ref[...])], axis=0)
    parts = [{lo: _dot(h, w_ref[:, lo:hi]) for lo, hi in groups} for h in hs]
    p_halo = _dot(halo, w_ref[:, C_MQK:C_MV])
    pext_ref[0:SUBLANES, :] = p_halo[0:SUBLANES] * keep_prev
    pext_ref[SUBLANES + tm:2 * SUBLANES + tm, :] = p_halo[SUBLANES:2 * SUBLANES] * keep_next
    for r, part in zip(halves, parts):
        pext_ref[SUBLANES + r.start:SUBLANES + r.stop, :] = part[C_MQK]

    for r, part in zip(halves, parts):
        cos, sin_a, sin_b = rope_ref[0, r, :], rope_ref[1, r, :], rope_ref[2, r, :]
        for lo, out_ref, mult in ((C_QA, qa_ref, DA_DH ** -0.5 * math.log2(math.e)), (C_KA, ka_ref, 1.0)):
            for hh in range(HEADS):
                xh = part[lo][:, hh * LANES:(hh + 1) * LANES]
                rot = xh * cos + pltpu.roll(xh, LANES - 16, 1) * sin_a + pltpu.roll(xh, 16, 1) * sin_b
                out_ref[r, hh * LANES:(hh + 1) * LANES] = (rot * mult).astype(BF16)
        va_ref[r, :] = part[C_VA].astype(BF16)
        mv_ref[r, :] = (_dot(part[C_MV].astype(BF16), e64) + ones_col).astype(BF16)
        mo_ref[r, :] = _dot(part[C_MO].astype(BF16), e64).astype(BF16)
        gq_ref[r, :] = _dot((part[C_GQ] * GL_DK ** -0.5).astype(BF16), e32).astype(BF16)
        gk_ref[r, :] = _dot(part[C_GK].astype(BF16), e32).astype(BF16)
        gv_ref[r, :] = _dot(part[C_GV].astype(BF16), e64).astype(BF16)
        gr_ref[r, :] = _dot(part[C_GR].astype(BF16), e64).astype(BF16)
        misc_ref[r, :] = part[C_MISC]

    for r, part in zip(halves, parts):
        y = (cw_ref[0:1, :] * pext_ref[SUBLANES - 1 + r.start:SUBLANES - 1 + r.stop, :] + cw_ref[1:2, :] * part[C_MQK]
             + cw_ref[2:3, :] * pext_ref[SUBLANES + 1 + r.start:SUBLANES + 1 + r.stop, :] + cb_ref[...])
        y = _silu(y)
        mq_ref[r, :] = _dot((y[:, :256] * ML_DH ** -0.5).astype(BF16), e64).astype(BF16)
        mk_ref[r, :] = _dot(y[:, 256:].astype(BF16), e64).astype(BF16)


def _in_proj(xc, modtab, w_in_p, rope, conv_w, conv_b, e64, e32, n_ctx):
    b, n, d = xc.shape
    n_tiles = n // TM
    n_ctx_tiles = n_ctx // TM
    hb = TM // SUBLANES
    n_hb = n // SUBLANES
    row = lambda bb, i: (bb, i, 0)
    const2 = lambda bb, i: (0, 0)
    outs = [jax.ShapeDtypeStruct((b, n, PAD_W), BF16)] * 11 + [jax.ShapeDtypeStruct((b, n, LANES), F32)]
    out_specs = [pl.BlockSpec((None, TM, PAD_W), row)] * 11 + [pl.BlockSpec((None, TM, LANES), row)]
    return pl.pallas_call(
        functools.partial(_in_kernel, n_ctx_tiles),
        grid=(b, n_tiles),
        in_specs=[
            pl.BlockSpec((None, TM, d), row),
            pl.BlockSpec((None, SUBLANES, d), lambda bb, i: (bb, jnp.maximum(i * hb - 1, 0), 0)),
            pl.BlockSpec((None, SUBLANES, d), lambda bb, i: (bb, jnp.minimum((i + 1) * hb, n_hb - 1), 0)),
            pl.BlockSpec((None, None, 6, d), lambda bb, i: (bb, jnp.where(i >= n_ctx_tiles, 1, 0), 0, 0)),
            pl.BlockSpec((d, IN_WP), const2),
            pl.BlockSpec((3, TM, LANES), lambda bb, i: (0, i, 0)),
            pl.BlockSpec((3, 512), const2),
            pl.BlockSpec((1, 512), const2),
            pl.BlockSpec((256, PAD_W), const2),
            pl.BlockSpec((128, PAD_W), const2),
        ],
        out_specs=out_specs,
        out_shape=outs,
        scratch_shapes=[pltpu.VMEM((TM + 2 * SUBLANES, 512), F32)],
        compiler_params=_cparams(("parallel", "arbitrary")),
        name="in_proj",
    )(xc, xc, xc, modtab, w_in_p, rope, conv_w, conv_b, e64, e32)


def _attn_kernel(lam_init, n_q, *refs):
    q_refs = refs[:n_q]
    k_ref, v_ref, lamp_ref, g_ref, o_ref = refs[n_q:]
    lp = lamp_ref[...]
    lam = (jnp.exp(jnp.sum(lp[0:1] * lp[1:2], axis=1, keepdims=True))
           - jnp.exp(jnp.sum(lp[2:3] * lp[3:4], axis=1, keepdims=True)) + lam_init)
    k = k_ref[...]
    v = jnp.concatenate([v_ref[...], jnp.ones(v_ref.shape, BF16)], axis=1)
    tq = q_refs[0].shape[0]
    lane = lax.broadcasted_iota(jnp.int32, (tq, LANES), 1)
    scores = []
    for q_ref in q_refs:
        q = q_ref[...]
        zero = jnp.zeros_like(q)
        scores.append((_dot_nt(jnp.where(lane < DA_DH, q, zero), k), _dot_nt(jnp.where(lane >= DA_DH, q, zero), k)))
    for t, (s1, s2) in enumerate(scores):
        p1 = _dot(jnp.exp2(s1 - jnp.max(s1, axis=1, keepdims=True)).astype(BF16), v)
        p2 = _dot(jnp.exp2(s2 - jnp.max(s2, axis=1, keepdims=True)).astype(BF16), v)
        o = p1[:, 0:LANES] / p1[:, LANES:2 * LANES] - p2[:, 0:LANES] * (lam / p2[:, LANES:2 * LANES])
        y = o * lax.rsqrt(jnp.mean(o * o, axis=1, keepdims=True) + LN_EPS) * g_ref[...]
        o_ref[t * tq:(t + 1) * tq, :] = (y * (1.0 - lam_init)).astype(BF16)


def _attention(qa, ka, va, lam_p, norm_g, lam_init, tile0, n_tiles, n_keys, n_q):
    b = qa.shape[0]
    tq = TM
    q_spec = lambda t: pl.BlockSpec((None, tq, LANES), lambda bb, h, i: (bb, tile0 + i * n_q + t, h))
    kv_spec = pl.BlockSpec((None, n_keys, LANES), lambda bb, h, i: (bb, 0, h))
    return pl.pallas_call(
        functools.partial(_attn_kernel, lam_init, n_q),
        grid=(b, HEADS, n_tiles // n_q),
        in_specs=[q_spec(t) for t in range(n_q)] + [
            kv_spec, kv_spec,
            pl.BlockSpec((4, DA_DH), lambda bb, h, i: (0, 0)),
            pl.BlockSpec((1, LANES), lambda bb, h, i: (0, 0)),
        ],
        out_specs=pl.BlockSpec((None, n_q * tq, LANES), lambda bb, h, i: (bb, i, h)),
        out_shape=jax.ShapeDtypeStruct((b, n_tiles * tq, PAD_W), BF16),
        compiler_params=_cparams(("parallel", "parallel", "arbitrary")),
        name="diff_attn",
    )(*([qa] * n_q), ka, va, lam_p, norm_g)


def _chunk_maps(n_ctx_chunks, n_chunks):
    def fwd(bb, t):
        return (bb, t, 0)

    def bwd(bb, t):
        return (bb, jnp.where(t < n_ctx_chunks, n_ctx_chunks - 1 - t, n_chunks - 1 + n_ctx_chunks - t), 0)

    return fwd, bwd


def _tri(direction):
    l = lax.broadcasted_iota(jnp.int32, (CHUNK, CHUNK), 0)
    s = lax.broadcasted_iota(jnp.int32, (CHUNK, CHUNK), 1)
    return (s <= l) if direction == 0 else (s >= l)


def _mlstm_kernel(qf_ref, kf_ref, vf_ref, gf_ref, qb_ref, kb_ref, vb_ref, gb_ref, bias_ref, sel_ref,
                  hf_ref, hb_ref, c_ref, m_ref):
    t = pl.program_id(1)

    @pl.when(t == 0)
    def _():
        c_ref[...] = jnp.zeros_like(c_ref)
        m_ref[...] = jnp.zeros_like(m_ref)

    lane = lax.broadcasted_iota(jnp.int32, (CHUNK, LANES), 1)
    is_fg = jnp.logical_and(lane >= M_FG, lane < M_GA)
    dirs = ((qf_ref, kf_ref, vf_ref, gf_ref, hf_ref), (qb_ref, kb_ref, vb_ref, gb_ref, hb_ref))
    n_b = qf_ref.shape[0]
    masks = [_tri(d) for d in range(2)]
    units = [(d, nb, h) for d in range(2) for nb in range(n_b) for h in range(HEADS)]

    pre = {}
    for d in range(2):
        tri = jnp.where(masks[d], 1.0, 0.0).astype(BF16)
        for nb in range(n_b):
            g = dirs[d][3][nb] + bias_ref[...]
            g = jnp.where(is_fg, _log_sigmoid(g), g)
            cum = _cumsum_dot(tri, g)
            pre[d, nb] = (_cumsum_dot(cum, sel_ref[d, 0], split_lhs=True),
                          _cumsum_dot(g, sel_ref[d, 1], split_lhs=True), g.T, cum.T)

    qk, qc = {}, {}
    for u in units:
        d, nb, h = u
        hs = slice(h * LANES, (h + 1) * LANES)
        q = dirs[d][0][nb, :, hs]
        qk[u] = _dot_nt(q, dirs[d][1][nb, :, hs])
        qc[u] = _dot(q, c_ref[nb, d, h].astype(BF16))

    gate = {}
    for u in units:
        d, nb, h = u
        hs = slice(h * LANES, (h + 1) * LANES)
        b_all, i_all, g_t, cum_t = pre[d, nb]
        ci, cf = M_IG + HEADS * d + h, M_FG + HEADS * d + h
        last = CHUNK - 1 if d == 0 else 0
        b_rep = b_all[:, hs]
        i_rep = i_all[:, hs]
        row = g_t[ci:ci + 1, :] - cum_t[cf:cf + 1, :]
        m_prev = m_ref[nb, d, h][0:1, :]
        log_d = jnp.where(masks[d], b_rep[:, 0:CHUNK] + row, -jnp.inf)
        log_a = b_rep + m_prev
        m_t = jnp.maximum(log_a, jnp.max(log_d, axis=1, keepdims=True))
        m_new = m_t[last:last + 1, :]
        b_last = b_rep[last:last + 1, :]
        gate[u] = (jnp.exp(log_d - m_t[:, 0:CHUNK]), jnp.exp(log_a - m_t), jnp.exp(-m_t),
                   jnp.exp(b_last - b_rep + i_rep - m_new), jnp.exp(b_last + m_prev - m_new), m_new)

    sv, kv = {}, {}
    for u in units:
        d, nb, h = u
        hs = slice(h * LANES, (h + 1) * LANES)
        v = dirs[d][2][nb, :, hs]
        dmat, _, _, w, _, _ = gate[u]
        sv[u] = _dot((qk[u] * dmat).astype(BF16), v)
        kv[u] = _dot_tn((dirs[d][1][nb, :, hs].astype(F32) * w).astype(BF16), v)

    for u in units:
        d, nb, h = u
        hs = slice(h * LANES, (h + 1) * LANES)
        _, a, floor, _, decay, m_new = gate[u]
        num = a * qc[u] + sv[u]
        qn = pltpu.roll(num, ML_DH, 1)
        den = jnp.maximum(jnp.abs(qn), floor)
        dirs[d][4][nb, :, hs] = jnp.where(lane < ML_DH, num / den, 0.0).astype(BF16)
        c_ref[nb, d, h] = decay * c_ref[nb, d, h] + kv[u]
        m_ref[nb, d, h] = jnp.broadcast_to(m_new, (SUBLANES, LANES))


def _gate_select_matrices():
    sel = np.zeros((2, 2, LANES, PAD_W), np.float32)
    for d in range(2):
        for h in range(HEADS):
            sel[d, 0, M_FG + HEADS * d + h, h * LANES:(h + 1) * LANES] = 1.0
            sel[d, 1, M_IG + HEADS * d + h, h * LANES:(h + 1) * LANES] = 1.0
    return jnp.asarray(sel, BF16)


def _mlstm(mq, mk, mv, misc, gate_bias, n_ctx):
    b, n, _ = mq.shape
    n_chunks = n // CHUNK
    fwd, bwd = _chunk_maps(n_ctx // CHUNK, n_chunks)
    wide = lambda m: pl.BlockSpec((SCAN_NB, CHUNK, PAD_W), m)
    thin = lambda m: pl.BlockSpec((SCAN_NB, CHUNK, LANES), m)
    return pl.pallas_call(
        _mlstm_kernel,
        grid=(b // SCAN_NB, n_chunks),
        in_specs=[wide(fwd), wide(fwd), wide(fwd), thin(fwd), wide(bwd), wide(bwd), wide(bwd), thin(bwd),
                  pl.BlockSpec((1, LANES), lambda bb, t: (0, 0)),
                  pl.BlockSpec((2, 2, LANES, PAD_W), lambda bb, t: (0, 0, 0, 0))],
        out_specs=[wide(fwd), wide(bwd)],
        out_shape=[jax.ShapeDtypeStruct((b, n, PAD_W), BF16)] * 2,
        scratch_shapes=[pltpu.VMEM((SCAN_NB, 2, HEADS, LANES, LANES), F32),
                        pltpu.VMEM((SCAN_NB, 2, HEADS, SUBLANES, LANES), F32)],
        compiler_params=_cparams(("parallel", "arbitrary")),
        name="mlstm_scan",
    )(mq, mk, mv, misc, mq, mk, mv, misc, gate_bias, _gate_select_matrices())


def _gla_kernel(qf_ref, kf_ref, vf_ref, gf_ref, qb_ref, kb_ref, vb_ref, gb_ref, wa_ref, ba_ref,
                hf_ref, hb_ref, st_ref):
    t = pl.program_id(1)

    @pl.when(t == 0)
    def _():
        st_ref[...] = jnp.zeros_like(st_ref)

    n_sub = CHUNK // SUB
    row_id = lax.broadcasted_iota(jnp.int32, (CHUNK, LANES), 0)
    row_blk = row_id // SUB
    col_id = lax.broadcasted_iota(jnp.int32, (CHUNK, LANES), 1)
    rows_w = lax.broadcasted_iota(jnp.int32, (CHUNK, PAD_W), 0)
    dirs = ((qf_ref, kf_ref, vf_ref, gf_ref, hf_ref), (qb_ref, kb_ref, vb_ref, gb_ref, hb_ref))
    n_b = qf_ref.shape[0]
    pairs = [(d, nb) for d in range(2) for nb in range(n_b)]
    units = [(d, nb, h) for d, nb in pairs for h in range(HEADS)]
    tris = [jnp.where(_tri(d), 1.0, 0.0).astype(BF16) for d in range(2)]
    causal = [col_id <= row_id, jnp.logical_and(col_id >= row_id, col_id < CHUNK)]

    las = {p: _log_sigmoid(_dot(dirs[p[0]][3][p[1]].astype(BF16), wa_ref[p[0]]) + ba_ref[p[0]])
           * (1.0 / GL_NORMALIZER) for p in pairs}
    cums = {p: _cumsum_dot(tris[p[0]], las[p]) for p in pairs}

    pre = {}
    for p in pairs:
        d, nb = p
        cum = cums[p]
        excl = cum - las[p]
        last = CHUNK - 1 if d == 0 else 0
        betas = [excl[r:r + 1, :] for r in (range(0, CHUNK, SUB) if d == 0 else range(SUB - 1, CHUNK, SUB))]
        beta_rows = jnp.concatenate([jnp.broadcast_to(bt, (SUB, PAD_W)) for bt in betas], axis=0)
        q = dirs[d][0][nb].astype(F32)
        k = dirs[d][1][nb].astype(F32)
        q_loc = (q * jnp.exp(cum - beta_rows)).astype(BF16)
        q_abs = (q * jnp.exp(cum)).astype(BF16)
        b_last = cum[last:last + 1, :]
        k_end = (k * jnp.exp(b_last - cum)).astype(BF16)
        k_subs = []
        for I in range(n_sub):
            seen = (rows_w < (I + 1) * SUB) if d == 0 else (rows_w >= I * SUB)
            k_subs.append((k * jnp.exp(jnp.where(seen, betas[I] - cum, -jnp.inf))).astype(BF16))
        pre[p] = (q_loc, q_abs, b_last, k_end, k_subs)

    scores, inter, kv = {}, {}, {}
    for u in units:
        d, nb, h = u
        hs = slice(h * LANES, (h + 1) * LANES)
        q_loc, q_abs, _, k_end, k_subs = pre[d, nb]
        kcat = jnp.concatenate([k_subs[I][:, hs] for I in range(n_sub)], axis=0)
        scores[u] = _dot_nt(q_loc[:, hs], kcat)
        inter[u] = _dot_nt(q_abs[:, hs], st_ref[nb, d, h].astype(BF16))
        kv[u] = _dot_tn(dirs[d][2][nb, :, hs], k_end[:, hs])

    att = {}
    for u in units:
        r = scores[u]
        a = None
        for j in range(n_sub // 2):
            pair = r[:, j * LANES:(j + 1) * LANES]
            for I, part in ((2 * j, pair), (2 * j + 1, pltpu.roll(pair, CHUNK, 1))):
                a = part if a is None else jnp.where(row_blk == I, part, a)
        att[u] = jnp.where(causal[u[0]], a, 0.0)[:, 0:CHUNK].astype(BF16)

    av = {u: _dot(att[u], dirs[u[0]][2][u[1], :, u[2] * LANES:(u[2] + 1) * LANES]) for u in units}
    for u in units:
        d, nb, h = u
        hs = slice(h * LANES, (h + 1) * LANES)
        dirs[d][4][nb, :, hs] = (inter[u] + av[u]).astype(BF16)
        st_ref[nb, d, h] = st_ref[nb, d, h] * jnp.exp(pre[d, nb][2][:, hs]) + kv[u]


def _gla(gq, gk, gv, misc, wa_p, ba_p, n_ctx):
    b, n, _ = gq.shape
    n_chunks = n // CHUNK
    fwd, bwd = _chunk_maps(n_ctx // CHUNK, n_chunks)
    wide = lambda m: pl.BlockSpec((SCAN_NB, CHUNK, PAD_W), m)
    thin = lambda m: pl.BlockSpec((SCAN_NB, CHUNK, LANES), m)
    return pl.pallas_call(
        _gla_kernel,
        grid=(b // SCAN_NB, n_chunks),
        in_specs=[wide(fwd), wide(fwd), wide(fwd), thin(fwd), wide(bwd), wide(bwd), wide(bwd), thin(bwd),
                  pl.BlockSpec((2, LANES, PAD_W), lambda bb, t: (0, 0, 0)),
                  pl.BlockSpec((2, 1, PAD_W), lambda bb, t: (0, 0, 0))],
        out_specs=[wide(fwd), wide(bwd)],
        out_shape=[jax.ShapeDtypeStruct((b, n, PAD_W), BF16)] * 2,
        scratch_shapes=[pltpu.VMEM((SCAN_NB, 2, HEADS, LANES, LANES), F32)],
        compiler_params=_cparams(("parallel", "arbitrary")),
        name="gla_scan",
    )(gq, gk, gv, misc, gq, gk, gv, misc, wa_p, ba_p)


def _head_rms_padded(x, width):
    parts = []
    for h in range(HEADS):
        xh = x[:, h * LANES:(h + 1) * LANES]
        ms = jnp.sum(xh * xh, axis=1, keepdims=True) * (1.0 / width)
        parts.append(xh * lax.rsqrt(ms + LN_EPS))
    return jnp.concatenate(parts, axis=1)


def _out_kernel(dn_alpha, n_ctx_tiles, x_ref, dac_ref, dal_ref, mhf_ref, mhb_ref, mo_ref, ghf_ref, ghb_ref, gr_ref,
                mod_ref, lng_ref, lnb_ref, mlg_ref, glg_ref, wo_ref, wr_ref,
                x1_ref, f_ref, info_ref, cnt_ref, rw_ref):
    tm = x_ref.shape[0]
    half = tm // 2
    halves = [slice(0, half), slice(half, tm)]
    gate1, shift2, scale2 = mod_ref[2:3, :], mod_ref[3:4, :], mod_ref[4:5, :]
    is_ctx = pl.program_id(1) < n_ctx_tiles
    lane = lax.broadcasted_iota(jnp.int32, (half, LANES), 1)
    neg = -jnp.inf
    big = jnp.int32(LANES)

    def first_argmax(vals, vmax):
        return jnp.min(jnp.where(vals == vmax, lane, big), axis=1, keepdims=True)

    mixed = []
    for r in halves:
        ml = (_head_rms_padded(mhf_ref[r, :].astype(F32) + mhb_ref[r, :].astype(F32), ML_DH) * mlg_ref[...]
              * _sigmoid(mo_ref[r, :].astype(F32)))
        gl = (_head_rms_padded(ghf_ref[r, :].astype(F32) + ghb_ref[r, :].astype(F32), GL_DV) * glg_ref[...]
              * _silu(gr_ref[r, :].astype(F32)))
        mixed.append((jnp.where(is_ctx, dac_ref[r, :], dal_ref[r, :]), ml.astype(BF16), gl.astype(BF16)))
    ys = [_dot(da, wo_ref[0:PAD_W, :]) + _dot(ml, wo_ref[PAD_W:2 * PAD_W, :])
          + _dot(gl, wo_ref[2 * PAD_W:3 * PAD_W, :]) for da, ml, gl in mixed]
    fs = []
    for r, y in zip(halves, ys):
        x1 = _ln(dn_alpha * x_ref[r, :] + gate1 * y) * lng_ref[...] + lnb_ref[...]
        x1_ref[r, :] = x1
        f = _ln(x1) * (1.0 + scale2) + shift2
        for s in range(SUBLANES):
            f_ref[pl.ds(r.start * SUBLANES + s, half, stride=SUBLANES), :] = f[:, s * LANES:(s + 1) * LANES]
        fs.append(f)

    all_logits = []
    for f in fs:
        f_top = _bf16_part(f)
        f_hi, f_lo = f_top.astype(BF16), (f - f_top).astype(BF16)
        all_logits.append(_dot(f_hi, wr_ref[0]) + (_dot(f_lo, wr_ref[0]) + _dot(f_hi, wr_ref[1])
                                                   + _dot(f_lo, wr_ref[1])))
    routed = []
    for r, logits in zip(halves, all_logits):
        gl_ = jnp.where(lane < MOE_GROUPS, logits, neg)
        gmax = jnp.max(gl_, axis=1, keepdims=True)
        g_idx = first_argmax(gl_, gmax)
        g_gate = 1.0 / jnp.sum(jnp.exp(gl_ - gmax), axis=1, keepdims=True)
        in_group = jnp.logical_and(lane >= MOE_GROUPS, (lane - MOE_GROUPS) // MOE_PER_GROUP == g_idx)
        el = jnp.where(in_group, logits, neg)
        emax = jnp.max(el, axis=1, keepdims=True)
        e0 = first_argmax(el, emax)
        el2 = jnp.where(lane == e0, neg, el)
        emax2 = jnp.max(el2, axis=1, keepdims=True)
        e1 = first_argmax(el2, emax2)
        esum = jnp.sum(jnp.exp(el - emax), axis=1, keepdims=True)
        p0 = 1.0 / esum
        p1 = jnp.exp(emax2 - emax) / esum
        w0 = g_gate * p0 / (p0 + p1)
        w1 = g_gate * p1 / (p0 + p1)
        rw_ref[r, :] = jnp.where(lane == 0, w0, jnp.where(lane == 1, w1, 0.0))
        routed.append((e0, e1))

    e0 = jnp.concatenate([e[0] for e in routed], axis=0)
    e1 = jnp.concatenate([e[1] for e in routed], axis=0)
    lane = lax.broadcasted_iota(jnp.int32, (tm, LANES), 1)
    oh0 = jnp.where(lane == e0, 1.0, 0.0)
    oh1 = jnp.where(lane == e1, 1.0, 0.0)
    both = (oh0 + oh1).astype(BF16)
    r_id = lax.broadcasted_iota(jnp.int32, (tm, tm), 0)
    c_id = lax.broadcasted_iota(jnp.int32, (tm, tm), 1)
    earlier = jnp.where(c_id < r_id, 1.0, 0.0).astype(BF16)
    before = _dot(earlier, both)
    rank0 = jnp.sum(before * oh0, axis=1, keepdims=True)
    rank1 = jnp.sum(before * oh1, axis=1, keepdims=True)
    cnt_ref[...] = _dot(jnp.ones((SUBLANES, tm), BF16), both)
    info = jnp.where(lane == 0, (e0 - MOE_GROUPS).astype(F32),
                     jnp.where(lane == 1, (e1 - MOE_GROUPS).astype(F32),
                               jnp.where(lane == 2, rank0, jnp.where(lane == 3, rank1, 0.0))))
    info_ref[...] = info.T[0:SUBLANES, :]


def _out_proj(xc, da_ctx, da_lat, mhf, mhb, mo, ghf, ghb, gr, modtab, ln_g, ln_b, ml_g, gl_g, wo_p, w_route,
              dn_alpha, n_ctx, tile0):
    b, n, d = xc.shape
    n_ctx_tiles = n_ctx // TM
    n_t = n // TM - tile0
    out_ctx_tiles = n_ctx_tiles - tile0
    row = lambda bb, i: (bb, i + tile0, 0)
    orow = lambda bb, i: (bb, i, 0)
    const2 = lambda bb, i: (0, 0)
    wide = pl.BlockSpec((None, TM, PAD_W), row)
    n_out = n_t * TM
    if da_ctx is None:
        da_ctx = da_lat
    ctx_rows = lambda bb, i: (bb, jnp.minimum(i, max(out_ctx_tiles - 1, 0)), 0)
    lat_rows = lambda bb, i: (bb, jnp.maximum(i - out_ctx_tiles, 0), 0)
    return pl.pallas_call(
        functools.partial(_out_kernel, dn_alpha, out_ctx_tiles),
        grid=(b, n_t),
        in_specs=[
            pl.BlockSpec((None, TM, d), row),
            pl.BlockSpec((None, TM, PAD_W), ctx_rows), pl.BlockSpec((None, TM, PAD_W), lat_rows),
            wide, wide, wide, wide, wide, wide,
            pl.BlockSpec((None, None, 6, d), lambda bb, i: (bb, jnp.where(i + tile0 >= n_ctx_tiles, 1, 0), 0, 0)),
            pl.BlockSpec((1, d), const2), pl.BlockSpec((1, d), const2),
            pl.BlockSpec((1, PAD_W), const2), pl.BlockSpec((1, PAD_W), const2),
            pl.BlockSpec((3 * PAD_W, d), const2),
            pl.BlockSpec((2, d, LANES), lambda bb, i: (0, 0, 0)),
        ],
        out_specs=[pl.BlockSpec((None, TM, d), orow),
                   pl.BlockSpec((None, TM * SUBLANES, LANES), orow),
                   pl.BlockSpec((None, SUBLANES, TM), orow),
                   pl.BlockSpec((None, SUBLANES, LANES), orow),
                   pl.BlockSpec((None, TM, LANES), orow)],
        out_shape=[jax.ShapeDtypeStruct((b, n_out, d), F32),
                   jax.ShapeDtypeStruct((b, n_out * SUBLANES, LANES), F32),
                   jax.ShapeDtypeStruct((b, n_t * SUBLANES, TM), F32),
                   jax.ShapeDtypeStruct((b, n_t * SUBLANES, LANES), F32),
                   jax.ShapeDtypeStruct((b, n_out, LANES), F32)],
        compiler_params=_cparams(("parallel", "arbitrary")),
        name="out_proj",
    )(xc, da_ctx, da_lat, mhf, mhb, mo, ghf, ghb, gr, modtab, ln_g, ln_b, ml_g, gl_g, wo_p, w_route)


def _tile_copy(src, src_tile, dst, dst_tile, sem):
    aligned = lambda t: t if isinstance(t, int) else pl.multiple_of(t, SUBLANES)
    return pltpu.make_async_copy(src.at[pl.ds(aligned(src_tile), SUBLANES), :],
                                 dst.at[pl.ds(aligned(dst_tile), SUBLANES), :], sem)


def _rows_from_tiles(ref, n_rows):
    return jnp.concatenate([ref[pl.ds(s, n_rows, stride=SUBLANES), :] for s in range(SUBLANES)], axis=1)


def _disp_kernel(dest_ref, cnt_ref, start_ref, padded_ref, f_ref, xs_hbm, zero_ref, sem, zsem, bsem):
    bb = pl.program_id(0)
    i = pl.program_id(1)
    tm = f_ref.shape[0] // SUBLANES
    n_tok = pl.num_programs(0) * pl.num_programs(1) * tm
    base = (bb * pl.num_programs(1) + i) * tm

    def scatter(start):
        def body(r, carry):
            for kk in range(2):
                cp = _tile_copy(f_ref, r * SUBLANES, xs_hbm, dest_ref[kk * n_tok + base + r], sem)
                if start:
                    cp.start(priority=kk)
                else:
                    cp.wait()
            return carry
        lax.fori_loop(0, tm, body, 0, unroll=8)

    scatter(True)
    scatter(False)

    @pl.when(jnp.logical_and(bb == pl.num_programs(0) - 1, i == pl.num_programs(1) - 1))
    def _():
        zero_ref[...] = jnp.zeros_like(zero_ref)
        n_slots = xs_hbm.shape[0] // SUBLANES
        last = MOE_EXPERTS - 1

        blk_tiles = zero_ref.shape[0]

        def fill(start):
            def row(r, c2):
                cp = _tile_copy(zero_ref, 0, xs_hbm, r * SUBLANES, zsem)
                if start:
                    cp.start()
                else:
                    cp.wait()
                return c2

            def block(j, c2):
                cp = pltpu.make_async_copy(zero_ref, xs_hbm.at[pl.ds(pl.multiple_of(j * blk_tiles, blk_tiles),
                                                                     blk_tiles), :], bsem)
                if start:
                    cp.start()
                else:
                    cp.wait()
                return c2

            def per_expert(e, carry):
                return lax.fori_loop(start_ref[e] + cnt_ref[e], start_ref[e] + padded_ref[e], row, carry)
            lax.fori_loop(0, MOE_EXPERTS, per_expert, 0)
            used_blocks = (start_ref[last] + padded_ref[last]) // (blk_tiles // SUBLANES)
            lax.fori_loop(used_blocks, n_slots * SUBLANES // blk_tiles, block, 0)

        fill(True)
        fill(False)


def _dispatch(dest8, counts, pad_start, padded, f3, n_slots):
    b, rows, _ = f3.shape
    n_t = rows // (TM * SUBLANES)
    return pl.pallas_call(
        _disp_kernel,
        grid_spec=pltpu.PrefetchScalarGridSpec(
            num_scalar_prefetch=4,
            grid=(b, n_t),
            in_specs=[pl.BlockSpec((TM * SUBLANES, LANES), lambda bb, i, *_: (bb * n_t + i, 0))],
            out_specs=pl.BlockSpec(memory_space=pl.ANY),
            scratch_shapes=[pltpu.VMEM((MOE_BLOCK * SUBLANES, LANES), F32), pltpu.SemaphoreType.DMA(()),
                            pltpu.SemaphoreType.DMA(()), pltpu.SemaphoreType.DMA(())],
        ),
        out_shape=jax.ShapeDtypeStruct((n_slots * SUBLANES, LANES), F32),
        compiler_params=_cparams(("arbitrary", "arbitrary")),
        name="moe_dispatch",
    )(dest8, counts, pad_start, padded, f3.reshape(b * rows, LANES))


def _moe_kernel(be_ref, nu_ref, xs_ref, w1_ref, w3_ref, w2_ref, y_ref, w1b, w3b, w2b):
    j = pl.program_id(0)
    n_used = nu_ref[0]
    blk = xs_ref.shape[0] // SUBLANES

    @pl.when(jnp.logical_and(j < n_used, jnp.logical_or(j == 0, be_ref[j] != be_ref[jnp.maximum(j - 1, 0)])))
    def _():
        w1b[...] = w1_ref[...].astype(BF16)
        w3b[...] = w3_ref[...].astype(BF16)
        w2b[...] = w2_ref[...].astype(BF16)

    @pl.when(j < n_used)
    def _():
        wide = 2 * LANES
        h1 = h3 = None
        for c in range(w1b.shape[0] // wide):
            xc = jnp.concatenate([xs_ref[pl.ds(2 * c + s, blk, stride=SUBLANES), :] for s in range(2)],
                                 axis=1).astype(BF16)
            d1 = _dot(xc, w1b[c * wide:(c + 1) * wide, :])
            d3 = _dot(xc, w3b[c * wide:(c + 1) * wide, :])
            h1, h3 = (d1, d3) if c == 0 else (h1 + d1, h3 + d3)
        hidden = (_silu(h1) * h3).astype(BF16)
        for c in range(w2b.shape[1] // wide):
            yc = _dot(hidden, w2b[:, c * wide:(c + 1) * wide])
            for s in range(2):
                y_ref[pl.ds(2 * c + s, blk, stride=SUBLANES), :] = yc[:, s * LANES:(s + 1) * LANES]

    @pl.when(j >= n_used)
    def _():
        y_ref[...] = jnp.zeros_like(y_ref)


def _moe(xs, block_e, n_used, w1, w3, w2, layer):
    n_blocks = block_e.shape[0]
    d, hid = w1.shape[-2:]
    wmap = lambda j, be, nu: (layer, be[j], 0, 0)
    tiles = MOE_BLOCK * SUBLANES
    return pl.pallas_call(
        _moe_kernel,
        grid_spec=pltpu.PrefetchScalarGridSpec(
            num_scalar_prefetch=2,
            grid=(n_blocks,),
            in_specs=[
                pl.BlockSpec((tiles, LANES), lambda j, be, nu: (jnp.minimum(j, jnp.maximum(nu[0] - 1, 0)), 0)),
                pl.BlockSpec((None, None, d, hid), wmap),
                pl.BlockSpec((None, None, d, hid), wmap),
                pl.BlockSpec((None, None, hid, d), wmap),
            ],
            out_specs=pl.BlockSpec((tiles, LANES), lambda j, be, nu: (j, 0)),
            scratch_shapes=[pltpu.VMEM((d, hid), BF16), pltpu.VMEM((d, hid), BF16), pltpu.VMEM((hid, d), BF16)],
        ),
        out_shape=jax.ShapeDtypeStruct((n_blocks * tiles, LANES), F32),
        compiler_params=_cparams(("arbitrary",)),
        name="moe_experts",
    )(block_e, n_used, xs, w1, w3, w2)


def _comb_kernel(dn_alpha, dest_ref, y_hbm, x1_ref, rw_ref, mod_ref, lng_ref, lnb_ref, o_ref, ybuf, sems):
    bb = pl.program_id(0)
    i = pl.program_id(1)
    tm = x1_ref.shape[0]
    n_tok = pl.num_programs(0) * pl.num_programs(1) * tm
    base = (bb * pl.num_programs(1) + i) * tm

    def gather(start):
        def body(r, carry):
            for kk in range(2):
                cp = _tile_copy(y_hbm, dest_ref[kk * n_tok + base + r], ybuf.at[kk], r * SUBLANES, sems.at[kk])
                if start:
                    cp.start(priority=kk)
                else:
                    cp.wait()
            return carry
        lax.fori_loop(0, tm, body, 0, unroll=8)

    gather(True)
    gather(False)
    rw = rw_ref[...]
    y = rw[:, 0:1] * _rows_from_tiles(ybuf.at[0], tm) + rw[:, 1:2] * _rows_from_tiles(ybuf.at[1], tm)
    gate2 = mod_ref[5:6, :]
    o_ref[...] = _ln(dn_alpha * x1_ref[...] + gate2 * y) * lng_ref[...] + lnb_ref[...]


def _combine(dest, y_slots, x1, rw, modtab, ln_g, ln_b, dn_alpha, seg_of_tile):
    b, n, d = x1.shape
    n_t = n // TM
    row = lambda bb, i, dr: (bb, i, 0)
    const2 = lambda bb, i, dr: (0, 0)
    return pl.pallas_call(
        functools.partial(_comb_kernel, dn_alpha),
        grid_spec=pltpu.PrefetchScalarGridSpec(
            num_scalar_prefetch=1,
            grid=(b, n_t),
            in_specs=[
                pl.BlockSpec(memory_space=pl.ANY),
                pl.BlockSpec((None, TM, d), row),
                pl.BlockSpec((None, TM, LANES), row),
                pl.BlockSpec((None, None, 6, d), lambda bb, i, dr: (bb, seg_of_tile(i), 0, 0)),
                pl.BlockSpec((1, d), const2), pl.BlockSpec((1, d), const2),
            ],
            out_specs=pl.BlockSpec((None, TM, d), row),
            scratch_shapes=[pltpu.VMEM((2, TM * SUBLANES, LANES), F32), pltpu.SemaphoreType.DMA((2,))],
        ),
        out_shape=jax.ShapeDtypeStruct((b, n, d), F32),
        compiler_params=_cparams(("arbitrary", "arbitrary")),
        name="moe_combine",
    )(dest, y_slots, x1, rw, modtab, ln_g, ln_b)


def _dispatch_tables(info, tile_counts, n_blocks):
    b, n_t, _, tm = info.shape
    info = info.astype(jnp.int32)
    expert = jnp.moveaxis(info[:, :, 0:2, :], 2, 0).reshape(2, b * n_t, tm)
    rank = jnp.moveaxis(info[:, :, 2:4, :], 2, 0).reshape(2, b * n_t, tm)
    counts = jnp.sum(tile_counts, axis=0)
    padded = (counts + MOE_BLOCK - 1) // MOE_BLOCK * MOE_BLOCK
    pad_end = jnp.cumsum(padded)
    pad_start = pad_end - padded
    tile_base = pad_start[None, :] + jnp.cumsum(tile_counts, axis=0) - tile_counts
    onehot = expert[..., None] == jnp.arange(MOE_EXPERTS, dtype=jnp.int32)
    dest = jnp.sum(jnp.where(onehot, tile_base[None, :, None, :], 0), axis=-1) + rank
    dest8 = (dest * SUBLANES).reshape(-1).astype(jnp.int32)
    block_start = jnp.arange(n_blocks, dtype=jnp.int32) * MOE_BLOCK
    block_e = jnp.minimum(jnp.sum(pad_end[None, :] <= block_start[:, None], axis=1), MOE_EXPERTS - 1).astype(jnp.int32)
    n_used = (pad_end[-1] // MOE_BLOCK).astype(jnp.int32).reshape(1)
    i32 = lambda v: v.astype(jnp.int32)
    return dest8, i32(counts), i32(pad_start), i32(padded), block_e, n_used


def _pad_heads(w, width):
    lead = w.shape[:-1]
    w = w.reshape(*lead, HEADS, width)
    w = jnp.pad(w, [(0, 0)] * len(lead) + [(0, 0), (0, LANES - width)])
    return w.reshape(*lead, PAD_W)


def _expand_matrix(width):
    e = np.zeros((HEADS * width, PAD_W), np.float32)
    for h in range(HEADS):
        e[h * width + np.arange(width), h * LANES + np.arange(width)] = 1.0
    return jnp.asarray(e, BF16)


def _rope_tables(n_ctx, n_lat):
    n_freq = DA_DH // 4
    inv_freq = ROPE_BASE ** (-jnp.arange(n_freq, dtype=F32) / n_freq)
    pos = jnp.arange(n_lat, dtype=jnp.int32)
    ang_r = (pos // GRID_W).astype(F32)[:, None] * inv_freq
    ang_c = (pos % GRID_W).astype(F32)[:, None] * inv_freq
    ang = jnp.concatenate([ang_r, ang_r, ang_c, ang_c] * 2, axis=-1)
    cos, sin = jnp.cos(ang), jnp.sin(ang)
    first = (jnp.arange(LANES) % 32) < 16
    sin_a = jnp.where(first, -sin, 0.0)
    sin_b = jnp.where(first, 0.0, sin)
    ctx_rows = lambda v: jnp.full((n_ctx, LANES), v, F32)
    return jnp.stack([jnp.concatenate([ctx_rows(1.0), cos]), jnp.concatenate([ctx_rows(0.0), sin_a]),
                      jnp.concatenate([ctx_rows(0.0), sin_b])])


def kernel(x, c, ctx, c_ctx, w_ada, b_ada, w_in, da_lambda, da_norm, ml_conv_w, ml_conv_b, ml_ib, ml_fb,
           ml_norm, gl_wa, gl_ba, gl_norm, w_out, ln_mix_g, ln_mix_b, ln_ffn_g, ln_ffn_b,
           moe_wg, moe_we, moe_w1, moe_w3, moe_w2):
    b, s, d = x.shape
    n_ctx = ctx.shape[1]
    n = n_ctx + s
    depth = w_in.shape[0]
    assert n_ctx % TM == 0 and s % (TM * ATTN_NQ) == 0 and d % LANES == 0 and b % SCAN_NB == 0
    dn_alpha = (2 * depth) ** 0.25

    rows = -(-(b + 1) // SUBLANES) * SUBLANES
    cc = jnp.concatenate([c, c_ctx[None, :], jnp.zeros((rows - b - 1, d), F32)], axis=0)
    mod = _ada(cc, w_ada, b_ada).reshape(depth, rows, 6, d)

    xc = jnp.concatenate([ctx, x], axis=1)
    rope = _rope_tables(n_ctx, s)
    e64, e32 = _expand_matrix(64), _expand_matrix(32)

    for l in range(depth):
        last = l == depth - 1
        tile0 = n_ctx // TM if last else 0
        lam_init = 0.8 - 0.6 * math.exp(-0.3 * l)
        modtab = jnp.stack([jnp.broadcast_to(mod[l, b], (b, 6, d)), mod[l, :b]], axis=1)

        wi = w_in[l]
        o = np.cumsum([0, 512, 512, 512, 512, 256, 256, 8, 8, 128, 128, 256, 256, 32])
        misc_w = jnp.concatenate([wi[:, o[6]:o[8]], wi[:, o[12]:o[13]],
                                  jnp.zeros((d, LANES - 16 - 2 * GL_RANK), F32)], axis=1)
        w_in_p = jnp.concatenate([wi[:, :o[6]], wi[:, o[8]:o[12]], misc_w], axis=1).astype(BF16)

        (qa, ka, va, mq, mk, mv, mo, gq, gk, gv, gr, misc) = _in_proj(
            xc, modtab, w_in_p, rope, ml_conv_w[l], ml_conv_b[l].reshape(1, -1), e64, e32, n_ctx)

        n_ctx_tiles = n_ctx // TM
        attend = functools.partial(_attention, qa, ka, va, da_lambda[l], da_norm[l].reshape(1, -1), lam_init)
        da_ctx = None if last else attend(0, n_ctx_tiles, n_ctx, 1)
        da_lat = attend(n_ctx_tiles, s // TM, n, ATTN_NQ)

        gate_bias = jnp.concatenate([ml_ib[l].reshape(-1), ml_fb[l].reshape(-1),
                                     jnp.zeros((LANES - 16,), F32)]).reshape(1, LANES)
        mhf, mhb = _mlstm(mq, mk, mv, misc, gate_bias, n_ctx)

        wa_p = jnp.stack([jnp.pad(_pad_heads(gl_wa[l, dd], GL_DK),
                                  ((M_GA + dd * GL_RANK, LANES - M_GA - (dd + 1) * GL_RANK), (0, 0)))
                          for dd in range(2)])
        ba_p = _pad_heads(gl_ba[l], GL_DK).reshape(2, 1, PAD_W)
        ghf, ghb = _gla(gq, gk, gv, misc, wa_p.astype(BF16), ba_p, n_ctx)

        wo = w_out[l]
        wo_p = jnp.concatenate([
            wo[:512],
            jnp.pad(wo[512:768].reshape(HEADS, 64, d), ((0, 0), (0, 64), (0, 0))).reshape(PAD_W, d),
            jnp.pad(wo[768:1024].reshape(HEADS, 64, d), ((0, 0), (0, 64), (0, 0))).reshape(PAD_W, d),
        ], axis=0).astype(BF16)
        w_route = jnp.concatenate([moe_wg[l], moe_we[l],
                                   jnp.zeros((d, LANES - MOE_GROUPS - MOE_EXPERTS), F32)], axis=1)
        w_route_top = _bf16_part(w_route)
        w_route = jnp.stack([w_route_top.astype(BF16), (w_route - w_route_top).astype(BF16)])
        ml_g = _pad_heads(jnp.tile(ml_norm[l], HEADS), ML_DH).reshape(1, PAD_W)
        gl_g = _pad_heads(jnp.tile(gl_norm[l], HEADS), GL_DV).reshape(1, PAD_W)
        x1, f3, info, cnt, rw = _out_proj(xc, da_ctx, da_lat, mhf, mhb, mo, ghf, ghb, gr, modtab,
                                          ln_mix_g[l].reshape(1, d), ln_mix_b[l].reshape(1, d), ml_g, gl_g,
                                          wo_p, w_route, dn_alpha, n_ctx, tile0)

        n_rows = x1.shape[1]
        n_t = n_rows // TM
        a_all = b * n_rows * 2
        n_blocks = -(-(a_all + MOE_EXPERTS * (MOE_BLOCK - 1)) // MOE_BLOCK)
        tile_counts = cnt.reshape(b * n_t, SUBLANES, LANES)[:, 0, MOE_GROUPS:MOE_GROUPS + MOE_EXPERTS]
        dest8, counts, pad_start, padded, block_e, n_used = _dispatch_tables(
            info.reshape(b, n_t, SUBLANES, TM), tile_counts.astype(jnp.int32), n_blocks)
        xs = _dispatch(dest8, counts, pad_start, padded, f3, n_blocks * MOE_BLOCK)
        y_slots = _moe(xs, block_e, n_used, moe_w1, moe_w3, moe_w2, l)
        n_ctx_tiles_here = 0 if last else n_ctx // TM
        seg = lambda i: jnp.where(i >= n_ctx_tiles_here, 1, 0)
        xc = _combine(dest8, y_slots, x1, rw, modtab, ln_ffn_g[l].reshape(1, d), ln_ffn_b[l].reshape(1, d),
                      dn_alpha, seg)
    return xc
```

```python
import functools
import math

import numpy as np
import jax
import jax.numpy as jnp
from jax import lax
from jax.experimental import pallas as pl
from jax.experimental.pallas import tpu as pltpu

F32 = jnp.float32
BF16 = jnp.bfloat16
HI = lax.Precision.HIGHEST

GRID_W = 64
HEADS = 4
DA_DH = 64
ML_DH = 64
GL_DK = 32
GL_DV = 64
GL_RANK = 16
GL_NORMALIZER = 16.0
CHUNK = 64
SUB = 16
ROPE_BASE = 10000.0
MOE_GROUPS = 4
MOE_PER_GROUP = 8
MOE_EXPERTS = 32
MOE_HIDDEN = 512
LN_EPS = 1e-6

LANES = 128
SUBLANES = 8
VMEM_LIMIT = 48 * 1024 * 1024

TM = 256
MOE_BLOCK = 512
PAD_RUN = 32
SCAN_NB = 8
ATTN_NQ = 4
PAD_W = HEADS * LANES

C_QA, C_KA, C_VA, C_MQK, C_MV, C_MO = 0, 512, 1024, 1536, 2048, 2304
C_GQ, C_GK, C_GV, C_GR, C_MISC, IN_WP = 2560, 2688, 2816, 3072, 3328, 3456
M_IG, M_FG, M_GA = 0, 8, 16


def _cparams(sem):
    return pltpu.CompilerParams(dimension_semantics=sem, vmem_limit_bytes=VMEM_LIMIT)


def _sigmoid(x):
    return 1.0 / (1.0 + jnp.exp(-x))


def _silu(x):
    return x * _sigmoid(x)


def _log_sigmoid(x):
    return jnp.minimum(x, 0.0) - jnp.log(1.0 + jnp.exp(-jnp.abs(x)))


def _ln(x):
    mu = jnp.mean(x, axis=-1, keepdims=True)
    xc = x - mu
    var = jnp.mean(xc * xc, axis=-1, keepdims=True)
    return xc * lax.rsqrt(var + LN_EPS)


def _bf16_part(x):
    bits = lax.bitcast_convert_type(x, jnp.int32) & jnp.int32(-65536)
    return lax.bitcast_convert_type(bits, F32)


def _dot(a, b):
    return jnp.dot(a, b, preferred_element_type=F32)


def _dot_hi(a, b):
    return jnp.dot(a, b, precision=HI, preferred_element_type=F32)


def _cumsum_dot(a, b, split_lhs=False):
    x = a if split_lhs else b
    x0 = _bf16_part(x)
    r1 = x - x0
    x1 = _bf16_part(r1)
    x2 = r1 - x1
    parts = [p.astype(BF16) for p in (x0, x1, x2)]
    if split_lhs:
        return _dot(parts[0], b) + (_dot(parts[1], b) + _dot(parts[2], b))
    return _dot(a, parts[0]) + (_dot(a, parts[1]) + _dot(a, parts[2]))


def _dot_nt(a, b):
    return lax.dot_general(a, b, (((1,), (1,)), ((), ())), preferred_element_type=F32)


def _dot_tn(a, b):
    return lax.dot_general(a, b, (((0,), (0,)), ((), ())), preferred_element_type=F32)


def _ada_kernel(c_ref, w_ref, b_ref, o_ref):
    o_ref[...] = _dot_hi(_silu(c_ref[...]), w_ref[...]) + b_ref[...]


def _ada(cc, w_ada, b_ada):
    depth, d, n6 = w_ada.shape
    tn = 1536
    rows = cc.shape[0]
    return pl.pallas_call(
        _ada_kernel,
        grid=(depth, n6 // tn),
        in_specs=[
            pl.BlockSpec((rows, d), lambda l, j: (0, 0)),
            pl.BlockSpec((None, d, tn), lambda l, j: (l, 0, j)),
            pl.BlockSpec((None, 1, tn), lambda l, j: (l, 0, j)),
        ],
        out_specs=pl.BlockSpec((None, rows, tn), lambda l, j: (l, 0, j)),
        out_shape=jax.ShapeDtypeStruct((depth, rows, n6), F32),
        compiler_params=_cparams(("arbitrary", "arbitrary")),
        name="ada_mod",
    )(cc, w_ada, b_ada.reshape(depth, 1, n6))


def _in_kernel(n_ctx_tiles, x_ref, xp_ref, xn_ref, mod_ref, w_ref, rope_ref, cw_ref, cb_ref,
               e64_ref, e32_ref,
               qa_ref, ka_ref, va_ref, mq_ref, mk_ref, mv_ref, mo_ref,
               gq_ref, gk_ref, gv_ref, gr_ref, misc_ref, pext_ref):
    i = pl.program_id(1)
    n_tiles = pl.num_programs(1)
    tm = x_ref.shape[0]
    shift, scale = mod_ref[0:1, :], mod_ref[1:2, :]

    def prenorm(x):
        return (_ln(x) * (1.0 + scale) + shift).astype(BF16)

    half = tm // 2
    halves = [slice(0, half), slice(half, tm)]
    groups = ((C_QA, C_KA), (C_KA, C_VA), (C_VA, C_MQK), (C_MQK, C_MV), (C_MV, C_MO), (C_MO, C_GQ),
              (C_GQ, C_GK), (C_GK, C_GV), (C_GV, C_GR), (C_GR, C_MISC), (C_MISC, IN_WP))
    e64 = e64_ref[...]
    e32 = e32_ref[...]
    lane = lax.broadcasted_iota(jnp.int32, (half, PAD_W), 1)
    ones_col = jnp.where(lane % LANES >= ML_DH, 1.0, 0.0)
    seg_first = jnp.logical_or(i == 0, i == n_ctx_tiles)
    seg_last = jnp.logical_or(i == n_ctx_tiles - 1, i == n_tiles - 1)
    keep_prev = jnp.where(seg_first, 0.0, 1.0)
    keep_next = jnp.where(seg_last, 0.0, 1.0)

    hs = [prenorm(x_ref[r, :]) for r in halves]
    halo = jnp.concatenate([prenorm(xp_ref[...]), prenorm(xn_ref[...])], axis=0)
    parts = [{lo: _dot(h, w_ref[:, lo:hi]) for lo, hi in groups} for h in hs]
    p_halo = _dot(halo, w_ref[:, C_MQK:C_MV])
    pext_ref[0:SUBLANES, :] = p_halo[0:SUBLANES] * keep_prev
    pext_ref[SUBLANES + tm:2 * SUBLANES + tm, :] = p_halo[SUBLANES:2 * SUBLANES] * keep_next
    for r, part in zip(halves, parts):
        pext_ref[SUBLANES + r.start:SUBLANES + r.stop, :] = part[C_MQK]

    for r, part in zip(halves, parts):
        cos, sin_a, sin_b = rope_ref[0, r, :], rope_ref[1, r, :], rope_ref[2, r, :]
        for lo, out_ref, mult in ((C_QA, qa_ref, DA_DH ** -0.5 * math.log2(math.e)), (C_KA, ka_ref, 1.0)):
            for hh in range(HEADS):
                xh = part[lo][:, hh * LANES:(hh + 1) * LANES]
                rot = xh * cos + pltpu.roll(xh, LANES - 16, 1) * sin_a + pltpu.roll(xh, 16, 1) * sin_b
                out_ref[r, hh * LANES:(hh + 1) * LANES] = (rot * mult).astype(BF16)
        va_ref[r, :] = part[C_VA].astype(BF16)
        mv_ref[r, :] = (_dot(part[C_MV].astype(BF16), e64) + ones_col).astype(BF16)
        mo_ref[r, :] = _dot(part[C_MO].astype(BF16), e64).astype(BF16)
        gq_ref[r, :] = _dot((part[C_GQ] * GL_DK ** -0.5).astype(BF16), e32).astype(BF16)
        gk_ref[r, :] = _dot(part[C_GK].astype(BF16), e32).astype(BF16)
        gv_ref[r, :] = _dot(part[C_GV].astype(BF16), e64).astype(BF16)
        gr_ref[r, :] = _dot(part[C_GR].astype(BF16), e64).astype(BF16)
        misc_ref[r, :] = part[C_MISC]

    for r, part in zip(halves, parts):
        y = (cw_ref[0:1, :] * pext_ref[SUBLANES - 1 + r.start:SUBLANES - 1 + r.stop, :] + cw_ref[1:2, :] * part[C_MQK]
             + cw_ref[2:3, :] * pext_ref[SUBLANES + 1 + r.start:SUBLANES + 1 + r.stop, :] + cb_ref[...])
        y = _silu(y)
        mq_ref[r, :] = _dot((y[:, :256] * ML_DH ** -0.5).astype(BF16), e64).astype(BF16)
        mk_ref[r, :] = _dot(y[:, 256:].astype(BF16), e64).astype(BF16)


def _in_proj(xc, modtab, w_in_p, rope, conv_w, conv_b, e64, e32, n_ctx):
    b, n, d = xc.shape
    n_tiles = n // TM
    n_ctx_tiles = n_ctx // TM
    hb = TM // SUBLANES
    n_hb = n // SUBLANES
    row = lambda bb, i: (bb, i, 0)
    const2 = lambda bb, i: (0, 0)
    outs = [jax.ShapeDtypeStruct((b, n, PAD_W), BF16)] * 11 + [jax.ShapeDtypeStruct((b, n, LANES), F32)]
    out_specs = [pl.BlockSpec((None, TM, PAD_W), row)] * 11 + [pl.BlockSpec((None, TM, LANES), row)]
    return pl.pallas_call(
        functools.partial(_in_kernel, n_ctx_tiles),
        grid=(b, n_tiles),
        in_specs=[
            pl.BlockSpec((None, TM, d), row),
            pl.BlockSpec((None, SUBLANES, d), lambda bb, i: (bb, jnp.maximum(i * hb - 1, 0), 0)),
            pl.BlockSpec((None, SUBLANES, d), lambda bb, i: (bb, jnp.minimum((i + 1) * hb, n_hb - 1), 0)),
            pl.BlockSpec((None, None, 6, d), lambda bb, i: (bb, jnp.where(i >= n_ctx_tiles, 1, 0), 0, 0)),
            pl.BlockSpec((d, IN_WP), const2),
            pl.BlockSpec((3, TM, LANES), lambda bb, i: (0, i, 0)),
            pl.BlockSpec((3, 512), const2),
            pl.BlockSpec((1, 512), const2),
            pl.BlockSpec((256, PAD_W), const2),
            pl.BlockSpec((128, PAD_W), const2),
        ],
        out_specs=out_specs,
        out_shape=outs,
        scratch_shapes=[pltpu.VMEM((TM + 2 * SUBLANES, 512), F32)],
        compiler_params=_cparams(("parallel", "arbitrary")),
        name="in_proj",
    )(xc, xc, xc, modtab, w_in_p, rope, conv_w, conv_b, e64, e32)


def _attn_kernel(lam_init, n_q, *refs):
    q_refs = refs[:n_q]
    k_ref, v_ref, lamp_ref, g_ref, o_ref = refs[n_q:]
    lp = lamp_ref[...]
    lam = (jnp.exp(jnp.sum(lp[0:1] * lp[1:2], axis=1, keepdims=True))
           - jnp.exp(jnp.sum(lp[2:3] * lp[3:4], axis=1, keepdims=True)) + lam_init)
    k = k_ref[...]
    v = jnp.concatenate([v_ref[...], jnp.ones(v_ref.shape, BF16)], axis=1)
    tq = q_refs[0].shape[0]
    lane = lax.broadcasted_iota(jnp.int32, (tq, LANES), 1)
    scores = []
    for q_ref in q_refs:
        q = q_ref[...]
        zero = jnp.zeros_like(q)
        scores.append((_dot_nt(jnp.where(lane < DA_DH, q, zero), k), _dot_nt(jnp.where(lane >= DA_DH, q, zero), k)))
    for t, (s1, s2) in enumerate(scores):
        p1 = _dot(jnp.exp2(s1 - jnp.max(s1, axis=1, keepdims=True)).astype(BF16), v)
        p2 = _dot(jnp.exp2(s2 - jnp.max(s2, axis=1, keepdims=True)).astype(BF16), v)
        o = p1[:, 0:LANES] / p1[:, LANES:2 * LANES] - p2[:, 0:LANES] * (lam / p2[:, LANES:2 * LANES])
        y = o * lax.rsqrt(jnp.mean(o * o, axis=1, keepdims=True) + LN_EPS) * g_ref[...]
        o_ref[t * tq:(t + 1) * tq, :] = (y * (1.0 - lam_init)).astype(BF16)


def _attention(qa, ka, va, lam_p, norm_g, lam_init, tile0, n_tiles, n_keys, n_q):
    b = qa.shape[0]
    tq = TM
    q_spec = lambda t: pl.BlockSpec((None, tq, LANES), lambda bb, h, i: (bb, tile0 + i * n_q + t, h))
    kv_spec = pl.BlockSpec((None, n_keys, LANES), lambda bb, h, i: (bb, 0, h))
    return pl.pallas_call(
        functools.partial(_attn_kernel, lam_init, n_q),
        grid=(b, HEADS, n_tiles // n_q),
        in_specs=[q_spec(t) for t in range(n_q)] + [
            kv_spec, kv_spec,
            pl.BlockSpec((4, DA_DH), lambda bb, h, i: (0, 0)),
            pl.BlockSpec((1, LANES), lambda bb, h, i: (0, 0)),
        ],
        out_specs=pl.BlockSpec((None, n_q * tq, LANES), lambda bb, h, i: (bb, i, h)),
        out_shape=jax.ShapeDtypeStruct((b, n_tiles * tq, PAD_W), BF16),
        compiler_params=_cparams(("parallel", "parallel", "arbitrary")),
        name="diff_attn",
    )(*([qa] * n_q), ka, va, lam_p, norm_g)


def _chunk_maps(n_ctx_chunks, n_chunks):
    def fwd(bb, t):
        return (bb, t, 0)

    def bwd(bb, t):
        return (bb, jnp.where(t < n_ctx_chunks, n_ctx_chunks - 1 - t, n_chunks - 1 + n_ctx_chunks - t), 0)

    return fwd, bwd


def _tri(direction):
    l = lax.broadcasted_iota(jnp.int32, (CHUNK, CHUNK), 0)
    s = lax.broadcasted_iota(jnp.int32, (CHUNK, CHUNK), 1)
    return (s <= l) if direction == 0 else (s >= l)


def _mlstm_kernel(qf_ref, kf_ref, vf_ref, gf_ref, qb_ref, kb_ref, vb_ref, gb_ref, bias_ref, sel_ref,
                  hf_ref, hb_ref, c_ref, m_ref):
    t = pl.program_id(1)

    @pl.when(t == 0)
    def _():
        c_ref[...] = jnp.zeros_like(c_ref)
        m_ref[...] = jnp.zeros_like(m_ref)

    lane = lax.broadcasted_iota(jnp.int32, (CHUNK, LANES), 1)
    is_fg = jnp.logical_and(lane >= M_FG, lane < M_GA)
    dirs = ((qf_ref, kf_ref, vf_ref, gf_ref, hf_ref), (qb_ref, kb_ref, vb_ref, gb_ref, hb_ref))
    n_b = qf_ref.shape[0]
    masks = [_tri(d) for d in range(2)]
    units = [(d, nb, h) for d in range(2) for nb in range(n_b) for h in range(HEADS)]

    pre = {}
    for d in range(2):
        tri = jnp.where(masks[d], 1.0, 0.0).astype(BF16)
        for nb in range(n_b):
            g = dirs[d][3][nb] + bias_ref[...]
            g = jnp.where(is_fg, _log_sigmoid(g), g)
            cum = _cumsum_dot(tri, g)
            pre[d, nb] = (_cumsum_dot(cum, sel_ref[d, 0], split_lhs=True),
                          _cumsum_dot(g, sel_ref[d, 1], split_lhs=True), g.T, cum.T)

    qk, qc = {}, {}
    for u in units:
        d, nb, h = u
        hs = slice(h * LANES, (h + 1) * LANES)
        q = dirs[d][0][nb, :, hs]
        qk[u] = _dot_nt(q, dirs[d][1][nb, :, hs])
        qc[u] = _dot(q, c_ref[nb, d, h].astype(BF16))

    gate = {}
    for u in units:
        d, nb, h = u
        hs = slice(h * LANES, (h + 1) * LANES)
        b_all, i_all, g_t, cum_t = pre[d, nb]
        ci, cf = M_IG + HEADS * d + h, M_FG + HEADS * d + h
        last = CHUNK - 1 if d == 0 else 0
        b_rep = b_all[:, hs]
        i_rep = i_all[:, hs]
        row = g_t[ci:ci + 1, :] - cum_t[cf:cf + 1, :]
        m_prev = m_ref[nb, d, h][0:1, :]
        log_d = jnp.where(masks[d], b_rep[:, 0:CHUNK] + row, -jnp.inf)
        log_a = b_rep + m_prev
        m_t = jnp.maximum(log_a, jnp.max(log_d, axis=1, keepdims=True))
        m_new = m_t[last:last + 1, :]
        b_last = b_rep[last:last + 1, :]
        gate[u] = (jnp.exp(log_d - m_t[:, 0:CHUNK]), jnp.exp(log_a - m_t), jnp.exp(-m_t),
                   jnp.exp(b_last - b_rep + i_rep - m_new), jnp.exp(b_last + m_prev - m_new), m_new)

    sv, kv = {}, {}
    for u in units:
        d, nb, h = u
        hs = slice(h * LANES, (h + 1) * LANES)
        v = dirs[d][2][nb, :, hs]
        dmat, _, _, w, _, _ = gate[u]
        sv[u] = _dot((qk[u] * dmat).astype(BF16), v)
        kv[u] = _dot_tn((dirs[d][1][nb, :, hs].astype(F32) * w).astype(BF16), v)

    for u in units:
        d, nb, h = u
        hs = slice(h * LANES, (h + 1) * LANES)
        _, a, floor, _, decay, m_new = gate[u]
        num = a * qc[u] + sv[u]
        qn = pltpu.roll(num, ML_DH, 1)
        den = jnp.maximum(jnp.abs(qn), floor)
        dirs[d][4][nb, :, hs] = jnp.where(lane < ML_DH, num / den, 0.0).astype(BF16)
        c_ref[nb, d, h] = decay * c_ref[nb, d, h] + kv[u]
        m_ref[nb, d, h] = jnp.broadcast_to(m_new, (SUBLANES, LANES))


def _gate_select_matrices():
    sel = np.zeros((2, 2, LANES, PAD_W), np.float32)
    for d in range(2):
        for h in range(HEADS):
            sel[d, 0, M_FG + HEADS * d + h, h * LANES:(h + 1) * LANES] = 1.0
            sel[d, 1, M_IG + HEADS * d + h, h * LANES:(h + 1) * LANES] = 1.0
    return jnp.asarray(sel, BF16)


def _mlstm(mq, mk, mv, misc, gate_bias, n_ctx):
    b, n, _ = mq.shape
    n_chunks = n // CHUNK
    fwd, bwd = _chunk_maps(n_ctx // CHUNK, n_chunks)
    wide = lambda m: pl.BlockSpec((SCAN_NB, CHUNK, PAD_W), m)
    thin = lambda m: pl.BlockSpec((SCAN_NB, CHUNK, LANES), m)
    return pl.pallas_call(
        _mlstm_kernel,
        grid=(b // SCAN_NB, n_chunks),
        in_specs=[wide(fwd), wide(fwd), wide(fwd), thin(fwd), wide(bwd), wide(bwd), wide(bwd), thin(bwd),
                  pl.BlockSpec((1, LANES), lambda bb, t: (0, 0)),
                  pl.BlockSpec((2, 2, LANES, PAD_W), lambda bb, t: (0, 0, 0, 0))],
        out_specs=[wide(fwd), wide(bwd)],
        out_shape=[jax.ShapeDtypeStruct((b, n, PAD_W), BF16)] * 2,
        scratch_shapes=[pltpu.VMEM((SCAN_NB, 2, HEADS, LANES, LANES), F32),
                        pltpu.VMEM((SCAN_NB, 2, HEADS, SUBLANES, LANES), F32)],
        compiler_params=_cparams(("parallel", "arbitrary")),
        name="mlstm_scan",
    )(mq, mk, mv, misc, mq, mk, mv, misc, gate_bias, _gate_select_matrices())


def _gla_kernel(qf_ref, kf_ref, vf_ref, gf_ref, qb_ref, kb_ref, vb_ref, gb_ref, wa_ref, ba_ref,
                hf_ref, hb_ref, st_ref):
    t = pl.program_id(1)

    @pl.when(t == 0)
    def _():
        st_ref[...] = jnp.zeros_like(st_ref)

    n_sub = CHUNK // SUB
    row_id = lax.broadcasted_iota(jnp.int32, (CHUNK, LANES), 0)
    row_blk = row_id // SUB
    col_id = lax.broadcasted_iota(jnp.int32, (CHUNK, LANES), 1)
    rows_w = lax.broadcasted_iota(jnp.int32, (CHUNK, PAD_W), 0)
    dirs = ((qf_ref, kf_ref, vf_ref, gf_ref, hf_ref), (qb_ref, kb_ref, vb_ref, gb_ref, hb_ref))
    n_b = qf_ref.shape[0]
    pairs = [(d, nb) for d in range(2) for nb in range(n_b)]
    units = [(d, nb, h) for d, nb in pairs for h in range(HEADS)]
    tris = [jnp.where(_tri(d), 1.0, 0.0).astype(BF16) for d in range(2)]
    causal = [col_id <= row_id, jnp.logical_and(col_id >= row_id, col_id < CHUNK)]

    las = {p: _log_sigmoid(_dot(dirs[p[0]][3][p[1]].astype(BF16), wa_ref[p[0]]) + ba_ref[p[0]])
           * (1.0 / GL_NORMALIZER) for p in pairs}
    cums = {p: _cumsum_dot(tris[p[0]], las[p]) for p in pairs}

    pre = {}
    for p in pairs:
        d, nb = p
        cum = cums[p]
        excl = cum - las[p]
        last = CHUNK - 1 if d == 0 else 0
        betas = [excl[r:r + 1, :] for r in (range(0, CHUNK, SUB) if d == 0 else range(SUB - 1, CHUNK, SUB))]
        beta_rows = jnp.concatenate([jnp.broadcast_to(bt, (SUB, PAD_W)) for bt in betas], axis=0)
        q = dirs[d][0][nb].astype(F32)
        k = dirs[d][1][nb].astype(F32)
        q_loc = (q * jnp.exp(cum - beta_rows)).astype(BF16)
        q_abs = (q * jnp.exp(cum)).astype(BF16)
        b_last = cum[last:last + 1, :]
        k_end = (k * jnp.exp(b_last - cum)).astype(BF16)
        k_subs = []
        for I in range(n_sub):
            seen = (rows_w < (I + 1) * SUB) if d == 0 else (rows_w >= I * SUB)
            k_subs.append((k * jnp.exp(jnp.where(seen, betas[I] - cum, -jnp.inf))).astype(BF16))
        pre[p] = (q_loc, q_abs, b_last, k_end, k_subs)

    scores, inter, kv = {}, {}, {}
    for u in units:
        d, nb, h = u
        hs = slice(h * LANES, (h + 1) * LANES)
        q_loc, q_abs, _, k_end, k_subs = pre[d, nb]
        kcat = jnp.concatenate([k_subs[I][:, hs] for I in range(n_sub)], axis=0)
        scores[u] = _dot_nt(q_loc[:, hs], kcat)
        inter[u] = _dot_nt(q_abs[:, hs], st_ref[nb, d, h].astype(BF16))
        kv[u] = _dot_tn(dirs[d][2][nb, :, hs], k_end[:, hs])

    att = {}
    for u in units:
        r = scores[u]
        a = None
        for j in range(n_sub // 2):
            pair = r[:, j * LANES:(j + 1) * LANES]
            for I, part in ((2 * j, pair), (2 * j + 1, pltpu.roll(pair, CHUNK, 1))):
                a = part if a is None else jnp.where(row_blk == I, part, a)
        att[u] = jnp.where(causal[u[0]], a, 0.0)[:, 0:CHUNK].astype(BF16)

    av = {u: _dot(att[u], dirs[u[0]][2][u[1], :, u[2] * LANES:(u[2] + 1) * LANES]) for u in units}
    for u in units:
        d, nb, h = u
        hs = slice(h * LANES, (h + 1) * LANES)
        dirs[d][4][nb, :, hs] = (inter[u] + av[u]).astype(BF16)
        st_ref[nb, d, h] = st_ref[nb, d, h] * jnp.exp(pre[d, nb][2][:, hs]) + kv[u]


def _gla(gq, gk, gv, misc, wa_p, ba_p, n_ctx):
    b, n, _ = gq.shape
    n_chunks = n // CHUNK
    fwd, bwd = _chunk_maps(n_ctx // CHUNK, n_chunks)
    wide = lambda m: pl.BlockSpec((SCAN_NB, CHUNK, PAD_W), m)
    thin = lambda m: pl.BlockSpec((SCAN_NB, CHUNK, LANES), m)
    return pl.pallas_call(
        _gla_kernel,
        grid=(b // SCAN_NB, n_chunks),
        in_specs=[wide(fwd), wide(fwd), wide(fwd), thin(fwd), wide(bwd), wide(bwd), wide(bwd), thin(bwd),
                  pl.BlockSpec((2, LANES, PAD_W), lambda bb, t: (0, 0, 0)),
                  pl.BlockSpec((2, 1, PAD_W), lambda bb, t: (0, 0, 0))],
        out_specs=[wide(fwd), wide(bwd)],
        out_shape=[jax.ShapeDtypeStruct((b, n, PAD_W), BF16)] * 2,
        scratch_shapes=[pltpu.VMEM((SCAN_NB, 2, HEADS, LANES, LANES), F32)],
        compiler_params=_cparams(("parallel", "arbitrary")),
        name="gla_scan",
    )(gq, gk, gv, misc, gq, gk, gv, misc, wa_p, ba_p)


def _head_rms_padded(x, width):
    parts = []
    for h in range(HEADS):
        xh = x[:, h * LANES:(h + 1) * LANES]
        ms = jnp.sum(xh * xh, axis=1, keepdims=True) * (1.0 / width)
        parts.append(xh * lax.rsqrt(ms + LN_EPS))
    return jnp.concatenate(parts, axis=1)


def _out_kernel(dn_alpha, n_ctx_tiles, x_ref, dac_ref, dal_ref, mhf_ref, mhb_ref, mo_ref, ghf_ref, ghb_ref, gr_ref,
                mod_ref, lng_ref, lnb_ref, mlg_ref, glg_ref, wo_ref, wr_ref,
                x1_ref, f_ref, info_ref, cnt_ref, rw_ref):
    tm = x_ref.shape[0]
    half = tm // 2
    halves = [slice(0, half), slice(half, tm)]
    gate1, shift2, scale2 = mod_ref[2:3, :], mod_ref[3:4, :], mod_ref[4:5, :]
    is_ctx = pl.program_id(1) < n_ctx_tiles
    lane = lax.broadcasted_iota(jnp.int32, (half, LANES), 1)
    neg = -jnp.inf
    big = jnp.int32(LANES)

    def first_argmax(vals, vmax):
        return jnp.min(jnp.where(vals == vmax, lane, big), axis=1, keepdims=True)

    mixed = []
    for r in halves:
        ml = (_head_rms_padded(mhf_ref[r, :].astype(F32) + mhb_ref[r, :].astype(F32), ML_DH) * mlg_ref[...]
              * _sigmoid(mo_ref[r, :].astype(F32)))
        gl = (_head_rms_padded(ghf_ref[r, :].astype(F32) + ghb_ref[r, :].astype(F32), GL_DV) * glg_ref[...]
              * _silu(gr_ref[r, :].astype(F32)))
        mixed.append((jnp.where(is_ctx, dac_ref[r, :], dal_ref[r, :]), ml.astype(BF16), gl.astype(BF16)))
    ys = [_dot(da, wo_ref[0:PAD_W, :]) + _dot(ml, wo_ref[PAD_W:2 * PAD_W, :])
          + _dot(gl, wo_ref[2 * PAD_W:3 * PAD_W, :]) for da, ml, gl in mixed]
    fs = []
    for r, y in zip(halves, ys):
        x1 = _ln(dn_alpha * x_ref[r, :] + gate1 * y) * lng_ref[...] + lnb_ref[...]
        x1_ref[r, :] = x1
        f = _ln(x1) * (1.0 + scale2) + shift2
        for s in range(SUBLANES):
            f_ref[pl.ds(r.start * SUBLANES + s, half, stride=SUBLANES), :] = f[:, s * LANES:(s + 1) * LANES]
        fs.append(f)

    all_logits = []
    for f in fs:
        f_top = _bf16_part(f)
        f_hi, f_lo = f_top.astype(BF16), (f - f_top).astype(BF16)
        all_logits.append(_dot(f_hi, wr_ref[0]) + (_dot(f_lo, wr_ref[0]) + _dot(f_hi, wr_ref[1])
                                                   + _dot(f_lo, wr_ref[1])))
    routed = []
    for r, logits in zip(halves, all_logits):
        gl_ = jnp.where(lane < MOE_GROUPS, logits, neg)
        gmax = jnp.max(gl_, axis=1, keepdims=True)
        g_idx = first_argmax(gl_, gmax)
        g_gate = 1.0 / jnp.sum(jnp.exp(gl_ - gmax), axis=1, keepdims=True)
        in_group = jnp.logical_and(lane >= MOE_GROUPS, (lane - MOE_GROUPS) // MOE_PER_GROUP == g_idx)
        el = jnp.where(in_group, logits, neg)
        emax = jnp.max(el, axis=1, keepdims=True)
        e0 = first_argmax(el, emax)
        el2 = jnp.where(lane == e0, neg, el)
        emax2 = jnp.max(el2, axis=1, keepdims=True)
        e1 = first_argmax(el2, emax2)
        esum = jnp.sum(jnp.exp(el - emax), axis=1, keepdims=True)
        p0 = 1.0 / esum
        p1 = jnp.exp(emax2 - emax) / esum
        w0 = g_gate * p0 / (p0 + p1)
        w1 = g_gate * p1 / (p0 + p1)
        rw_ref[r, :] = jnp.where(lane == 0, w0, jnp.where(lane == 1, w1, 0.0))
        routed.append((e0, e1))

    e0 = jnp.concatenate([e[0] for e in routed], axis=0)
    e1 = jnp.concatenate([e[1] for e in routed], axis=0)
    lane = lax.broadcasted_iota(jnp.int32, (tm, LANES), 1)
    oh0 = jnp.where(lane == e0, 1.0, 0.0)
    oh1 = jnp.where(lane == e1, 1.0, 0.0)
    both = (oh0 + oh1).astype(BF16)
    r_id = lax.broadcasted_iota(jnp.int32, (tm, tm), 0)
    c_id = lax.broadcasted_iota(jnp.int32, (tm, tm), 1)
    earlier = jnp.where(c_id < r_id, 1.0, 0.0).astype(BF16)
    before = _dot(earlier, both)
    rank0 = jnp.sum(before * oh0, axis=1, keepdims=True)
    rank1 = jnp.sum(before * oh1, axis=1, keepdims=True)
    cnt_ref[...] = _dot(jnp.ones((SUBLANES, tm), BF16), both)
    info = jnp.where(lane == 0, (e0 - MOE_GROUPS).astype(F32),
                     jnp.where(lane == 1, (e1 - MOE_GROUPS).astype(F32),
                               jnp.where(lane == 2, rank0, jnp.where(lane == 3, rank1, 0.0))))
    info_ref[...] = info.T[0:SUBLANES, :]


def _out_proj(xc, da_ctx, da_lat, mhf, mhb, mo, ghf, ghb, gr, modtab, ln_g, ln_b, ml_g, gl_g, wo_p, w_route,
              dn_alpha, n_ctx, tile0):
    b, n, d = xc.shape
    n_ctx_tiles = n_ctx // TM
    n_t = n // TM - tile0
    out_ctx_tiles = n_ctx_tiles - tile0
    row = lambda bb, i: (bb, i + tile0, 0)
    orow = lambda bb, i: (bb, i, 0)
    const2 = lambda bb, i: (0, 0)
    wide = pl.BlockSpec((None, TM, PAD_W), row)
    n_out = n_t * TM
    if da_ctx is None:
        da_ctx = da_lat
    ctx_rows = lambda bb, i: (bb, jnp.minimum(i, max(out_ctx_tiles - 1, 0)), 0)
    lat_rows = lambda bb, i: (bb, jnp.maximum(i - out_ctx_tiles, 0), 0)
    return pl.pallas_call(
        functools.partial(_out_kernel, dn_alpha, out_ctx_tiles),
        grid=(b, n_t),
        in_specs=[
            pl.BlockSpec((None, TM, d), row),
            pl.BlockSpec((None, TM, PAD_W), ctx_rows), pl.BlockSpec((None, TM, PAD_W), lat_rows),
            wide, wide, wide, wide, wide, wide,
            pl.BlockSpec((None, None, 6, d), lambda bb, i: (bb, jnp.where(i + tile0 >= n_ctx_tiles, 1, 0), 0, 0)),
            pl.BlockSpec((1, d), const2), pl.BlockSpec((1, d), const2),
            pl.BlockSpec((1, PAD_W), const2), pl.BlockSpec((1, PAD_W), const2),
            pl.BlockSpec((3 * PAD_W, d), const2),
            pl.BlockSpec((2, d, LANES), lambda bb, i: (0, 0, 0)),
        ],
        out_specs=[pl.BlockSpec((None, TM, d), orow),
                   pl.BlockSpec((None, TM * SUBLANES, LANES), orow),
                   pl.BlockSpec((None, SUBLANES, TM), orow),
                   pl.BlockSpec((None, SUBLANES, LANES), orow),
                   pl.BlockSpec((None, TM, LANES), orow)],
        out_shape=[jax.ShapeDtypeStruct((b, n_out, d), F32),
                   jax.ShapeDtypeStruct((b, n_out * SUBLANES, LANES), F32),
                   jax.ShapeDtypeStruct((b, n_t * SUBLANES, TM), F32),
                   jax.ShapeDtypeStruct((b, n_t * SUBLANES, LANES), F32),
                   jax.ShapeDtypeStruct((b, n_out, LANES), F32)],
        compiler_params=_cparams(("parallel", "arbitrary")),
        name="out_proj",
    )(xc, da_ctx, da_lat, mhf, mhb, mo, ghf, ghb, gr, modtab, ln_g, ln_b, ml_g, gl_g, wo_p, w_route)


def _tile_copy(src, src_tile, dst, dst_tile, sem):
    aligned = lambda t: t if isinstance(t, int) else pl.multiple_of(t, SUBLANES)
    return pltpu.make_async_copy(src.at[pl.ds(aligned(src_tile), SUBLANES), :],
                                 dst.at[pl.ds(aligned(dst_tile), SUBLANES), :], sem)


def _rows_from_tiles(ref, n_rows):
    return jnp.concatenate([ref[pl.ds(s, n_rows, stride=SUBLANES), :] for s in range(SUBLANES)], axis=1)


def _disp_kernel(dest_ref, cnt_ref, start_ref, padded_ref, f_ref, xs_hbm, zero_ref, sem, zsem, rsem, bsem):
    bb = pl.program_id(0)
    i = pl.program_id(1)
    tm = f_ref.shape[0] // SUBLANES
    n_tok = pl.num_programs(0) * pl.num_programs(1) * tm
    base = (bb * pl.num_programs(1) + i) * tm

    def scatter(start):
        def body(r, carry):
            for kk in range(2):
                cp = _tile_copy(f_ref, r * SUBLANES, xs_hbm, dest_ref[kk * n_tok + base + r], sem)
                if start:
                    cp.start(priority=kk)
                else:
                    cp.wait()
            return carry
        lax.fori_loop(0, tm, body, 0, unroll=8)

    scatter(True)
    scatter(False)

    @pl.when(jnp.logical_and(bb == pl.num_programs(0) - 1, i == pl.num_programs(1) - 1))
    def _():
        zero_ref[...] = jnp.zeros_like(zero_ref)
        n_slots = xs_hbm.shape[0] // SUBLANES
        last = MOE_EXPERTS - 1

        blk_tiles = zero_ref.shape[0]

        def fill(start):
            def row(r, c2):
                cp = _tile_copy(zero_ref, 0, xs_hbm, r * SUBLANES, zsem)
                if start:
                    cp.start()
                else:
                    cp.wait()
                return c2

            def block(j, c2):
                cp = pltpu.make_async_copy(zero_ref, xs_hbm.at[pl.ds(pl.multiple_of(j * blk_tiles, blk_tiles),
                                                                     blk_tiles), :], bsem)
                if start:
                    cp.start()
                else:
                    cp.wait()
                return c2

            def run(g, c2):
                cp = pltpu.make_async_copy(
                    zero_ref.at[pl.ds(0, PAD_RUN * SUBLANES), :],
                    xs_hbm.at[pl.ds(pl.multiple_of(g * (PAD_RUN * SUBLANES), PAD_RUN * SUBLANES),
                                    PAD_RUN * SUBLANES), :], rsem)
                if start:
                    cp.start()
                else:
                    cp.wait()
                return c2

            def per_expert(e, carry):
                lo, hi = start_ref[e] + cnt_ref[e], start_ref[e] + padded_ref[e]
                mid = jnp.minimum((lo + PAD_RUN - 1) // PAD_RUN * PAD_RUN, hi)
                carry = lax.fori_loop(lo, mid, row, carry)
                return lax.fori_loop(mid // PAD_RUN, hi // PAD_RUN, run, carry)
            lax.fori_loop(0, MOE_EXPERTS, per_expert, 0)
            used_blocks = (start_ref[last] + padded_ref[last]) // (blk_tiles // SUBLANES)
            lax.fori_loop(used_blocks, n_slots * SUBLANES // blk_tiles, block, 0)

        fill(True)
        fill(False)


def _dispatch(dest8, counts, pad_start, padded, f3, n_slots):
    b, rows, _ = f3.shape
    n_t = rows // (TM * SUBLANES)
    return pl.pallas_call(
        _disp_kernel,
        grid_spec=pltpu.PrefetchScalarGridSpec(
            num_scalar_prefetch=4,
            grid=(b, n_t),
            in_specs=[pl.BlockSpec((TM * SUBLANES, LANES), lambda bb, i, *_: (bb * n_t + i, 0))],
            out_specs=pl.BlockSpec(memory_space=pl.ANY),
            scratch_shapes=[pltpu.VMEM((MOE_BLOCK * SUBLANES, LANES), F32), pltpu.SemaphoreType.DMA(()),
                            pltpu.SemaphoreType.DMA(()), pltpu.SemaphoreType.DMA(()),
                            pltpu.SemaphoreType.DMA(())],
        ),
        out_shape=jax.ShapeDtypeStruct((n_slots * SUBLANES, LANES), F32),
        compiler_params=_cparams(("arbitrary", "arbitrary")),
        name="moe_dispatch",
    )(dest8, counts, pad_start, padded, f3.reshape(b * rows, LANES))


def _moe_kernel(be_ref, nu_ref, xs_ref, w1_ref, w3_ref, w2_ref, y_ref, w1b, w3b, w2b):
    j = pl.program_id(0)
    n_used = nu_ref[0]
    blk = xs_ref.shape[0] // SUBLANES

    @pl.when(jnp.logical_and(j < n_used, jnp.logical_or(j == 0, be_ref[j] != be_ref[jnp.maximum(j - 1, 0)])))
    def _():
        w1b[...] = w1_ref[...].astype(BF16)
        w3b[...] = w3_ref[...].astype(BF16)
        w2b[...] = w2_ref[...].astype(BF16)

    @pl.when(j < n_used)
    def _():
        wide = 2 * LANES
        h1 = h3 = None
        for c in range(w1b.shape[0] // wide):
            xc = jnp.concatenate([xs_ref[pl.ds(2 * c + s, blk, stride=SUBLANES), :] for s in range(2)],
                                 axis=1).astype(BF16)
            d1 = _dot(xc, w1b[c * wide:(c + 1) * wide, :])
            d3 = _dot(xc, w3b[c * wide:(c + 1) * wide, :])
            h1, h3 = (d1, d3) if c == 0 else (h1 + d1, h3 + d3)
        hidden = (_silu(h1) * h3).astype(BF16)
        for c in range(w2b.shape[1] // wide):
            yc = _dot(hidden, w2b[:, c * wide:(c + 1) * wide])
            for s in range(2):
                y_ref[pl.ds(2 * c + s, blk, stride=SUBLANES), :] = yc[:, s * LANES:(s + 1) * LANES]

    @pl.when(j >= n_used)
    def _():
        y_ref[...] = jnp.zeros_like(y_ref)


def _moe(xs, block_e, n_used, w1, w3, w2, layer):
    n_blocks = block_e.shape[0]
    d, hid = w1.shape[-2:]
    wmap = lambda j, be, nu: (layer, be[j], 0, 0)
    tiles = MOE_BLOCK * SUBLANES
    return pl.pallas_call(
        _moe_kernel,
        grid_spec=pltpu.PrefetchScalarGridSpec(
            num_scalar_prefetch=2,
            grid=(n_blocks,),
            in_specs=[
                pl.BlockSpec((tiles, LANES), lambda j, be, nu: (jnp.minimum(j, jnp.maximum(nu[0] - 1, 0)), 0)),
                pl.BlockSpec((None, None, d, hid), wmap),
                pl.BlockSpec((None, None, d, hid), wmap),
                pl.BlockSpec((None, None, hid, d), wmap),
            ],
            out_specs=pl.BlockSpec((tiles, LANES), lambda j, be, nu: (j, 0)),
            scratch_shapes=[pltpu.VMEM((d, hid), BF16), pltpu.VMEM((d, hid), BF16), pltpu.VMEM((hid, d), BF16)],
        ),
        out_shape=jax.ShapeDtypeStruct((n_blocks * tiles, LANES), F32),
        compiler_params=_cparams(("arbitrary",)),
        name="moe_experts",
    )(block_e, n_used, xs, w1, w3, w2)


def _comb_kernel(dn_alpha, dest_ref, y_hbm, x1_ref, rw_ref, mod_ref, lng_ref, lnb_ref, o_ref, ybuf, sems):
    bb = pl.program_id(0)
    i = pl.program_id(1)
    tm = x1_ref.shape[0]
    n_tok = pl.num_programs(0) * pl.num_programs(1) * tm
    base = (bb * pl.num_programs(1) + i) * tm

    def gather(start):
        def body(r, carry):
            for kk in range(2):
                cp = _tile_copy(y_hbm, dest_ref[kk * n_tok + base + r], ybuf.at[kk], r * SUBLANES, sems.at[kk])
                if start:
                    cp.start(priority=kk)
                else:
                    cp.wait()
            return carry
        lax.fori_loop(0, tm, body, 0, unroll=8)

    gather(True)
    gather(False)
    rw = rw_ref[...]
    y = rw[:, 0:1] * _rows_from_tiles(ybuf.at[0], tm) + rw[:, 1:2] * _rows_from_tiles(ybuf.at[1], tm)
    gate2 = mod_ref[5:6, :]
    o_ref[...] = _ln(dn_alpha * x1_ref[...] + gate2 * y) * lng_ref[...] + lnb_ref[...]


def _combine(dest, y_slots, x1, rw, modtab, ln_g, ln_b, dn_alpha, seg_of_tile):
    b, n, d = x1.shape
    n_t = n // TM
    row = lambda bb, i, dr: (bb, i, 0)
    const2 = lambda bb, i, dr: (0, 0)
    return pl.pallas_call(
        functools.partial(_comb_kernel, dn_alpha),
        grid_spec=pltpu.PrefetchScalarGridSpec(
            num_scalar_prefetch=1,
            grid=(b, n_t),
            in_specs=[
                pl.BlockSpec(memory_space=pl.ANY),
                pl.BlockSpec((None, TM, d), row),
                pl.BlockSpec((None, TM, LANES), row),
                pl.BlockSpec((None, None, 6, d), lambda bb, i, dr: (bb, seg_of_tile(i), 0, 0)),
                pl.BlockSpec((1, d), const2), pl.BlockSpec((1, d), const2),
            ],
            out_specs=pl.BlockSpec((None, TM, d), row),
            scratch_shapes=[pltpu.VMEM((2, TM * SUBLANES, LANES), F32), pltpu.SemaphoreType.DMA((2,))],
        ),
        out_shape=jax.ShapeDtypeStruct((b, n, d), F32),
        compiler_params=_cparams(("arbitrary", "arbitrary")),
        name="moe_combine",
    )(dest, y_slots, x1, rw, modtab, ln_g, ln_b)


def _dispatch_tables(info, tile_counts, n_blocks):
    b, n_t, _, tm = info.shape
    info = info.astype(jnp.int32)
    expert = jnp.moveaxis(info[:, :, 0:2, :], 2, 0).reshape(2, b * n_t, tm)
    rank = jnp.moveaxis(info[:, :, 2:4, :], 2, 0).reshape(2, b * n_t, tm)
    counts = jnp.sum(tile_counts, axis=0)
    padded = (counts + MOE_BLOCK - 1) // MOE_BLOCK * MOE_BLOCK
    pad_end = jnp.cumsum(padded)
    pad_start = pad_end - padded
    tile_base = pad_start[None, :] + jnp.cumsum(tile_counts, axis=0) - tile_counts
    onehot = expert[..., None] == jnp.arange(MOE_EXPERTS, dtype=jnp.int32)
    dest = jnp.sum(jnp.where(onehot, tile_base[None, :, None, :], 0), axis=-1) + rank
    dest8 = (dest * SUBLANES).reshape(-1).astype(jnp.int32)
    block_start = jnp.arange(n_blocks, dtype=jnp.int32) * MOE_BLOCK
    block_e = jnp.minimum(jnp.sum(pad_end[None, :] <= block_start[:, None], axis=1), MOE_EXPERTS - 1).astype(jnp.int32)
    n_used = (pad_end[-1] // MOE_BLOCK).astype(jnp.int32).reshape(1)
    i32 = lambda v: v.astype(jnp.int32)
    return dest8, i32(counts), i32(pad_start), i32(padded), block_e, n_used


def _pad_heads(w, width):
    lead = w.shape[:-1]
    w = w.reshape(*lead, HEADS, width)
    w = jnp.pad(w, [(0, 0)] * len(lead) + [(0, 0), (0, LANES - width)])
    return w.reshape(*lead, PAD_W)


def _expand_matrix(width):
    e = np.zeros((HEADS * width, PAD_W), np.float32)
    for h in range(HEADS):
        e[h * width + np.arange(width), h * LANES + np.arange(width)] = 1.0
    return jnp.asarray(e, BF16)


def _rope_tables(n_ctx, n_lat):
    n_freq = DA_DH // 4
    inv_freq = ROPE_BASE ** (-jnp.arange(n_freq, dtype=F32) / n_freq)
    pos = jnp.arange(n_lat, dtype=jnp.int32)
    ang_r = (pos // GRID_W).astype(F32)[:, None] * inv_freq
    ang_c = (pos % GRID_W).astype(F32)[:, None] * inv_freq
    ang = jnp.concatenate([ang_r, ang_r, ang_c, ang_c] * 2, axis=-1)
    cos, sin = jnp.cos(ang), jnp.sin(ang)
    first = (jnp.arange(LANES) % 32) < 16
    sin_a = jnp.where(first, -sin, 0.0)
    sin_b = jnp.where(first, 0.0, sin)
    ctx_rows = lambda v: jnp.full((n_ctx, LANES), v, F32)
    return jnp.stack([jnp.concatenate([ctx_rows(1.0), cos]), jnp.concatenate([ctx_rows(0.0), sin_a]),
                      jnp.concatenate([ctx_rows(0.0), sin_b])])


def kernel(x, c, ctx, c_ctx, w_ada, b_ada, w_in, da_lambda, da_norm, ml_conv_w, ml_conv_b, ml_ib, ml_fb,
           ml_norm, gl_wa, gl_ba, gl_norm, w_out, ln_mix_g, ln_mix_b, ln_ffn_g, ln_ffn_b,
           moe_wg, moe_we, moe_w1, moe_w3, moe_w2):
    b, s, d = x.shape
    n_ctx = ctx.shape[1]
    n = n_ctx + s
    depth = w_in.shape[0]
    assert n_ctx % TM == 0 and s % (TM * ATTN_NQ) == 0 and d % LANES == 0 and b % SCAN_NB == 0
    dn_alpha = (2 * depth) ** 0.25

    rows = -(-(b + 1) // SUBLANES) * SUBLANES
    cc = jnp.concatenate([c, c_ctx[None, :], jnp.zeros((rows - b - 1, d), F32)], axis=0)
    mod = _ada(cc, w_ada, b_ada).reshape(depth, rows, 6, d)

    xc = jnp.concatenate([ctx, x], axis=1)
    rope = _rope_tables(n_ctx, s)
    e64, e32 = _expand_matrix(64), _expand_matrix(32)

    for l in range(depth):
        last = l == depth - 1
        tile0 = n_ctx // TM if last else 0
        lam_init = 0.8 - 0.6 * math.exp(-0.3 * l)
        modtab = jnp.stack([jnp.broadcast_to(mod[l, b], (b, 6, d)), mod[l, :b]], axis=1)

        wi = w_in[l]
        o = np.cumsum([0, 512, 512, 512, 512, 256, 256, 8, 8, 128, 128, 256, 256, 32])
        misc_w = jnp.concatenate([wi[:, o[6]:o[8]], wi[:, o[12]:o[13]],
                                  jnp.zeros((d, LANES - 16 - 2 * GL_RANK), F32)], axis=1)
        w_in_p = jnp.concatenate([wi[:, :o[6]], wi[:, o[8]:o[12]], misc_w], axis=1).astype(BF16)

        (qa, ka, va, mq, mk, mv, mo, gq, gk, gv, gr, misc) = _in_proj(
            xc, modtab, w_in_p, rope, ml_conv_w[l], ml_conv_b[l].reshape(1, -1), e64, e32, n_ctx)

        n_ctx_tiles = n_ctx // TM
        attend = functools.partial(_attention, qa, ka, va, da_lambda[l], da_norm[l].reshape(1, -1), lam_init)
        da_ctx = None if last else attend(0, n_ctx_tiles, n_ctx, 1)
        da_lat = attend(n_ctx_tiles, s // TM, n, ATTN_NQ)

        gate_bias = jnp.concatenate([ml_ib[l].reshape(-1), ml_fb[l].reshape(-1),
                                     jnp.zeros((LANES - 16,), F32)]).reshape(1, LANES)
        mhf, mhb = _mlstm(mq, mk, mv, misc, gate_bias, n_ctx)

        wa_p = jnp.stack([jnp.pad(_pad_heads(gl_wa[l, dd], GL_DK),
                                  ((M_GA + dd * GL_RANK, LANES - M_GA - (dd + 1) * GL_RANK), (0, 0)))
                          for dd in range(2)])
        ba_p = _pad_heads(gl_ba[l], GL_DK).reshape(2, 1, PAD_W)
        ghf, ghb = _gla(gq, gk, gv, misc, wa_p.astype(BF16), ba_p, n_ctx)

        wo = w_out[l]
        wo_p = jnp.concatenate([
            wo[:512],
            jnp.pad(wo[512:768].reshape(HEADS, 64, d), ((0, 0), (0, 64), (0, 0))).reshape(PAD_W, d),
            jnp.pad(wo[768:1024].reshape(HEADS, 64, d), ((0, 0), (0, 64), (0, 0))).reshape(PAD_W, d),
        ], axis=0).astype(BF16)
        w_route = jnp.concatenate([moe_wg[l], moe_we[l],
                                   jnp.zeros((d, LANES - MOE_GROUPS - MOE_EXPERTS), F32)], axis=1)
        w_route_top = _bf16_part(w_route)
        w_route = jnp.stack([w_route_top.astype(BF16), (w_route - w_route_top).astype(BF16)])
        ml_g = _pad_heads(jnp.tile(ml_norm[l], HEADS), ML_DH).reshape(1, PAD_W)
        gl_g = _pad_heads(jnp.tile(gl_norm[l], HEADS), GL_DV).reshape(1, PAD_W)
        x1, f3, info, cnt, rw = _out_proj(xc, da_ctx, da_lat, mhf, mhb, mo, ghf, ghb, gr, modtab,
                                          ln_mix_g[l].reshape(1, d), ln_mix_b[l].reshape(1, d), ml_g, gl_g,
                                          wo_p, w_route, dn_alpha, n_ctx, tile0)

        n_rows = x1.shape[1]
        n_t = n_rows // TM
        a_all = b * n_rows * 2
        n_blocks = -(-(a_all + MOE_EXPERTS * (MOE_BLOCK - 1)) // MOE_BLOCK)
        tile_counts = cnt.reshape(b * n_t, SUBLANES, LANES)[:, 0, MOE_GROUPS:MOE_GROUPS + MOE_EXPERTS]
        dest8, counts, pad_start, padded, block_e, n_used = _dispatch_tables(
            info.reshape(b, n_t, SUBLANES, TM), tile_counts.astype(jnp.int32), n_blocks)
        xs = _dispatch(dest8, counts, pad_start, padded, f3, n_blocks * MOE_BLOCK)
        y_slots = _moe(xs, block_e, n_used, moe_w1, moe_w3, moe_w2, l)
        n_ctx_tiles_here = 0 if last else n_ctx // TM
        seg = lambda i: jnp.where(i >= n_ctx_tiles_here, 1, 0)
        xc = _combine(dest8, y_slots, x1, rw, modtab, ln_ffn_g[l].reshape(1, d), ln_ffn_b[l].reshape(1, d),
                      dn_alpha, seg)
    return xc
```

```python
import functools
import math

import numpy as np
import jax
import jax.numpy as jnp
from jax import lax
from jax.experimental import pallas as pl
from jax.experimental.pallas import tpu as pltpu

F32 = jnp.float32
BF16 = jnp.bfloat16
HI = lax.Precision.HIGHEST

GRID_W = 64
HEADS = 4
DA_DH = 64
ML_DH = 64
GL_DK = 32
GL_DV = 64
GL_RANK = 16
GL_NORMALIZER = 16.0
CHUNK = 64
SUB = 16
ROPE_BASE = 10000.0
MOE_GROUPS = 4
MOE_PER_GROUP = 8
MOE_EXPERTS = 32
MOE_HIDDEN = 512
LN_EPS = 1e-6

LANES = 128
SUBLANES = 8
VMEM_LIMIT = 48 * 1024 * 1024

TM = 256
MOE_BLOCK = 512
PAD_RUN = 32
SCAN_NB = 8
ATTN_NQ = 4
PAD_W = HEADS * LANES

C_QA, C_KA, C_VA, C_MQK, C_MV, C_MO = 0, 512, 1024, 1536, 2048, 2304
C_GQ, C_GK, C_GV, C_GR, C_MISC, IN_WP = 2560, 2688, 2816, 3072, 3328, 3456
M_IG, M_FG, M_GA = 0, 8, 16


def _cparams(sem):
    return pltpu.CompilerParams(dimension_semantics=sem, vmem_limit_bytes=VMEM_LIMIT)


def _sigmoid(x):
    return 1.0 / (1.0 + jnp.exp(-x))


def _silu(x):
    return x * _sigmoid(x)


def _log_sigmoid(x):
    return jnp.minimum(x, 0.0) - jnp.log(1.0 + jnp.exp(-jnp.abs(x)))


def _ln(x):
    mu = jnp.mean(x, axis=-1, keepdims=True)
    xc = x - mu
    var = jnp.mean(xc * xc, axis=-1, keepdims=True)
    return xc * lax.rsqrt(var + LN_EPS)


def _bf16_part(x):
    bits = lax.bitcast_convert_type(x, jnp.int32) & jnp.int32(-65536)
    return lax.bitcast_convert_type(bits, F32)


def _dot(a, b):
    return jnp.dot(a, b, preferred_element_type=F32)


def _dot_hi(a, b):
    return jnp.dot(a, b, precision=HI, preferred_element_type=F32)


def _cumsum_dot(a, b, split_lhs=False):
    x = a if split_lhs else b
    x0 = _bf16_part(x)
    r1 = x - x0
    x1 = _bf16_part(r1)
    x2 = r1 - x1
    parts = [p.astype(BF16) for p in (x0, x1, x2)]
    if split_lhs:
        return _dot(parts[0], b) + (_dot(parts[1], b) + _dot(parts[2], b))
    return _dot(a, parts[0]) + (_dot(a, parts[1]) + _dot(a, parts[2]))


def _dot_nt(a, b):
    return lax.dot_general(a, b, (((1,), (1,)), ((), ())), preferred_element_type=F32)


def _dot_tn(a, b):
    return lax.dot_general(a, b, (((0,), (0,)), ((), ())), preferred_element_type=F32)


def _ada_kernel(c_ref, w_ref, b_ref, o_ref):
    o_ref[...] = _dot_hi(_silu(c_ref[...]), w_ref[...]) + b_ref[...]


def _ada(cc, w_ada, b_ada):
    depth, d, n6 = w_ada.shape
    tn = 1536
    rows = cc.shape[0]
    return pl.pallas_call(
        _ada_kernel,
        grid=(depth, n6 // tn),
        in_specs=[
            pl.BlockSpec((rows, d), lambda l, j: (0, 0)),
            pl.BlockSpec((None, d, tn), lambda l, j: (l, 0, j)),
            pl.BlockSpec((None, 1, tn), lambda l, j: (l, 0, j)),
        ],
        out_specs=pl.BlockSpec((None, rows, tn), lambda l, j: (l, 0, j)),
        out_shape=jax.ShapeDtypeStruct((depth, rows, n6), F32),
        compiler_params=_cparams(("arbitrary", "arbitrary")),
        name="ada_mod",
    )(cc, w_ada, b_ada.reshape(depth, 1, n6))


def _in_kernel(n_ctx_tiles, xc_ref, x_ref, xp_ref, xn_ref, mod_ref, w_ref, rope_ref, cw_ref, cb_ref,
               e64_ref, e32_ref,
               qa_ref, ka_ref, va_ref, mq_ref, mk_ref, mv_ref, mo_ref,
               gq_ref, gk_ref, gv_ref, gr_ref, misc_ref, pext_ref):
    i = pl.program_id(1)
    n_tiles = pl.num_programs(1)
    tm = x_ref.shape[0]
    shift, scale = mod_ref[0:1, :], mod_ref[1:2, :]

    def prenorm(x):
        return (_ln(x) * (1.0 + scale) + shift).astype(BF16)

    half = tm // 2
    halves = [slice(0, half), slice(half, tm)]
    groups = ((C_QA, C_KA), (C_KA, C_VA), (C_VA, C_MQK), (C_MQK, C_MV), (C_MV, C_MO), (C_MO, C_GQ),
              (C_GQ, C_GK), (C_GK, C_GV), (C_GV, C_GR), (C_GR, C_MISC), (C_MISC, IN_WP))
    e64 = e64_ref[...]
    e32 = e32_ref[...]
    lane = lax.broadcasted_iota(jnp.int32, (half, PAD_W), 1)
    ones_col = jnp.where(lane % LANES >= ML_DH, 1.0, 0.0)
    seg_first = jnp.logical_or(i == 0, i == n_ctx_tiles)
    seg_last = jnp.logical_or(i == n_ctx_tiles - 1, i == n_tiles - 1)
    keep_prev = jnp.where(seg_first, 0.0, 1.0)
    keep_next = jnp.where(seg_last, 0.0, 1.0)

    hs = [prenorm(jnp.where(i < n_ctx_tiles, xc_ref[r, :], x_ref[r, :])) for r in halves]
    halo = jnp.concatenate([prenorm(xp_ref[...]), prenorm(xn_ref[...])], axis=0)
    parts = [{lo: _dot(h, w_ref[:, lo:hi]) for lo, hi in groups} for h in hs]
    p_halo = _dot(halo, w_ref[:, C_MQK:C_MV])
    pext_ref[0:SUBLANES, :] = p_halo[0:SUBLANES] * keep_prev
    pext_ref[SUBLANES + tm:2 * SUBLANES + tm, :] = p_halo[SUBLANES:2 * SUBLANES] * keep_next
    for r, part in zip(halves, parts):
        pext_ref[SUBLANES + r.start:SUBLANES + r.stop, :] = part[C_MQK]

    for r, part in zip(halves, parts):
        cos, sin_a, sin_b = rope_ref[0, r, :], rope_ref[1, r, :], rope_ref[2, r, :]
        for lo, out_ref, mult in ((C_QA, qa_ref, DA_DH ** -0.5 * math.log2(math.e)), (C_KA, ka_ref, 1.0)):
            for hh in range(HEADS):
                xh = part[lo][:, hh * LANES:(hh + 1) * LANES]
                rot = xh * cos + pltpu.roll(xh, LANES - 16, 1) * sin_a + pltpu.roll(xh, 16, 1) * sin_b
                out_ref[r, hh * LANES:(hh + 1) * LANES] = (rot * mult).astype(BF16)
        va_ref[r, :] = part[C_VA].astype(BF16)
        mv_ref[r, :] = (_dot(part[C_MV].astype(BF16), e64) + ones_col).astype(BF16)
        mo_ref[r, :] = _dot(part[C_MO].astype(BF16), e64).astype(BF16)
        gq_ref[r, :] = _dot((part[C_GQ] * GL_DK ** -0.5).astype(BF16), e32).astype(BF16)
        gk_ref[r, :] = _dot(part[C_GK].astype(BF16), e32).astype(BF16)
        gv_ref[r, :] = _dot(part[C_GV].astype(BF16), e64).astype(BF16)
        gr_ref[r, :] = _dot(part[C_GR].astype(BF16), e64).astype(BF16)
        misc_ref[r, :] = part[C_MISC]

    for r, part in zip(halves, parts):
        y = (cw_ref[0:1, :] * pext_ref[SUBLANES - 1 + r.start:SUBLANES - 1 + r.stop, :] + cw_ref[1:2, :] * part[C_MQK]
             + cw_ref[2:3, :] * pext_ref[SUBLANES + 1 + r.start:SUBLANES + 1 + r.stop, :] + cb_ref[...])
        y = _silu(y)
        mq_ref[r, :] = _dot((y[:, :256] * ML_DH ** -0.5).astype(BF16), e64).astype(BF16)
        mk_ref[r, :] = _dot(y[:, 256:].astype(BF16), e64).astype(BF16)


def _in_proj(x_ctx, x_lat, lat_off, n_lat, modtab, w_in_p, rope, conv_w, conv_b, e64, e32, n_ctx):
    b, _, d = x_lat.shape
    n = n_ctx + n_lat
    n_tiles = n // TM
    n_ctx_tiles = n_ctx // TM
    assert n_ctx_tiles == 1
    hb = TM // SUBLANES
    n_lat_hb = n_lat // SUBLANES
    lat_tile = lambda i: jnp.maximum(i - n_ctx_tiles, 0)
    row = lambda bb, i: (bb, i, 0)
    const2 = lambda bb, i: (0, 0)
    outs = [jax.ShapeDtypeStruct((b, n, PAD_W), BF16)] * 11 + [jax.ShapeDtypeStruct((b, n, LANES), F32)]
    out_specs = [pl.BlockSpec((None, TM, PAD_W), row)] * 11 + [pl.BlockSpec((None, TM, LANES), row)]
    return pl.pallas_call(
        functools.partial(_in_kernel, n_ctx_tiles),
        grid=(b, n_tiles),
        in_specs=[
            pl.BlockSpec((None, TM, d), lambda bb, i: (bb, jnp.minimum(i, n_ctx_tiles - 1), 0)),
            pl.BlockSpec((None, TM, d), lambda bb, i: (bb, lat_tile(i) + lat_off, 0)),
            pl.BlockSpec((None, SUBLANES, d),
                         lambda bb, i: (bb, jnp.maximum(lat_tile(i) * hb - 1, 0) + lat_off * hb, 0)),
            pl.BlockSpec((None, SUBLANES, d),
                         lambda bb, i: (bb, jnp.minimum((lat_tile(i) + 1) * hb, n_lat_hb - 1) + lat_off * hb, 0)),
            pl.BlockSpec((None, None, 6, d), lambda bb, i: (bb, jnp.where(i >= n_ctx_tiles, 1, 0), 0, 0)),
            pl.BlockSpec((d, IN_WP), const2),
            pl.BlockSpec((3, TM, LANES), lambda bb, i: (0, i, 0)),
            pl.BlockSpec((3, 512), const2),
            pl.BlockSpec((1, 512), const2),
            pl.BlockSpec((256, PAD_W), const2),
            pl.BlockSpec((128, PAD_W), const2),
        ],
        out_specs=out_specs,
        out_shape=outs,
        scratch_shapes=[pltpu.VMEM((TM + 2 * SUBLANES, 512), F32)],
        compiler_params=_cparams(("parallel", "arbitrary")),
        name="in_proj",
    )(x_ctx, x_lat, x_lat, x_lat, modtab, w_in_p, rope, conv_w, conv_b, e64, e32)


def _attn_kernel(lam_init, n_q, *refs):
    q_refs = refs[:n_q]
    k_ref, v_ref, lamp_ref, g_ref, o_ref = refs[n_q:]
    lp = lamp_ref[...]
    lam = (jnp.exp(jnp.sum(lp[0:1] * lp[1:2], axis=1, keepdims=True))
           - jnp.exp(jnp.sum(lp[2:3] * lp[3:4], axis=1, keepdims=True)) + lam_init)
    k = k_ref[...]
    v = jnp.concatenate([v_ref[...], jnp.ones(v_ref.shape, BF16)], axis=1)
    tq = q_refs[0].shape[0]
    lane = lax.broadcasted_iota(jnp.int32, (tq, LANES), 1)
    scores = []
    for q_ref in q_refs:
        q = q_ref[...]
        zero = jnp.zeros_like(q)
        scores.append((_dot_nt(jnp.where(lane < DA_DH, q, zero), k), _dot_nt(jnp.where(lane >= DA_DH, q, zero), k)))
    for t, (s1, s2) in enumerate(scores):
        p1 = _dot(jnp.exp2(s1 - jnp.max(s1, axis=1, keepdims=True)).astype(BF16), v)
        p2 = _dot(jnp.exp2(s2 - jnp.max(s2, axis=1, keepdims=True)).astype(BF16), v)
        o = p1[:, 0:LANES] / p1[:, LANES:2 * LANES] - p2[:, 0:LANES] * (lam / p2[:, LANES:2 * LANES])
        y = o * lax.rsqrt(jnp.mean(o * o, axis=1, keepdims=True) + LN_EPS) * g_ref[...]
        o_ref[t * tq:(t + 1) * tq, :] = (y * (1.0 - lam_init)).astype(BF16)


def _attention(qa, ka, va, lam_p, norm_g, lam_init, tile0, n_tiles, n_keys, n_q):
    b = qa.shape[0]
    tq = TM
    q_spec = lambda t: pl.BlockSpec((None, tq, LANES), lambda bb, h, i: (bb, tile0 + i * n_q + t, h))
    kv_spec = pl.BlockSpec((None, n_keys, LANES), lambda bb, h, i: (bb, 0, h))
    return pl.pallas_call(
        functools.partial(_attn_kernel, lam_init, n_q),
        grid=(b, HEADS, n_tiles // n_q),
        in_specs=[q_spec(t) for t in range(n_q)] + [
            kv_spec, kv_spec,
            pl.BlockSpec((4, DA_DH), lambda bb, h, i: (0, 0)),
            pl.BlockSpec((1, LANES), lambda bb, h, i: (0, 0)),
        ],
        out_specs=pl.BlockSpec((None, n_q * tq, LANES), lambda bb, h, i: (bb, i, h)),
        out_shape=jax.ShapeDtypeStruct((b, n_tiles * tq, PAD_W), BF16),
        compiler_params=_cparams(("parallel", "parallel", "arbitrary")),
        name="diff_attn",
    )(*([qa] * n_q), ka, va, lam_p, norm_g)


def _chunk_maps(n_ctx_chunks, n_chunks):
    def fwd(bb, t):
        return (bb, t, 0)

    def bwd(bb, t):
        return (bb, jnp.where(t < n_ctx_chunks, n_ctx_chunks - 1 - t, n_chunks - 1 + n_ctx_chunks - t), 0)

    return fwd, bwd


def _tri(direction):
    l = lax.broadcasted_iota(jnp.int32, (CHUNK, CHUNK), 0)
    s = lax.broadcasted_iota(jnp.int32, (CHUNK, CHUNK), 1)
    return (s <= l) if direction == 0 else (s >= l)


def _mlstm_kernel(qf_ref, kf_ref, vf_ref, gf_ref, qb_ref, kb_ref, vb_ref, gb_ref, bias_ref, sel_ref,
                  hf_ref, hb_ref, c_ref, m_ref):
    t = pl.program_id(1)

    @pl.when(t == 0)
    def _():
        c_ref[...] = jnp.zeros_like(c_ref)
        m_ref[...] = jnp.zeros_like(m_ref)

    lane = lax.broadcasted_iota(jnp.int32, (CHUNK, LANES), 1)
    is_fg = jnp.logical_and(lane >= M_FG, lane < M_GA)
    dirs = ((qf_ref, kf_ref, vf_ref, gf_ref, hf_ref), (qb_ref, kb_ref, vb_ref, gb_ref, hb_ref))
    n_b = qf_ref.shape[0]
    masks = [_tri(d) for d in range(2)]
    units = [(d, nb, h) for d in range(2) for nb in range(n_b) for h in range(HEADS)]

    pre = {}
    for d in range(2):
        tri = jnp.where(masks[d], 1.0, 0.0).astype(BF16)
        for nb in range(n_b):
            g = dirs[d][3][nb] + bias_ref[...]
            g = jnp.where(is_fg, _log_sigmoid(g), g)
            cum = _cumsum_dot(tri, g)
            pre[d, nb] = (_cumsum_dot(cum, sel_ref[d, 0], split_lhs=True),
                          _cumsum_dot(g, sel_ref[d, 1], split_lhs=True), g.T, cum.T)

    qk, qc = {}, {}
    for u in units:
        d, nb, h = u
        hs = slice(h * LANES, (h + 1) * LANES)
        q = dirs[d][0][nb, :, hs]
        qk[u] = _dot_nt(q, dirs[d][1][nb, :, hs])
        qc[u] = _dot(q, c_ref[nb, d, h].astype(BF16))

    gate = {}
    for u in units:
        d, nb, h = u
        hs = slice(h * LANES, (h + 1) * LANES)
        b_all, i_all, g_t, cum_t = pre[d, nb]
        ci, cf = M_IG + HEADS * d + h, M_FG + HEADS * d + h
        last = CHUNK - 1 if d == 0 else 0
        b_rep = b_all[:, hs]
        i_rep = i_all[:, hs]
        row = g_t[ci:ci + 1, :] - cum_t[cf:cf + 1, :]
        m_prev = m_ref[nb, d, h][0:1, :]
        log_d = jnp.where(masks[d], b_rep[:, 0:CHUNK] + row, -jnp.inf)
        log_a = b_rep + m_prev
        m_t = jnp.maximum(log_a, jnp.max(log_d, axis=1, keepdims=True))
        m_new = m_t[last:last + 1, :]
        b_last = b_rep[last:last + 1, :]
        gate[u] = (jnp.exp(log_d - m_t[:, 0:CHUNK]), jnp.exp(log_a - m_t), jnp.exp(-m_t),
                   jnp.exp(b_last - b_rep + i_rep - m_new), jnp.exp(b_last + m_prev - m_new), m_new)

    sv, kv = {}, {}
    for u in units:
        d, nb, h = u
        hs = slice(h * LANES, (h + 1) * LANES)
        v = dirs[d][2][nb, :, hs]
        dmat, _, _, w, _, _ = gate[u]
        sv[u] = _dot((qk[u] * dmat).astype(BF16), v)
        kv[u] = _dot_tn((dirs[d][1][nb, :, hs].astype(F32) * w).astype(BF16), v)

    for u in units:
        d, nb, h = u
        hs = slice(h * LANES, (h + 1) * LANES)
        _, a, floor, _, decay, m_new = gate[u]
        num = a * qc[u] + sv[u]
        qn = pltpu.roll(num, ML_DH, 1)
        den = jnp.maximum(jnp.abs(qn), floor)
        dirs[d][4][nb, :, hs] = jnp.where(lane < ML_DH, num / den, 0.0).astype(BF16)
        c_ref[nb, d, h] = decay * c_ref[nb, d, h] + kv[u]
        m_ref[nb, d, h] = jnp.broadcast_to(m_new, (SUBLANES, LANES))


def _gate_select_matrices():
    sel = np.zeros((2, 2, LANES, PAD_W), np.float32)
    for d in range(2):
        for h in range(HEADS):
            sel[d, 0, M_FG + HEADS * d + h, h * LANES:(h + 1) * LANES] = 1.0
            sel[d, 1, M_IG + HEADS * d + h, h * LANES:(h + 1) * LANES] = 1.0
    return jnp.asarray(sel, BF16)


def _mlstm(mq, mk, mv, misc, gate_bias, n_ctx):
    b, n, _ = mq.shape
    n_chunks = n // CHUNK
    fwd, bwd = _chunk_maps(n_ctx // CHUNK, n_chunks)
    wide = lambda m: pl.BlockSpec((SCAN_NB, CHUNK, PAD_W), m)
    thin = lambda m: pl.BlockSpec((SCAN_NB, CHUNK, LANES), m)
    return pl.pallas_call(
        _mlstm_kernel,
        grid=(b // SCAN_NB, n_chunks),
        in_specs=[wide(fwd), wide(fwd), wide(fwd), thin(fwd), wide(bwd), wide(bwd), wide(bwd), thin(bwd),
                  pl.BlockSpec((1, LANES), lambda bb, t: (0, 0)),
                  pl.BlockSpec((2, 2, LANES, PAD_W), lambda bb, t: (0, 0, 0, 0))],
        out_specs=[wide(fwd), wide(bwd)],
        out_shape=[jax.ShapeDtypeStruct((b, n, PAD_W), BF16)] * 2,
        scratch_shapes=[pltpu.VMEM((SCAN_NB, 2, HEADS, LANES, LANES), F32),
                        pltpu.VMEM((SCAN_NB, 2, HEADS, SUBLANES, LANES), F32)],
        compiler_params=_cparams(("parallel", "arbitrary")),
        name="mlstm_scan",
    )(mq, mk, mv, misc, mq, mk, mv, misc, gate_bias, _gate_select_matrices())


def _gla_kernel(qf_ref, kf_ref, vf_ref, gf_ref, qb_ref, kb_ref, vb_ref, gb_ref, wa_ref, ba_ref,
                hf_ref, hb_ref, st_ref):
    t = pl.program_id(1)

    @pl.when(t == 0)
    def _():
        st_ref[...] = jnp.zeros_like(st_ref)

    n_sub = CHUNK // SUB
    row_id = lax.broadcasted_iota(jnp.int32, (CHUNK, LANES), 0)
    row_blk = row_id // SUB
    col_id = lax.broadcasted_iota(jnp.int32, (CHUNK, LANES), 1)
    rows_w = lax.broadcasted_iota(jnp.int32, (CHUNK, PAD_W), 0)
    dirs = ((qf_ref, kf_ref, vf_ref, gf_ref, hf_ref), (qb_ref, kb_ref, vb_ref, gb_ref, hb_ref))
    n_b = qf_ref.shape[0]
    pairs = [(d, nb) for d in range(2) for nb in range(n_b)]
    units = [(d, nb, h) for d, nb in pairs for h in range(HEADS)]
    tris = [jnp.where(_tri(d), 1.0, 0.0).astype(BF16) for d in range(2)]
    causal = [col_id <= row_id, jnp.logical_and(col_id >= row_id, col_id < CHUNK)]

    las = {p: _log_sigmoid(_dot(dirs[p[0]][3][p[1]].astype(BF16), wa_ref[p[0]]) + ba_ref[p[0]])
           * (1.0 / GL_NORMALIZER) for p in pairs}
    cums = {p: _cumsum_dot(tris[p[0]], las[p]) for p in pairs}

    pre = {}
    for p in pairs:
        d, nb = p
        cum = cums[p]
        excl = cum - las[p]
        last = CHUNK - 1 if d == 0 else 0
        betas = [excl[r:r + 1, :] for r in (range(0, CHUNK, SUB) if d == 0 else range(SUB - 1, CHUNK, SUB))]
        beta_rows = jnp.concatenate([jnp.broadcast_to(bt, (SUB, PAD_W)) for bt in betas], axis=0)
        q = dirs[d][0][nb].astype(F32)
        k = dirs[d][1][nb].astype(F32)
        q_loc = (q * jnp.exp(cum - beta_rows)).astype(BF16)
        q_abs = (q * jnp.exp(cum)).astype(BF16)
        b_last = cum[last:last + 1, :]
        k_end = (k * jnp.exp(b_last - cum)).astype(BF16)
        k_subs = []
        for I in range(n_sub):
            seen = (rows_w < (I + 1) * SUB) if d == 0 else (rows_w >= I * SUB)
            k_subs.append((k * jnp.exp(jnp.where(seen, betas[I] - cum, -jnp.inf))).astype(BF16))
        pre[p] = (q_loc, q_abs, b_last, k_end, k_subs)

    scores, inter, kv = {}, {}, {}
    for u in units:
        d, nb, h = u
        hs = slice(h * LANES, (h + 1) * LANES)
        q_loc, q_abs, _, k_end, k_subs = pre[d, nb]
        kcat = jnp.concatenate([k_subs[I][:, hs] for I in range(n_sub)], axis=0)
        scores[u] = _dot_nt(q_loc[:, hs], kcat)
        inter[u] = _dot_nt(q_abs[:, hs], st_ref[nb, d, h].astype(BF16))
        kv[u] = _dot_tn(dirs[d][2][nb, :, hs], k_end[:, hs])

    att = {}
    for u in units:
        r = scores[u]
        a = None
        for j in range(n_sub // 2):
            pair = r[:, j * LANES:(j + 1) * LANES]
            for I, part in ((2 * j, pair), (2 * j + 1, pltpu.roll(pair, CHUNK, 1))):
                a = part if a is None else jnp.where(row_blk == I, part, a)
        att[u] = jnp.where(causal[u[0]], a, 0.0)[:, 0:CHUNK].astype(BF16)

    av = {u: _dot(att[u], dirs[u[0]][2][u[1], :, u[2] * LANES:(u[2] + 1) * LANES]) for u in units}
    for u in units:
        d, nb, h = u
        hs = slice(h * LANES, (h + 1) * LANES)
        dirs[d][4][nb, :, hs] = (inter[u] + av[u]).astype(BF16)
        st_ref[nb, d, h] = st_ref[nb, d, h] * jnp.exp(pre[d, nb][2][:, hs]) + kv[u]


def _gla(gq, gk, gv, misc, wa_p, ba_p, n_ctx):
    b, n, _ = gq.shape
    n_chunks = n // CHUNK
    fwd, bwd = _chunk_maps(n_ctx // CHUNK, n_chunks)
    wide = lambda m: pl.BlockSpec((SCAN_NB, CHUNK, PAD_W), m)
    thin = lambda m: pl.BlockSpec((SCAN_NB, CHUNK, LANES), m)
    return pl.pallas_call(
        _gla_kernel,
        grid=(b // SCAN_NB, n_chunks),
        in_specs=[wide(fwd), wide(fwd), wide(fwd), thin(fwd), wide(bwd), wide(bwd), wide(bwd), thin(bwd),
                  pl.BlockSpec((2, LANES, PAD_W), lambda bb, t: (0, 0, 0)),
                  pl.BlockSpec((2, 1, PAD_W), lambda bb, t: (0, 0, 0))],
        out_specs=[wide(fwd), wide(bwd)],
        out_shape=[jax.ShapeDtypeStruct((b, n, PAD_W), BF16)] * 2,
        scratch_shapes=[pltpu.VMEM((SCAN_NB, 2, HEADS, LANES, LANES), F32)],
        compiler_params=_cparams(("parallel", "arbitrary")),
        name="gla_scan",
    )(gq, gk, gv, misc, gq, gk, gv, misc, wa_p, ba_p)


def _head_rms_padded(x, width):
    parts = []
    for h in range(HEADS):
        xh = x[:, h * LANES:(h + 1) * LANES]
        ms = jnp.sum(xh * xh, axis=1, keepdims=True) * (1.0 / width)
        parts.append(xh * lax.rsqrt(ms + LN_EPS))
    return jnp.concatenate(parts, axis=1)


def _out_kernel(dn_alpha, n_ctx_tiles, xc_ref, x_ref, dac_ref, dal_ref, mhf_ref, mhb_ref, mo_ref, ghf_ref, ghb_ref, gr_ref,
                mod_ref, lng_ref, lnb_ref, mlg_ref, glg_ref, wo_ref, wr_ref,
                x1_ref, f_ref, info_ref, cnt_ref, rw_ref):
    tm = x_ref.shape[0]
    half = tm // 2
    halves = [slice(0, half), slice(half, tm)]
    gate1, shift2, scale2 = mod_ref[2:3, :], mod_ref[3:4, :], mod_ref[4:5, :]
    is_ctx = pl.program_id(1) < n_ctx_tiles
    lane = lax.broadcasted_iota(jnp.int32, (half, LANES), 1)
    neg = -jnp.inf
    big = jnp.int32(LANES)

    def first_argmax(vals, vmax):
        return jnp.min(jnp.where(vals == vmax, lane, big), axis=1, keepdims=True)

    mixed = []
    for r in halves:
        ml = (_head_rms_padded(mhf_ref[r, :].astype(F32) + mhb_ref[r, :].astype(F32), ML_DH) * mlg_ref[...]
              * _sigmoid(mo_ref[r, :].astype(F32)))
        gl = (_head_rms_padded(ghf_ref[r, :].astype(F32) + ghb_ref[r, :].astype(F32), GL_DV) * glg_ref[...]
              * _silu(gr_ref[r, :].astype(F32)))
        mixed.append((jnp.where(is_ctx, dac_ref[r, :], dal_ref[r, :]), ml.astype(BF16), gl.astype(BF16)))
    ys = [_dot(da, wo_ref[0:PAD_W, :]) + _dot(ml, wo_ref[PAD_W:2 * PAD_W, :])
          + _dot(gl, wo_ref[2 * PAD_W:3 * PAD_W, :]) for da, ml, gl in mixed]
    fs = []
    for r, y in zip(halves, ys):
        x_in = jnp.where(is_ctx, xc_ref[r, :], x_ref[r, :])
        x1 = _ln(dn_alpha * x_in + gate1 * y) * lng_ref[...] + lnb_ref[...]
        x1_ref[r, :] = x1
        f = _ln(x1) * (1.0 + scale2) + shift2
        for s in range(SUBLANES):
            f_ref[pl.ds(r.start * SUBLANES + s, half, stride=SUBLANES), :] = f[:, s * LANES:(s + 1) * LANES]
        fs.append(f)

    all_logits = []
    for f in fs:
        f_top = _bf16_part(f)
        f_hi, f_lo = f_top.astype(BF16), (f - f_top).astype(BF16)
        all_logits.append(_dot(f_hi, wr_ref[0]) + (_dot(f_lo, wr_ref[0]) + _dot(f_hi, wr_ref[1])
                                                   + _dot(f_lo, wr_ref[1])))
    routed = []
    for r, logits in zip(halves, all_logits):
        gl_ = jnp.where(lane < MOE_GROUPS, logits, neg)
        gmax = jnp.max(gl_, axis=1, keepdims=True)
        g_idx = first_argmax(gl_, gmax)
        g_gate = 1.0 / jnp.sum(jnp.exp(gl_ - gmax), axis=1, keepdims=True)
        in_group = jnp.logical_and(lane >= MOE_GROUPS, (lane - MOE_GROUPS) // MOE_PER_GROUP == g_idx)
        el = jnp.where(in_group, logits, neg)
        emax = jnp.max(el, axis=1, keepdims=True)
        e0 = first_argmax(el, emax)
        el2 = jnp.where(lane == e0, neg, el)
        emax2 = jnp.max(el2, axis=1, keepdims=True)
        e1 = first_argmax(el2, emax2)
        esum = jnp.sum(jnp.exp(el - emax), axis=1, keepdims=True)
        p0 = 1.0 / esum
        p1 = jnp.exp(emax2 - emax) / esum
        w0 = g_gate * p0 / (p0 + p1)
        w1 = g_gate * p1 / (p0 + p1)
        rw_ref[r, :] = jnp.where(lane == 0, w0, jnp.where(lane == 1, w1, 0.0))
        routed.append((e0, e1))

    e0 = jnp.concatenate([e[0] for e in routed], axis=0)
    e1 = jnp.concatenate([e[1] for e in routed], axis=0)
    lane = lax.broadcasted_iota(jnp.int32, (tm, LANES), 1)
    oh0 = jnp.where(lane == e0, 1.0, 0.0)
    oh1 = jnp.where(lane == e1, 1.0, 0.0)
    both = (oh0 + oh1).astype(BF16)
    r_id = lax.broadcasted_iota(jnp.int32, (tm, tm), 0)
    c_id = lax.broadcasted_iota(jnp.int32, (tm, tm), 1)
    earlier = jnp.where(c_id < r_id, 1.0, 0.0).astype(BF16)
    before = _dot(earlier, both)
    rank0 = jnp.sum(before * oh0, axis=1, keepdims=True)
    rank1 = jnp.sum(before * oh1, axis=1, keepdims=True)
    cnt_ref[...] = _dot(jnp.ones((SUBLANES, tm), BF16), both)
    info = jnp.where(lane == 0, (e0 - MOE_GROUPS).astype(F32),
                     jnp.where(lane == 1, (e1 - MOE_GROUPS).astype(F32),
                               jnp.where(lane == 2, rank0, jnp.where(lane == 3, rank1, 0.0))))
    info_ref[...] = info.T[0:SUBLANES, :]


def _out_proj(x_ctx, x_lat, lat_off, n, da_ctx, da_lat, mhf, mhb, mo, ghf, ghb, gr, modtab, ln_g, ln_b, ml_g, gl_g,
              wo_p, w_route, dn_alpha, n_ctx, tile0):
    b, _, d = x_lat.shape
    n_ctx_tiles = n_ctx // TM
    n_t = n // TM - tile0
    out_ctx_tiles = n_ctx_tiles - tile0
    row = lambda bb, i: (bb, i + tile0, 0)
    orow = lambda bb, i: (bb, i, 0)
    const2 = lambda bb, i: (0, 0)
    wide = pl.BlockSpec((None, TM, PAD_W), row)
    n_out = n_t * TM
    if da_ctx is None:
        da_ctx = da_lat
    ctx_rows = lambda bb, i: (bb, jnp.minimum(i, max(out_ctx_tiles - 1, 0)), 0)
    lat_rows = lambda bb, i: (bb, jnp.maximum(i - out_ctx_tiles, 0), 0)
    return pl.pallas_call(
        functools.partial(_out_kernel, dn_alpha, out_ctx_tiles),
        grid=(b, n_t),
        in_specs=[
            pl.BlockSpec((None, TM, d), lambda bb, i: (bb, jnp.minimum(i + tile0, n_ctx_tiles - 1), 0)),
            pl.BlockSpec((None, TM, d), lambda bb, i: (bb, jnp.maximum(i + tile0 - n_ctx_tiles, 0) + lat_off, 0)),
            pl.BlockSpec((None, TM, PAD_W), ctx_rows), pl.BlockSpec((None, TM, PAD_W), lat_rows),
            wide, wide, wide, wide, wide, wide,
            pl.BlockSpec((None, None, 6, d), lambda bb, i: (bb, jnp.where(i + tile0 >= n_ctx_tiles, 1, 0), 0, 0)),
            pl.BlockSpec((1, d), const2), pl.BlockSpec((1, d), const2),
            pl.BlockSpec((1, PAD_W), const2), pl.BlockSpec((1, PAD_W), const2),
            pl.BlockSpec((3 * PAD_W, d), const2),
            pl.BlockSpec((2, d, LANES), lambda bb, i: (0, 0, 0)),
        ],
        out_specs=[pl.BlockSpec((None, TM, d), orow),
                   pl.BlockSpec((None, TM * SUBLANES, LANES), orow),
                   pl.BlockSpec((None, SUBLANES, TM), orow),
                   pl.BlockSpec((None, SUBLANES, LANES), orow),
                   pl.BlockSpec((None, TM, LANES), orow)],
        out_shape=[jax.ShapeDtypeStruct((b, n_out, d), F32),
                   jax.ShapeDtypeStruct((b, n_out * SUBLANES, LANES), F32),
                   jax.ShapeDtypeStruct((b, n_t * SUBLANES, TM), F32),
                   jax.ShapeDtypeStruct((b, n_t * SUBLANES, LANES), F32),
                   jax.ShapeDtypeStruct((b, n_out, LANES), F32)],
        compiler_params=_cparams(("parallel", "arbitrary")),
        name="out_proj",
    )(x_ctx, x_lat, da_ctx, da_lat, mhf, mhb, mo, ghf, ghb, gr, modtab, ln_g, ln_b, ml_g, gl_g, wo_p, w_route)


def _tile_copy(src, src_tile, dst, dst_tile, sem):
    aligned = lambda t: t if isinstance(t, int) else pl.multiple_of(t, SUBLANES)
    return pltpu.make_async_copy(src.at[pl.ds(aligned(src_tile), SUBLANES), :],
                                 dst.at[pl.ds(aligned(dst_tile), SUBLANES), :], sem)


def _rows_from_tiles(ref, n_rows):
    return jnp.concatenate([ref[pl.ds(s, n_rows, stride=SUBLANES), :] for s in range(SUBLANES)], axis=1)


def _disp_kernel(dest_ref, cnt_ref, start_ref, padded_ref, f_ref, xs_hbm, zero_ref, sem, zsem, rsem, bsem):
    bb = pl.program_id(0)
    i = pl.program_id(1)
    tm = f_ref.shape[0] // SUBLANES
    n_tok = pl.num_programs(0) * pl.num_programs(1) * tm
    base = (bb * pl.num_programs(1) + i) * tm

    def scatter(start):
        def body(r, carry):
            for kk in range(2):
                cp = _tile_copy(f_ref, r * SUBLANES, xs_hbm, dest_ref[kk * n_tok + base + r], sem)
                if start:
                    cp.start(priority=kk)
                else:
                    cp.wait()
            return carry
        lax.fori_loop(0, tm, body, 0, unroll=8)

    scatter(True)
    scatter(False)

    @pl.when(jnp.logical_and(bb == pl.num_programs(0) - 1, i == pl.num_programs(1) - 1))
    def _():
        zero_ref[...] = jnp.zeros_like(zero_ref)
        n_slots = xs_hbm.shape[0] // SUBLANES
        last = MOE_EXPERTS - 1

        blk_tiles = zero_ref.shape[0]

        def fill(start):
            def row(r, c2):
                cp = _tile_copy(zero_ref, 0, xs_hbm, r * SUBLANES, zsem)
                if start:
                    cp.start()
                else:
                    cp.wait()
                return c2

            def block(j, c2):
                cp = pltpu.make_async_copy(zero_ref, xs_hbm.at[pl.ds(pl.multiple_of(j * blk_tiles, blk_tiles),
                                                                     blk_tiles), :], bsem)
                if start:
                    cp.start()
                else:
                    cp.wait()
                return c2

            def run(g, c2):
                cp = pltpu.make_async_copy(
                    zero_ref.at[pl.ds(0, PAD_RUN * SUBLANES), :],
                    xs_hbm.at[pl.ds(pl.multiple_of(g * (PAD_RUN * SUBLANES), PAD_RUN * SUBLANES),
                                    PAD_RUN * SUBLANES), :], rsem)
                if start:
                    cp.start()
                else:
                    cp.wait()
                return c2

            def per_expert(e, carry):
                lo, hi = start_ref[e] + cnt_ref[e], start_ref[e] + padded_ref[e]
                mid = jnp.minimum((lo + PAD_RUN - 1) // PAD_RUN * PAD_RUN, hi)
                carry = lax.fori_loop(lo, mid, row, carry)
                return lax.fori_loop(mid // PAD_RUN, hi // PAD_RUN, run, carry)
            lax.fori_loop(0, MOE_EXPERTS, per_expert, 0)
            used_blocks = (start_ref[last] + padded_ref[last]) // (blk_tiles // SUBLANES)
            lax.fori_loop(used_blocks, n_slots * SUBLANES // blk_tiles, block, 0)

        fill(True)
        fill(False)


def _dispatch(dest8, counts, pad_start, padded, f3, n_slots):
    b, rows, _ = f3.shape
    n_t = rows // (TM * SUBLANES)
    return pl.pallas_call(
        _disp_kernel,
        grid_spec=pltpu.PrefetchScalarGridSpec(
            num_scalar_prefetch=4,
            grid=(b, n_t),
            in_specs=[pl.BlockSpec((TM * SUBLANES, LANES), lambda bb, i, *_: (bb * n_t + i, 0))],
            out_specs=pl.BlockSpec(memory_space=pl.ANY),
            scratch_shapes=[pltpu.VMEM((MOE_BLOCK * SUBLANES, LANES), F32), pltpu.SemaphoreType.DMA(()),
                            pltpu.SemaphoreType.DMA(()), pltpu.SemaphoreType.DMA(()),
                            pltpu.SemaphoreType.DMA(())],
        ),
        out_shape=jax.ShapeDtypeStruct((n_slots * SUBLANES, LANES), F32),
        compiler_params=_cparams(("arbitrary", "arbitrary")),
        name="moe_dispatch",
    )(dest8, counts, pad_start, padded, f3.reshape(b * rows, LANES))


def _moe_kernel(be_ref, nu_ref, xs_ref, w1_ref, w3_ref, w2_ref, y_ref, w1b, w3b, w2b):
    j = pl.program_id(0)
    n_used = nu_ref[0]
    blk = xs_ref.shape[0] // SUBLANES

    @pl.when(jnp.logical_and(j < n_used, jnp.logical_or(j == 0, be_ref[j] != be_ref[jnp.maximum(j - 1, 0)])))
    def _():
        w1b[...] = w1_ref[...].astype(BF16)
        w3b[...] = w3_ref[...].astype(BF16)
        w2b[...] = w2_ref[...].astype(BF16)

    @pl.when(j < n_used)
    def _():
        wide = 2 * LANES
        h1 = h3 = None
        for c in range(w1b.shape[0] // wide):
            xc = jnp.concatenate([xs_ref[pl.ds(2 * c + s, blk, stride=SUBLANES), :] for s in range(2)],
                                 axis=1).astype(BF16)
            d1 = _dot(xc, w1b[c * wide:(c + 1) * wide, :])
            d3 = _dot(xc, w3b[c * wide:(c + 1) * wide, :])
            h1, h3 = (d1, d3) if c == 0 else (h1 + d1, h3 + d3)
        hidden = (_silu(h1) * h3).astype(BF16)
        for c in range(w2b.shape[1] // wide):
            yc = _dot(hidden, w2b[:, c * wide:(c + 1) * wide])
            for s in range(2):
                y_ref[pl.ds(2 * c + s, blk, stride=SUBLANES), :] = yc[:, s * LANES:(s + 1) * LANES]

    @pl.when(j >= n_used)
    def _():
        y_ref[...] = jnp.zeros_like(y_ref)


def _moe(xs, block_e, n_used, w1, w3, w2, layer):
    n_blocks = block_e.shape[0]
    d, hid = w1.shape[-2:]
    wmap = lambda j, be, nu: (layer, be[j], 0, 0)
    tiles = MOE_BLOCK * SUBLANES
    return pl.pallas_call(
        _moe_kernel,
        grid_spec=pltpu.PrefetchScalarGridSpec(
            num_scalar_prefetch=2,
            grid=(n_blocks,),
            in_specs=[
                pl.BlockSpec((tiles, LANES), lambda j, be, nu: (jnp.minimum(j, jnp.maximum(nu[0] - 1, 0)), 0)),
                pl.BlockSpec((None, None, d, hid), wmap),
                pl.BlockSpec((None, None, d, hid), wmap),
                pl.BlockSpec((None, None, hid, d), wmap),
            ],
            out_specs=pl.BlockSpec((tiles, LANES), lambda j, be, nu: (j, 0)),
            scratch_shapes=[pltpu.VMEM((d, hid), BF16), pltpu.VMEM((d, hid), BF16), pltpu.VMEM((hid, d), BF16)],
        ),
        out_shape=jax.ShapeDtypeStruct((n_blocks * tiles, LANES), F32),
        compiler_params=_cparams(("arbitrary",)),
        name="moe_experts",
    )(block_e, n_used, xs, w1, w3, w2)


def _comb_kernel(dn_alpha, dest_ref, y_hbm, x1_ref, rw_ref, mod_ref, lng_ref, lnb_ref, o_ref, ybuf, sems):
    bb = pl.program_id(0)
    i = pl.program_id(1)
    tm = x1_ref.shape[0]
    n_steps = pl.num_programs(0) * pl.num_programs(1)
    n_tok = n_steps * tm
    step = bb * pl.num_programs(1) + i
    slot = step % 2

    def gather(tile, buf, start):
        def body(r, carry):
            for kk in range(2):
                cp = _tile_copy(y_hbm, dest_ref[kk * n_tok + tile * tm + r], ybuf.at[buf, kk], r * SUBLANES,
                                sems.at[buf, kk])
                if start:
                    cp.start(priority=kk)
                else:
                    cp.wait()
            return carry
        lax.fori_loop(0, tm, body, 0, unroll=8)

    @pl.when(step == 0)
    def _():
        gather(0, 0, True)

    @pl.when(step + 1 < n_steps)
    def _():
        gather(step + 1, 1 - slot, True)

    gather(step, slot, False)
    rw = rw_ref[...]
    y = (rw[:, 0:1] * _rows_from_tiles(ybuf.at[slot, 0], tm)
         + rw[:, 1:2] * _rows_from_tiles(ybuf.at[slot, 1], tm))
    gate2 = mod_ref[5:6, :]
    o_ref[...] = _ln(dn_alpha * x1_ref[...] + gate2 * y) * lng_ref[...] + lnb_ref[...]


def _combine(dest, y_slots, x1, rw, modtab, ln_g, ln_b, dn_alpha, seg_of_tile):
    b, n, d = x1.shape
    n_t = n // TM
    row = lambda bb, i, dr: (bb, i, 0)
    const2 = lambda bb, i, dr: (0, 0)
    return pl.pallas_call(
        functools.partial(_comb_kernel, dn_alpha),
        grid_spec=pltpu.PrefetchScalarGridSpec(
            num_scalar_prefetch=1,
            grid=(b, n_t),
            in_specs=[
                pl.BlockSpec(memory_space=pl.ANY),
                pl.BlockSpec((None, TM, d), row),
                pl.BlockSpec((None, TM, LANES), row),
                pl.BlockSpec((None, None, 6, d), lambda bb, i, dr: (bb, seg_of_tile(i), 0, 0)),
                pl.BlockSpec((1, d), const2), pl.BlockSpec((1, d), const2),
            ],
            out_specs=pl.BlockSpec((None, TM, d), row),
            scratch_shapes=[pltpu.VMEM((2, 2, TM * SUBLANES, LANES), F32), pltpu.SemaphoreType.DMA((2, 2))],
        ),
        out_shape=jax.ShapeDtypeStruct((b, n, d), F32),
        compiler_params=_cparams(("arbitrary", "arbitrary")),
        name="moe_combine",
    )(dest, y_slots, x1, rw, modtab, ln_g, ln_b)


def _dispatch_tables(info, tile_counts, n_blocks):
    b, n_t, _, tm = info.shape
    info = info.astype(jnp.int32)
    expert = jnp.moveaxis(info[:, :, 0:2, :], 2, 0).reshape(2, b * n_t, tm)
    rank = jnp.moveaxis(info[:, :, 2:4, :], 2, 0).reshape(2, b * n_t, tm)
    counts = jnp.sum(tile_counts, axis=0)
    padded = (counts + MOE_BLOCK - 1) // MOE_BLOCK * MOE_BLOCK
    pad_end = jnp.cumsum(padded)
    pad_start = pad_end - padded
    tile_base = pad_start[None, :] + jnp.cumsum(tile_counts, axis=0) - tile_counts
    onehot = expert[..., None] == jnp.arange(MOE_EXPERTS, dtype=jnp.int32)
    dest = jnp.sum(jnp.where(onehot, tile_base[None, :, None, :], 0), axis=-1) + rank
    dest8 = (dest * SUBLANES).reshape(-1).astype(jnp.int32)
    block_start = jnp.arange(n_blocks, dtype=jnp.int32) * MOE_BLOCK
    block_e = jnp.minimum(jnp.sum(pad_end[None, :] <= block_start[:, None], axis=1), MOE_EXPERTS - 1).astype(jnp.int32)
    n_used = (pad_end[-1] // MOE_BLOCK).astype(jnp.int32).reshape(1)
    i32 = lambda v: v.astype(jnp.int32)
    return dest8, i32(counts), i32(pad_start), i32(padded), block_e, n_used


def _pad_heads(w, width):
    lead = w.shape[:-1]
    w = w.reshape(*lead, HEADS, width)
    w = jnp.pad(w, [(0, 0)] * len(lead) + [(0, 0), (0, LANES - width)])
    return w.reshape(*lead, PAD_W)


def _expand_matrix(width):
    e = np.zeros((HEADS * width, PAD_W), np.float32)
    for h in range(HEADS):
        e[h * width + np.arange(width), h * LANES + np.arange(width)] = 1.0
    return jnp.asarray(e, BF16)


def _rope_tables(n_ctx, n_lat):
    n_freq = DA_DH // 4
    inv_freq = ROPE_BASE ** (-jnp.arange(n_freq, dtype=F32) / n_freq)
    pos = jnp.arange(n_lat, dtype=jnp.int32)
    ang_r = (pos // GRID_W).astype(F32)[:, None] * inv_freq
    ang_c = (pos % GRID_W).astype(F32)[:, None] * inv_freq
    ang = jnp.concatenate([ang_r, ang_r, ang_c, ang_c] * 2, axis=-1)
    cos, sin = jnp.cos(ang), jnp.sin(ang)
    first = (jnp.arange(LANES) % 32) < 16
    sin_a = jnp.where(first, -sin, 0.0)
    sin_b = jnp.where(first, 0.0, sin)
    ctx_rows = lambda v: jnp.full((n_ctx, LANES), v, F32)
    return jnp.stack([jnp.concatenate([ctx_rows(1.0), cos]), jnp.concatenate([ctx_rows(0.0), sin_a]),
                      jnp.concatenate([ctx_rows(0.0), sin_b])])


def kernel(x, c, ctx, c_ctx, w_ada, b_ada, w_in, da_lambda, da_norm, ml_conv_w, ml_conv_b, ml_ib, ml_fb,
           ml_norm, gl_wa, gl_ba, gl_norm, w_out, ln_mix_g, ln_mix_b, ln_ffn_g, ln_ffn_b,
           moe_wg, moe_we, moe_w1, moe_w3, moe_w2):
    b, s, d = x.shape
    n_ctx = ctx.shape[1]
    n = n_ctx + s
    depth = w_in.shape[0]
    assert n_ctx % TM == 0 and s % (TM * ATTN_NQ) == 0 and d % LANES == 0 and b % SCAN_NB == 0
    dn_alpha = (2 * depth) ** 0.25

    rows = -(-(b + 1) // SUBLANES) * SUBLANES
    cc = jnp.concatenate([c, c_ctx[None, :], jnp.zeros((rows - b - 1, d), F32)], axis=0)
    mod = _ada(cc, w_ada, b_ada).reshape(depth, rows, 6, d)

    x_ctx, x_lat, lat_off = ctx, x, 0
    rope = _rope_tables(n_ctx, s)
    e64, e32 = _expand_matrix(64), _expand_matrix(32)

    for l in range(depth):
        last = l == depth - 1
        tile0 = n_ctx // TM if last else 0
        lam_init = 0.8 - 0.6 * math.exp(-0.3 * l)
        modtab = jnp.stack([jnp.broadcast_to(mod[l, b], (b, 6, d)), mod[l, :b]], axis=1)

        wi = w_in[l]
        o = np.cumsum([0, 512, 512, 512, 512, 256, 256, 8, 8, 128, 128, 256, 256, 32])
        misc_w = jnp.concatenate([wi[:, o[6]:o[8]], wi[:, o[12]:o[13]],
                                  jnp.zeros((d, LANES - 16 - 2 * GL_RANK), F32)], axis=1)
        w_in_p = jnp.concatenate([wi[:, :o[6]], wi[:, o[8]:o[12]], misc_w], axis=1).astype(BF16)

        (qa, ka, va, mq, mk, mv, mo, gq, gk, gv, gr, misc) = _in_proj(
            x_ctx, x_lat, lat_off, s, modtab, w_in_p, rope, ml_conv_w[l], ml_conv_b[l].reshape(1, -1), e64, e32, n_ctx)

        n_ctx_tiles = n_ctx // TM
        attend = functools.partial(_attention, qa, ka, va, da_lambda[l], da_norm[l].reshape(1, -1), lam_init)
        da_ctx = None if last else attend(0, n_ctx_tiles, n_ctx, 1)
        da_lat = attend(n_ctx_tiles, s // TM, n, ATTN_NQ)

        gate_bias = jnp.concatenate([ml_ib[l].reshape(-1), ml_fb[l].reshape(-1),
                                     jnp.zeros((LANES - 16,), F32)]).reshape(1, LANES)
        mhf, mhb = _mlstm(mq, mk, mv, misc, gate_bias, n_ctx)

        wa_p = jnp.stack([jnp.pad(_pad_heads(gl_wa[l, dd], GL_DK),
                                  ((M_GA + dd * GL_RANK, LANES - M_GA - (dd + 1) * GL_RANK), (0, 0)))
                          for dd in range(2)])
        ba_p = _pad_heads(gl_ba[l], GL_DK).reshape(2, 1, PAD_W)
        ghf, ghb = _gla(gq, gk, gv, misc, wa_p.astype(BF16), ba_p, n_ctx)

        wo = w_out[l]
        wo_p = jnp.concatenate([
            wo[:512],
            jnp.pad(wo[512:768].reshape(HEADS, 64, d), ((0, 0), (0, 64), (0, 0))).reshape(PAD_W, d),
            jnp.pad(wo[768:1024].reshape(HEADS, 64, d), ((0, 0), (0, 64), (0, 0))).reshape(PAD_W, d),
        ], axis=0).astype(BF16)
        w_route = jnp.concatenate([moe_wg[l], moe_we[l],
                                   jnp.zeros((d, LANES - MOE_GROUPS - MOE_EXPERTS), F32)], axis=1)
        w_route_top = _bf16_part(w_route)
        w_route = jnp.stack([w_route_top.astype(BF16), (w_route - w_route_top).astype(BF16)])
        ml_g = _pad_heads(jnp.tile(ml_norm[l], HEADS), ML_DH).reshape(1, PAD_W)
        gl_g = _pad_heads(jnp.tile(gl_norm[l], HEADS), GL_DV).reshape(1, PAD_W)
        x1, f3, info, cnt, rw = _out_proj(x_ctx, x_lat, lat_off, n, da_ctx, da_lat, mhf, mhb, mo, ghf, ghb, gr, modtab,
                                          ln_mix_g[l].reshape(1, d), ln_mix_b[l].reshape(1, d), ml_g, gl_g,
                                          wo_p, w_route, dn_alpha, n_ctx, tile0)

        n_rows = x1.shape[1]
        n_t = n_rows // TM
        a_all = b * n_rows * 2
        n_blocks = -(-(a_all + MOE_EXPERTS * (MOE_BLOCK - 1)) // MOE_BLOCK)
        tile_counts = cnt.reshape(b * n_t, SUBLANES, LANES)[:, 0, MOE_GROUPS:MOE_GROUPS + MOE_EXPERTS]
        dest8, counts, pad_start, padded, block_e, n_used = _dispatch_tables(
            info.reshape(b, n_t, SUBLANES, TM), tile_counts.astype(jnp.int32), n_blocks)
        xs = _dispatch(dest8, counts, pad_start, padded, f3, n_blocks * MOE_BLOCK)
        y_slots = _moe(xs, block_e, n_used, moe_w1, moe_w3, moe_w2, l)
        n_ctx_tiles_here = 0 if last else n_ctx // TM
        seg = lambda i: jnp.where(i >= n_ctx_tiles_here, 1, 0)
        x_new = _combine(dest8, y_slots, x1, rw, modtab, ln_ffn_g[l].reshape(1, d), ln_ffn_b[l].reshape(1, d),
                         dn_alpha, seg)
        x_ctx, x_lat, lat_off = x_new, x_new, n_ctx // TM
    return x_new
```

```python
import functools
import math

import numpy as np
import jax
import jax.numpy as jnp
from jax import lax
from jax.experimental import pallas as pl
from jax.experimental.pallas import tpu as pltpu

F32 = jnp.float32
BF16 = jnp.bfloat16
HI = lax.Precision.HIGHEST

GRID_W = 64
HEADS = 4
DA_DH = 64
ML_DH = 64
GL_DK = 32
GL_DV = 64
GL_RANK = 16
GL_NORMALIZER = 16.0
CHUNK = 64
SUB = 16
ROPE_BASE = 10000.0
MOE_GROUPS = 4
MOE_PER_GROUP = 8
MOE_EXPERTS = 32
LN_EPS = 1e-6

LANES = 128
SUBLANES = 8
VMEM_LIMIT = 48 * 1024 * 1024

TM = 256
MOE_BLOCK = 512
PAD_RUN = 32
SCAN_NB = 8
ATTN_NQ = 4
PAD_W = HEADS * LANES

C_QA, C_KA, C_VA, C_MQK, C_MV, C_MO = 0, 512, 1024, 1536, 2048, 2304
C_GQ, C_GK, C_GV, C_GR, C_MISC, IN_WP = 2560, 2688, 2816, 3072, 3328, 3456
M_IG, M_FG, M_GA = 0, 8, 16


def _cparams(sem):
    return pltpu.CompilerParams(dimension_semantics=sem, vmem_limit_bytes=VMEM_LIMIT)


def _sigmoid(x):
    return 1.0 / (1.0 + jnp.exp(-x))


def _silu(x):
    return x * _sigmoid(x)


def _log_sigmoid(x):
    return jnp.minimum(x, 0.0) - jnp.log(1.0 + jnp.exp(-jnp.abs(x)))


def _ln(x):
    mu = jnp.mean(x, axis=-1, keepdims=True)
    xc = x - mu
    var = jnp.mean(xc * xc, axis=-1, keepdims=True)
    return xc * lax.rsqrt(var + LN_EPS)


def _bf16_part(x):
    bits = lax.bitcast_convert_type(x, jnp.int32) & jnp.int32(-65536)
    return lax.bitcast_convert_type(bits, F32)


def _dot(a, b):
    return jnp.dot(a, b, preferred_element_type=F32)


def _dot_hi(a, b):
    return jnp.dot(a, b, precision=HI, preferred_element_type=F32)


def _cumsum_dot(a, b, split_lhs=False):
    x = a if split_lhs else b
    x0 = _bf16_part(x)
    r1 = x - x0
    x1 = _bf16_part(r1)
    x2 = r1 - x1
    parts = [p.astype(BF16) for p in (x0, x1, x2)]
    if split_lhs:
        return _dot(parts[0], b) + (_dot(parts[1], b) + _dot(parts[2], b))
    return _dot(a, parts[0]) + (_dot(a, parts[1]) + _dot(a, parts[2]))


def _dot_nt(a, b):
    return lax.dot_general(a, b, (((1,), (1,)), ((), ())), preferred_element_type=F32)


def _dot_tn(a, b):
    return lax.dot_general(a, b, (((0,), (0,)), ((), ())), preferred_element_type=F32)


def _ada_kernel(c_ref, w_ref, b_ref, o_ref):
    o_ref[...] = _dot_hi(_silu(c_ref[...]), w_ref[...]) + b_ref[...]


def _ada(cc, w_ada, b_ada):
    depth, d, n6 = w_ada.shape
    tn = 1536
    rows = cc.shape[0]
    return pl.pallas_call(
        _ada_kernel,
        grid=(depth, n6 // tn),
        in_specs=[
            pl.BlockSpec((rows, d), lambda l, j: (0, 0)),
            pl.BlockSpec((None, d, tn), lambda l, j: (l, 0, j)),
            pl.BlockSpec((None, 1, tn), lambda l, j: (l, 0, j)),
        ],
        out_specs=pl.BlockSpec((None, rows, tn), lambda l, j: (l, 0, j)),
        out_shape=jax.ShapeDtypeStruct((depth, rows, n6), F32),
        compiler_params=_cparams(("arbitrary", "arbitrary")),
        name="ada_mod",
    )(cc, w_ada, b_ada.reshape(depth, 1, n6))


def _in_kernel(n_ctx_tiles, xc_ref, x_ref, xp_ref, xn_ref, mod_ref, w_ref, rope_ref, cw_ref, cb_ref,
               e64_ref, e32_ref,
               qa_ref, ka_ref, va_ref, mq_ref, mk_ref, mv_ref, mo_ref,
               gq_ref, gk_ref, gv_ref, gr_ref, misc_ref, pext_ref):
    i = pl.program_id(1)
    n_tiles = pl.num_programs(1)
    tm = x_ref.shape[0]
    shift, scale = mod_ref[0:1, :], mod_ref[1:2, :]

    def prenorm(x):
        return (_ln(x) * (1.0 + scale) + shift).astype(BF16)

    half = tm // 2
    halves = [slice(0, half), slice(half, tm)]
    groups = ((C_QA, C_KA), (C_KA, C_VA), (C_VA, C_MQK), (C_MQK, C_MV), (C_MV, C_MO), (C_MO, C_GQ),
              (C_GQ, C_GK), (C_GK, C_GV), (C_GV, C_GR), (C_GR, C_MISC), (C_MISC, IN_WP))
    e64 = e64_ref[...]
    e32 = e32_ref[...]
    lane = lax.broadcasted_iota(jnp.int32, (half, PAD_W), 1)
    ones_col = jnp.where(lane % LANES >= ML_DH, 1.0, 0.0)
    seg_first = jnp.logical_or(i == 0, i == n_ctx_tiles)
    seg_last = jnp.logical_or(i == n_ctx_tiles - 1, i == n_tiles - 1)
    keep_prev = jnp.where(seg_first, 0.0, 1.0)
    keep_next = jnp.where(seg_last, 0.0, 1.0)

    hs = [prenorm(jnp.where(i < n_ctx_tiles, xc_ref[r, :], x_ref[r, :])) for r in halves]
    halo = jnp.concatenate([prenorm(xp_ref[...]), prenorm(xn_ref[...])], axis=0)
    parts = [{lo: _dot(h, w_ref[:, lo:hi]) for lo, hi in groups} for h in hs]
    p_halo = _dot(halo, w_ref[:, C_MQK:C_MV])
    pext_ref[0:SUBLANES, :] = p_halo[0:SUBLANES] * keep_prev
    pext_ref[SUBLANES + tm:2 * SUBLANES + tm, :] = p_halo[SUBLANES:2 * SUBLANES] * keep_next
    for r, part in zip(halves, parts):
        pext_ref[SUBLANES + r.start:SUBLANES + r.stop, :] = part[C_MQK]

    for r, part in zip(halves, parts):
        cos, sin_a, sin_b = rope_ref[0, r, :], rope_ref[1, r, :], rope_ref[2, r, :]
        for lo, out_ref, mult in ((C_QA, qa_ref, DA_DH ** -0.5 * math.log2(math.e)), (C_KA, ka_ref, 1.0)):
            for hh in range(HEADS):
                xh = part[lo][:, hh * LANES:(hh + 1) * LANES]
                rot = xh * cos + pltpu.roll(xh, LANES - 16, 1) * sin_a + pltpu.roll(xh, 16, 1) * sin_b
                out_ref[r, hh * LANES:(hh + 1) * LANES] = (rot * mult).astype(BF16)
        va_ref[r, :] = part[C_VA].astype(BF16)
        mv_ref[r, :] = (_dot(part[C_MV].astype(BF16), e64) + ones_col).astype(BF16)
        mo_ref[r, :] = _dot(part[C_MO].astype(BF16), e64).astype(BF16)
        gq_ref[r, :] = _dot((part[C_GQ] * GL_DK ** -0.5).astype(BF16), e32).astype(BF16)
        gk_ref[r, :] = _dot(part[C_GK].astype(BF16), e32).astype(BF16)
        gv_ref[r, :] = _dot(part[C_GV].astype(BF16), e64).astype(BF16)
        gr_ref[r, :] = _dot(part[C_GR].astype(BF16), e64).astype(BF16)
        misc_ref[r, :] = part[C_MISC]

    for r, part in zip(halves, parts):
        y = (cw_ref[0:1, :] * pext_ref[SUBLANES - 1 + r.start:SUBLANES - 1 + r.stop, :] + cw_ref[1:2, :] * part[C_MQK]
             + cw_ref[2:3, :] * pext_ref[SUBLANES + 1 + r.start:SUBLANES + 1 + r.stop, :] + cb_ref[...])
        y = _silu(y)
        mq_ref[r, :] = _dot((y[:, :256] * ML_DH ** -0.5).astype(BF16), e64).astype(BF16)
        mk_ref[r, :] = _dot(y[:, 256:].astype(BF16), e64).astype(BF16)


def _in_proj(x_ctx, x_lat, lat_off, n_lat, modtab, w_in_p, rope, conv_w, conv_b, e64, e32, n_ctx):
    b, _, d = x_lat.shape
    n = n_ctx + n_lat
    n_tiles = n // TM
    n_ctx_tiles = n_ctx // TM
    assert n_ctx_tiles == 1
    hb = TM // SUBLANES
    n_lat_hb = n_lat // SUBLANES
    lat_tile = lambda i: jnp.maximum(i - n_ctx_tiles, 0)
    row = lambda bb, i: (bb, i, 0)
    const2 = lambda bb, i: (0, 0)
    outs = [jax.ShapeDtypeStruct((b, n, PAD_W), BF16)] * 11 + [jax.ShapeDtypeStruct((b, n, LANES), F32)]
    out_specs = [pl.BlockSpec((None, TM, PAD_W), row)] * 11 + [pl.BlockSpec((None, TM, LANES), row)]
    return pl.pallas_call(
        functools.partial(_in_kernel, n_ctx_tiles),
        grid=(b, n_tiles),
        in_specs=[
            pl.BlockSpec((None, TM, d), lambda bb, i: (bb, jnp.minimum(i, n_ctx_tiles - 1), 0)),
            pl.BlockSpec((None, TM, d), lambda bb, i: (bb, lat_tile(i) + lat_off, 0)),
            pl.BlockSpec((None, SUBLANES, d),
                         lambda bb, i: (bb, jnp.maximum(lat_tile(i) * hb - 1, 0) + lat_off * hb, 0)),
            pl.BlockSpec((None, SUBLANES, d),
                         lambda bb, i: (bb, jnp.minimum((lat_tile(i) + 1) * hb, n_lat_hb - 1) + lat_off * hb, 0)),
            pl.BlockSpec((None, None, 6, d), lambda bb, i: (bb, jnp.where(i >= n_ctx_tiles, 1, 0), 0, 0)),
            pl.BlockSpec((d, IN_WP), const2),
            pl.BlockSpec((3, TM, LANES), lambda bb, i: (0, i, 0)),
            pl.BlockSpec((3, 512), const2),
            pl.BlockSpec((1, 512), const2),
            pl.BlockSpec((256, PAD_W), const2),
            pl.BlockSpec((128, PAD_W), const2),
        ],
        out_specs=out_specs,
        out_shape=outs,
        scratch_shapes=[pltpu.VMEM((TM + 2 * SUBLANES, 512), F32)],
        compiler_params=_cparams(("parallel", "arbitrary")),
        name="in_proj",
    )(x_ctx, x_lat, x_lat, x_lat, modtab, w_in_p, rope, conv_w, conv_b, e64, e32)


def _attn_kernel(lam_init, n_q, *refs):
    q_refs = refs[:n_q]
    k_ref, v_ref, lamp_ref, g_ref, o_ref = refs[n_q:]
    lp = lamp_ref[...]
    lam = (jnp.exp(jnp.sum(lp[0:1] * lp[1:2], axis=1, keepdims=True))
           - jnp.exp(jnp.sum(lp[2:3] * lp[3:4], axis=1, keepdims=True)) + lam_init)
    k = k_ref[...]
    v = jnp.concatenate([v_ref[...], jnp.ones(v_ref.shape, BF16)], axis=1)
    tq = q_refs[0].shape[0]
    lane = lax.broadcasted_iota(jnp.int32, (tq, LANES), 1)
    scores = []
    for q_ref in q_refs:
        q = q_ref[...]
        zero = jnp.zeros_like(q)
        scores.append((_dot_nt(jnp.where(lane < DA_DH, q, zero), k), _dot_nt(jnp.where(lane >= DA_DH, q, zero), k)))
    for t, (s1, s2) in enumerate(scores):
        p1 = _dot(jnp.exp2(s1 - jnp.max(s1, axis=1, keepdims=True)).astype(BF16), v)
        p2 = _dot(jnp.exp2(s2 - jnp.max(s2, axis=1, keepdims=True)).astype(BF16), v)
        o = p1[:, 0:LANES] / p1[:, LANES:2 * LANES] - p2[:, 0:LANES] * (lam / p2[:, LANES:2 * LANES])
        y = o * lax.rsqrt(jnp.mean(o * o, axis=1, keepdims=True) + LN_EPS) * g_ref[...]
        o_ref[t * tq:(t + 1) * tq, :] = (y * (1.0 - lam_init)).astype(BF16)


def _attention(qa, ka, va, lam_p, norm_g, lam_init, tile0, n_tiles, n_keys, n_q):
    b = qa.shape[0]
    tq = TM
    q_spec = lambda t: pl.BlockSpec((None, tq, LANES), lambda bb, h, i: (bb, tile0 + i * n_q + t, h))
    kv_spec = pl.BlockSpec((None, n_keys, LANES), lambda bb, h, i: (bb, 0, h))
    return pl.pallas_call(
        functools.partial(_attn_kernel, lam_init, n_q),
        grid=(b, HEADS, n_tiles // n_q),
        in_specs=[q_spec(t) for t in range(n_q)] + [
            kv_spec, kv_spec,
            pl.BlockSpec((4, DA_DH), lambda bb, h, i: (0, 0)),
            pl.BlockSpec((1, LANES), lambda bb, h, i: (0, 0)),
        ],
        out_specs=pl.BlockSpec((None, n_q * tq, LANES), lambda bb, h, i: (bb, i, h)),
        out_shape=jax.ShapeDtypeStruct((b, n_tiles * tq, PAD_W), BF16),
        compiler_params=_cparams(("parallel", "parallel", "arbitrary")),
        name="diff_attn",
    )(*([qa] * n_q), ka, va, lam_p, norm_g)


def _chunk_maps(n_ctx_chunks, n_chunks):
    def fwd(bb, t):
        return (bb, t, 0)

    def bwd(bb, t):
        return (bb, jnp.where(t < n_ctx_chunks, n_ctx_chunks - 1 - t, n_chunks - 1 + n_ctx_chunks - t), 0)

    return fwd, bwd


def _tri(direction):
    l = lax.broadcasted_iota(jnp.int32, (CHUNK, CHUNK), 0)
    s = lax.broadcasted_iota(jnp.int32, (CHUNK, CHUNK), 1)
    return (s <= l) if direction == 0 else (s >= l)


def _mlstm_kernel(qf_ref, kf_ref, vf_ref, gf_ref, qb_ref, kb_ref, vb_ref, gb_ref, bias_ref, sel_ref,
                  hf_ref, hb_ref, c_ref, m_ref):
    t = pl.program_id(1)

    @pl.when(t == 0)
    def _():
        c_ref[...] = jnp.zeros_like(c_ref)
        m_ref[...] = jnp.zeros_like(m_ref)

    lane = lax.broadcasted_iota(jnp.int32, (CHUNK, LANES), 1)
    is_fg = jnp.logical_and(lane >= M_FG, lane < M_GA)
    dirs = ((qf_ref, kf_ref, vf_ref, gf_ref, hf_ref), (qb_ref, kb_ref, vb_ref, gb_ref, hb_ref))
    n_b = qf_ref.shape[0]
    masks = [_tri(d) for d in range(2)]
    units = [(d, nb, h) for d in range(2) for nb in range(n_b) for h in range(HEADS)]

    pre = {}
    for d in range(2):
        tri = jnp.where(masks[d], 1.0, 0.0).astype(BF16)
        gs, cums = [], []
        for nb in range(n_b):
            g = dirs[d][3][nb] + bias_ref[...]
            g = jnp.where(is_fg, _log_sigmoid(g), g)
            gs.append(g)
            cums.append(_cumsum_dot(tri, g))
        b_all = _cumsum_dot(jnp.concatenate(cums, axis=0), sel_ref[d, 0], split_lhs=True)
        i_all = _cumsum_dot(jnp.concatenate(gs, axis=0), sel_ref[d, 1], split_lhs=True)
        for nb in range(n_b):
            rows = slice(nb * CHUNK, (nb + 1) * CHUNK)
            pre[d, nb] = (b_all[rows], i_all[rows], gs[nb].T, cums[nb].T)

    qk, qc = {}, {}
    for u in units:
        d, nb, h = u
        hs = slice(h * LANES, (h + 1) * LANES)
        q = dirs[d][0][nb, :, hs]
        qk[u] = _dot_nt(q, dirs[d][1][nb, :, hs])
        qc[u] = _dot(q, c_ref[nb, d, h].astype(BF16))

    gate = {}
    for u in units:
        d, nb, h = u
        hs = slice(h * LANES, (h + 1) * LANES)
        b_all, i_all, g_t, cum_t = pre[d, nb]
        ci, cf = M_IG + HEADS * d + h, M_FG + HEADS * d + h
        last = CHUNK - 1 if d == 0 else 0
        b_rep = b_all[:, hs]
        i_rep = i_all[:, hs]
        row = g_t[ci:ci + 1, :] - cum_t[cf:cf + 1, :]
        m_prev = m_ref[nb, d, h][0:1, :]
        log_d = jnp.where(masks[d], b_rep[:, 0:CHUNK] + row, -jnp.inf)
        log_a = b_rep + m_prev
        m_t = jnp.maximum(log_a, jnp.max(log_d, axis=1, keepdims=True))
        m_new = m_t[last:last + 1, :]
        b_last = b_rep[last:last + 1, :]
        gate[u] = (jnp.exp(log_d - m_t[:, 0:CHUNK]), jnp.exp(log_a - m_t), jnp.exp(-m_t),
                   jnp.exp(b_last - b_rep + i_rep - m_new), jnp.exp(b_last + m_prev - m_new), m_new)

    sv, kv = {}, {}
    for u in units:
        d, nb, h = u
        hs = slice(h * LANES, (h + 1) * LANES)
        v = dirs[d][2][nb, :, hs]
        dmat, _, _, w, _, _ = gate[u]
        sv[u] = _dot((qk[u] * dmat).astype(BF16), v)
        kv[u] = _dot_tn((dirs[d][1][nb, :, hs].astype(F32) * w).astype(BF16), v)

    for u in units:
        d, nb, h = u
        hs = slice(h * LANES, (h + 1) * LANES)
        _, a, floor, _, decay, m_new = gate[u]
        num = a * qc[u] + sv[u]
        qn = pltpu.roll(num, ML_DH, 1)
        den = jnp.maximum(jnp.abs(qn), floor)
        dirs[d][4][nb, :, hs] = jnp.where(lane < ML_DH, num / den, 0.0).astype(BF16)
        c_ref[nb, d, h] = decay * c_ref[nb, d, h] + kv[u]
        m_ref[nb, d, h] = jnp.broadcast_to(m_new, (SUBLANES, LANES))


def _gate_select_matrices():
    sel = np.zeros((2, 2, LANES, PAD_W), np.float32)
    for d in range(2):
        for h in range(HEADS):
            sel[d, 0, M_FG + HEADS * d + h, h * LANES:(h + 1) * LANES] = 1.0
            sel[d, 1, M_IG + HEADS * d + h, h * LANES:(h + 1) * LANES] = 1.0
    return jnp.asarray(sel, BF16)


def _mlstm(mq, mk, mv, misc, gate_bias, n_ctx):
    b, n, _ = mq.shape
    n_chunks = n // CHUNK
    fwd, bwd = _chunk_maps(n_ctx // CHUNK, n_chunks)
    wide = lambda m: pl.BlockSpec((SCAN_NB, CHUNK, PAD_W), m)
    thin = lambda m: pl.BlockSpec((SCAN_NB, CHUNK, LANES), m)
    return pl.pallas_call(
        _mlstm_kernel,
        grid=(b // SCAN_NB, n_chunks),
        in_specs=[wide(fwd), wide(fwd), wide(fwd), thin(fwd), wide(bwd), wide(bwd), wide(bwd), thin(bwd),
                  pl.BlockSpec((1, LANES), lambda bb, t: (0, 0)),
                  pl.BlockSpec((2, 2, LANES, PAD_W), lambda bb, t: (0, 0, 0, 0))],
        out_specs=[wide(fwd), wide(bwd)],
        out_shape=[jax.ShapeDtypeStruct((b, n, PAD_W), BF16)] * 2,
        scratch_shapes=[pltpu.VMEM((SCAN_NB, 2, HEADS, LANES, LANES), F32),
                        pltpu.VMEM((SCAN_NB, 2, HEADS, SUBLANES, LANES), F32)],
        compiler_params=_cparams(("parallel", "arbitrary")),
        name="mlstm_scan",
    )(mq, mk, mv, misc, mq, mk, mv, misc, gate_bias, _gate_select_matrices())


def _gla_kernel(qf_ref, kf_ref, vf_ref, gf_ref, qb_ref, kb_ref, vb_ref, gb_ref, wa_ref, ba_ref,
                hf_ref, hb_ref, st_ref):
    t = pl.program_id(1)

    @pl.when(t == 0)
    def _():
        st_ref[...] = jnp.zeros_like(st_ref)

    n_sub = CHUNK // SUB
    row_id = lax.broadcasted_iota(jnp.int32, (CHUNK, LANES), 0)
    row_blk = row_id // SUB
    col_id = lax.broadcasted_iota(jnp.int32, (CHUNK, LANES), 1)
    rows_w = lax.broadcasted_iota(jnp.int32, (CHUNK, PAD_W), 0)
    dirs = ((qf_ref, kf_ref, vf_ref, gf_ref, hf_ref), (qb_ref, kb_ref, vb_ref, gb_ref, hb_ref))
    n_b = qf_ref.shape[0]
    pairs = [(d, nb) for d in range(2) for nb in range(n_b)]
    units = [(d, nb, h) for d, nb in pairs for h in range(HEADS)]
    tris = [jnp.where(_tri(d), 1.0, 0.0).astype(BF16) for d in range(2)]
    causal = [col_id <= row_id, jnp.logical_and(col_id >= row_id, col_id < CHUNK)]

    las = {}
    for d in range(2):
        z = _dot(jnp.concatenate([dirs[d][3][nb] for nb in range(n_b)], axis=0).astype(BF16), wa_ref[d])
        la_all = _log_sigmoid(z + ba_ref[d]) * (1.0 / GL_NORMALIZER)
        for nb in range(n_b):
            las[d, nb] = la_all[nb * CHUNK:(nb + 1) * CHUNK]
    cums = {p: _cumsum_dot(tris[p[0]], las[p]) for p in pairs}

    pre = {}
    for p in pairs:
        d, nb = p
        cum = cums[p]
        excl = cum - las[p]
        last = CHUNK - 1 if d == 0 else 0
        betas = [excl[r:r + 1, :] for r in (range(0, CHUNK, SUB) if d == 0 else range(SUB - 1, CHUNK, SUB))]
        beta_rows = jnp.concatenate([jnp.broadcast_to(bt, (SUB, PAD_W)) for bt in betas], axis=0)
        q = dirs[d][0][nb].astype(F32)
        k = dirs[d][1][nb].astype(F32)
        q_loc = (q * jnp.exp(cum - beta_rows)).astype(BF16)
        q_abs = (q * jnp.exp(cum)).astype(BF16)
        b_last = cum[last:last + 1, :]
        k_end = (k * jnp.exp(b_last - cum)).astype(BF16)
        k_subs = []
        for I in range(n_sub):
            seen = (rows_w < (I + 1) * SUB) if d == 0 else (rows_w >= I * SUB)
            k_subs.append((k * jnp.exp(jnp.where(seen, betas[I] - cum, -jnp.inf))).astype(BF16))
        pre[p] = (q_loc, q_abs, b_last, k_end, k_subs)

    scores, inter, kv = {}, {}, {}
    for u in units:
        d, nb, h = u
        hs = slice(h * LANES, (h + 1) * LANES)
        q_loc, q_abs, _, k_end, k_subs = pre[d, nb]
        kcat = jnp.concatenate([k_subs[I][:, hs] for I in range(n_sub)], axis=0)
        scores[u] = _dot_nt(q_loc[:, hs], kcat)
        inter[u] = _dot_nt(q_abs[:, hs], st_ref[nb, d, h].astype(BF16))
        kv[u] = _dot_tn(dirs[d][2][nb, :, hs], k_end[:, hs])

    att = {}
    for u in units:
        r = scores[u]
        a = None
        for j in range(n_sub // 2):
            pair = r[:, j * LANES:(j + 1) * LANES]
            for I, part in ((2 * j, pair), (2 * j + 1, pltpu.roll(pair, CHUNK, 1))):
                a = part if a is None else jnp.where(row_blk == I, part, a)
        att[u] = jnp.where(causal[u[0]], a, 0.0)[:, 0:CHUNK].astype(BF16)

    av = {u: _dot(att[u], dirs[u[0]][2][u[1], :, u[2] * LANES:(u[2] + 1) * LANES]) for u in units}
    for u in units:
        d, nb, h = u
        hs = slice(h * LANES, (h + 1) * LANES)
        dirs[d][4][nb, :, hs] = (inter[u] + av[u]).astype(BF16)
        st_ref[nb, d, h] = st_ref[nb, d, h] * jnp.exp(pre[d, nb][2][:, hs]) + kv[u]


def _gla(gq, gk, gv, misc, wa_p, ba_p, n_ctx):
    b, n, _ = gq.shape
    n_chunks = n // CHUNK
    fwd, bwd = _chunk_maps(n_ctx // CHUNK, n_chunks)
    wide = lambda m: pl.BlockSpec((SCAN_NB, CHUNK, PAD_W), m)
    thin = lambda m: pl.BlockSpec((SCAN_NB, CHUNK, LANES), m)
    return pl.pallas_call(
        _gla_kernel,
        grid=(b // SCAN_NB, n_chunks),
        in_specs=[wide(fwd), wide(fwd), wide(fwd), thin(fwd), wide(bwd), wide(bwd), wide(bwd), thin(bwd),
                  pl.BlockSpec((2, LANES, PAD_W), lambda bb, t: (0, 0, 0)),
                  pl.BlockSpec((2, 1, PAD_W), lambda bb, t: (0, 0, 0))],
        out_specs=[wide(fwd), wide(bwd)],
        out_shape=[jax.ShapeDtypeStruct((b, n, PAD_W), BF16)] * 2,
        scratch_shapes=[pltpu.VMEM((SCAN_NB, 2, HEADS, LANES, LANES), F32)],
        compiler_params=_cparams(("parallel", "arbitrary")),
        name="gla_scan",
    )(gq, gk, gv, misc, gq, gk, gv, misc, wa_p, ba_p)


def _head_rms_padded(x, width):
    parts = []
    for h in range(HEADS):
        xh = x[:, h * LANES:(h + 1) * LANES]
        ms = jnp.sum(xh * xh, axis=1, keepdims=True) * (1.0 / width)
        parts.append(xh * lax.rsqrt(ms + LN_EPS))
    return jnp.concatenate(parts, axis=1)


def _out_kernel(dn_alpha, n_ctx_tiles, xc_ref, x_ref, dac_ref, dal_ref, mhf_ref, mhb_ref, mo_ref, ghf_ref, ghb_ref, gr_ref,
                mod_ref, lng_ref, lnb_ref, mlg_ref, glg_ref, wo_ref, wr_ref,
                x1_ref, f_ref, info_ref, cnt_ref, rw_ref):
    tm = x_ref.shape[0]
    half = tm // 2
    halves = [slice(0, half), slice(half, tm)]
    gate1, shift2, scale2 = mod_ref[2:3, :], mod_ref[3:4, :], mod_ref[4:5, :]
    is_ctx = pl.program_id(1) < n_ctx_tiles
    lane = lax.broadcasted_iota(jnp.int32, (half, LANES), 1)
    neg = -jnp.inf
    big = jnp.int32(LANES)

    def first_argmax(vals, vmax):
        return jnp.min(jnp.where(vals == vmax, lane, big), axis=1, keepdims=True)

    mixed = []
    for r in halves:
        ml = (_head_rms_padded(mhf_ref[r, :].astype(F32) + mhb_ref[r, :].astype(F32), ML_DH) * mlg_ref[...]
              * _sigmoid(mo_ref[r, :].astype(F32)))
        gl = (_head_rms_padded(ghf_ref[r, :].astype(F32) + ghb_ref[r, :].astype(F32), GL_DV) * glg_ref[...]
              * _silu(gr_ref[r, :].astype(F32)))
        mixed.append((jnp.where(is_ctx, dac_ref[r, :], dal_ref[r, :]), ml.astype(BF16), gl.astype(BF16)))
    ys = [_dot(da, wo_ref[0:PAD_W, :]) + _dot(ml, wo_ref[PAD_W:2 * PAD_W, :])
          + _dot(gl, wo_ref[2 * PAD_W:3 * PAD_W, :]) for da, ml, gl in mixed]
    fs = []
    for r, y in zip(halves, ys):
        x_in = jnp.where(is_ctx, xc_ref[r, :], x_ref[r, :])
        x1 = _ln(dn_alpha * x_in + gate1 * y) * lng_ref[...] + lnb_ref[...]
        x1_ref[r, :] = x1
        f = _ln(x1) * (1.0 + scale2) + shift2
        for s in range(SUBLANES):
            f_ref[pl.ds(r.start * SUBLANES + s, half, stride=SUBLANES), :] = f[:, s * LANES:(s + 1) * LANES]
        fs.append(f)

    all_logits = []
    for f in fs:
        f_top = _bf16_part(f)
        f_hi, f_lo = f_top.astype(BF16), (f - f_top).astype(BF16)
        all_logits.append(_dot(f_hi, wr_ref[0]) + (_dot(f_lo, wr_ref[0]) + _dot(f_hi, wr_ref[1])
                                                   + _dot(f_lo, wr_ref[1])))
    routed = []
    for r, logits in zip(halves, all_logits):
        gl_ = jnp.where(lane < MOE_GROUPS, logits, neg)
        gmax = jnp.max(gl_, axis=1, keepdims=True)
        g_idx = first_argmax(gl_, gmax)
        g_gate = 1.0 / jnp.sum(jnp.exp(gl_ - gmax), axis=1, keepdims=True)
        in_group = jnp.logical_and(lane >= MOE_GROUPS, (lane - MOE_GROUPS) // MOE_PER_GROUP == g_idx)
        el = jnp.where(in_group, logits, neg)
        emax = jnp.max(el, axis=1, keepdims=True)
        e0 = first_argmax(el, emax)
        el2 = jnp.where(lane == e0, neg, el)
        emax2 = jnp.max(el2, axis=1, keepdims=True)
        e1 = first_argmax(el2, emax2)
        esum = jnp.sum(jnp.exp(el - emax), axis=1, keepdims=True)
        p0 = 1.0 / esum
        p1 = jnp.exp(emax2 - emax) / esum
        w0 = g_gate * p0 / (p0 + p1)
        w1 = g_gate * p1 / (p0 + p1)
        rw_ref[r, :] = jnp.where(lane == 0, w0, jnp.where(lane == 1, w1, 0.0))
        routed.append((e0, e1))

    e0 = jnp.concatenate([e[0] for e in routed], axis=0)
    e1 = jnp.concatenate([e[1] for e in routed], axis=0)
    lane = lax.broadcasted_iota(jnp.int32, (tm, LANES), 1)
    oh0 = jnp.where(lane == e0, 1.0, 0.0)
    oh1 = jnp.where(lane == e1, 1.0, 0.0)
    both = (oh0 + oh1).astype(BF16)
    r_id = lax.broadcasted_iota(jnp.int32, (tm, tm), 0)
    c_id = lax.broadcasted_iota(jnp.int32, (tm, tm), 1)
    earlier = jnp.where(c_id < r_id, 1.0, 0.0).astype(BF16)
    before = _dot(earlier, both)
    rank0 = jnp.sum(before * oh0, axis=1, keepdims=True)
    rank1 = jnp.sum(before * oh1, axis=1, keepdims=True)
    cnt_ref[...] = _dot(jnp.ones((SUBLANES, tm), BF16), both)
    info = jnp.where(lane == 0, (e0 - MOE_GROUPS).astype(F32),
                     jnp.where(lane == 1, (e1 - MOE_GROUPS).astype(F32),
                               jnp.where(lane == 2, rank0, jnp.where(lane == 3, rank1, 0.0))))
    info_ref[...] = info.T[0:SUBLANES, :]


def _out_proj(x_ctx, x_lat, lat_off, n, da_ctx, da_lat, mhf, mhb, mo, ghf, ghb, gr, modtab, ln_g, ln_b, ml_g, gl_g,
              wo_p, w_route, dn_alpha, n_ctx, tile0):
    b, _, d = x_lat.shape
    n_ctx_tiles = n_ctx // TM
    n_t = n // TM - tile0
    out_ctx_tiles = n_ctx_tiles - tile0
    row = lambda bb, i: (bb, i + tile0, 0)
    orow = lambda bb, i: (bb, i, 0)
    const2 = lambda bb, i: (0, 0)
    wide = pl.BlockSpec((None, TM, PAD_W), row)
    n_out = n_t * TM
    if da_ctx is None:
        da_ctx = da_lat
    ctx_rows = lambda bb, i: (bb, jnp.minimum(i, max(out_ctx_tiles - 1, 0)), 0)
    lat_rows = lambda bb, i: (bb, jnp.maximum(i - out_ctx_tiles, 0), 0)
    return pl.pallas_call(
        functools.partial(_out_kernel, dn_alpha, out_ctx_tiles),
        grid=(b, n_t),
        in_specs=[
            pl.BlockSpec((None, TM, d), lambda bb, i: (bb, jnp.minimum(i + tile0, n_ctx_tiles - 1), 0)),
            pl.BlockSpec((None, TM, d), lambda bb, i: (bb, jnp.maximum(i + tile0 - n_ctx_tiles, 0) + lat_off, 0)),
            pl.BlockSpec((None, TM, PAD_W), ctx_rows), pl.BlockSpec((None, TM, PAD_W), lat_rows),
            wide, wide, wide, wide, wide, wide,
            pl.BlockSpec((None, None, 6, d), lambda bb, i: (bb, jnp.where(i + tile0 >= n_ctx_tiles, 1, 0), 0, 0)),
            pl.BlockSpec((1, d), const2), pl.BlockSpec((1, d), const2),
            pl.BlockSpec((1, PAD_W), const2), pl.BlockSpec((1, PAD_W), const2),
            pl.BlockSpec((3 * PAD_W, d), const2),
            pl.BlockSpec((2, d, LANES), lambda bb, i: (0, 0, 0)),
        ],
        out_specs=[pl.BlockSpec((None, TM, d), orow),
                   pl.BlockSpec((None, TM * SUBLANES, LANES), orow),
                   pl.BlockSpec((None, SUBLANES, TM), orow),
                   pl.BlockSpec((None, SUBLANES, LANES), orow),
                   pl.BlockSpec((None, TM, LANES), orow)],
        out_shape=[jax.ShapeDtypeStruct((b, n_out, d), F32),
                   jax.ShapeDtypeStruct((b, n_out * SUBLANES, LANES), F32),
                   jax.ShapeDtypeStruct((b, n_t * SUBLANES, TM), F32),
                   jax.ShapeDtypeStruct((b, n_t * SUBLANES, LANES), F32),
                   jax.ShapeDtypeStruct((b, n_out, LANES), F32)],
        compiler_params=_cparams(("parallel", "arbitrary")),
        name="out_proj",
    )(x_ctx, x_lat, da_ctx, da_lat, mhf, mhb, mo, ghf, ghb, gr, modtab, ln_g, ln_b, ml_g, gl_g, wo_p, w_route)


def _tile_copy(src, src_tile, dst, dst_tile, sem):
    aligned = lambda t: t if isinstance(t, int) else pl.multiple_of(t, SUBLANES)
    return pltpu.make_async_copy(src.at[pl.ds(aligned(src_tile), SUBLANES), :],
                                 dst.at[pl.ds(aligned(dst_tile), SUBLANES), :], sem)


def _rows_from_tiles(ref, n_rows):
    return jnp.concatenate([ref[pl.ds(s, n_rows, stride=SUBLANES), :] for s in range(SUBLANES)], axis=1)


def _disp_kernel(dest_ref, cnt_ref, start_ref, padded_ref, f_ref, xs_hbm, zero_ref, sem, zsem, rsem, bsem):
    bb = pl.program_id(0)
    i = pl.program_id(1)
    tm = f_ref.shape[0] // SUBLANES
    n_tok = pl.num_programs(0) * pl.num_programs(1) * tm
    base = (bb * pl.num_programs(1) + i) * tm

    def scatter(start):
        def body(r, carry):
            for kk in range(2):
                cp = _tile_copy(f_ref, r * SUBLANES, xs_hbm, dest_ref[kk * n_tok + base + r], sem)
                if start:
                    cp.start(priority=kk)
                else:
                    cp.wait()
            return carry
        lax.fori_loop(0, tm, body, 0, unroll=8)

    scatter(True)
    scatter(False)

    @pl.when(jnp.logical_and(bb == pl.num_programs(0) - 1, i == pl.num_programs(1) - 1))
    def _():
        zero_ref[...] = jnp.zeros_like(zero_ref)
        n_slots = xs_hbm.shape[0] // SUBLANES
        last = MOE_EXPERTS - 1

        blk_tiles = zero_ref.shape[0]

        def fill(start):
            def row(r, c2):
                cp = _tile_copy(zero_ref, 0, xs_hbm, r * SUBLANES, zsem)
                if start:
                    cp.start()
                else:
                    cp.wait()
                return c2

            def block(j, c2):
                cp = pltpu.make_async_copy(zero_ref, xs_hbm.at[pl.ds(pl.multiple_of(j * blk_tiles, blk_tiles),
                                                                     blk_tiles), :], bsem)
                if start:
                    cp.start()
                else:
                    cp.wait()
                return c2

            def run(g, c2):
                cp = pltpu.make_async_copy(
                    zero_ref.at[pl.ds(0, PAD_RUN * SUBLANES), :],
                    xs_hbm.at[pl.ds(pl.multiple_of(g * (PAD_RUN * SUBLANES), PAD_RUN * SUBLANES),
                                    PAD_RUN * SUBLANES), :], rsem)
                if start:
                    cp.start()
                else:
                    cp.wait()
                return c2

            def per_expert(e, carry):
                lo, hi = start_ref[e] + cnt_ref[e], start_ref[e] + padded_ref[e]
                mid = jnp.minimum((lo + PAD_RUN - 1) // PAD_RUN * PAD_RUN, hi)
                carry = lax.fori_loop(lo, mid, row, carry)
                return lax.fori_loop(mid // PAD_RUN, hi // PAD_RUN, run, carry)
            lax.fori_loop(0, MOE_EXPERTS, per_expert, 0)
            used_blocks = (start_ref[last] + padded_ref[last]) // (blk_tiles // SUBLANES)
            lax.fori_loop(used_blocks, n_slots * SUBLANES // blk_tiles, block, 0)

        fill(True)
        fill(False)


def _dispatch(dest8, counts, pad_start, padded, f3, n_slots):
    b, rows, _ = f3.shape
    n_t = rows // (TM * SUBLANES)
    return pl.pallas_call(
        _disp_kernel,
        grid_spec=pltpu.PrefetchScalarGridSpec(
            num_scalar_prefetch=4,
            grid=(b, n_t),
            in_specs=[pl.BlockSpec((TM * SUBLANES, LANES), lambda bb, i, *_: (bb * n_t + i, 0))],
            out_specs=pl.BlockSpec(memory_space=pl.ANY),
            scratch_shapes=[pltpu.VMEM((MOE_BLOCK * SUBLANES, LANES), F32), pltpu.SemaphoreType.DMA(()),
                            pltpu.SemaphoreType.DMA(()), pltpu.SemaphoreType.DMA(()),
                            pltpu.SemaphoreType.DMA(())],
        ),
        out_shape=jax.ShapeDtypeStruct((n_slots * SUBLANES, LANES), F32),
        compiler_params=_cparams(("arbitrary", "arbitrary")),
        name="moe_dispatch",
    )(dest8, counts, pad_start, padded, f3.reshape(b * rows, LANES))


def _moe_kernel(be_ref, nu_ref, xs_ref, w1_ref, w3_ref, w2_ref, y_ref, w1b, w3b, w2b):
    j = pl.program_id(0)
    n_used = nu_ref[0]
    blk = xs_ref.shape[0] // SUBLANES

    @pl.when(jnp.logical_and(j < n_used, jnp.logical_or(j == 0, be_ref[j] != be_ref[jnp.maximum(j - 1, 0)])))
    def _():
        w1b[...] = w1_ref[...].astype(BF16)
        w3b[...] = w3_ref[...].astype(BF16)
        w2b[...] = w2_ref[...].astype(BF16)

    @pl.when(j < n_used)
    def _():
        wide = 2 * LANES
        h1 = h3 = None
        for c in range(w1b.shape[0] // wide):
            xc = jnp.concatenate([xs_ref[pl.ds(2 * c + s, blk, stride=SUBLANES), :] for s in range(2)],
                                 axis=1).astype(BF16)
            d1 = _dot(xc, w1b[c * wide:(c + 1) * wide, :])
            d3 = _dot(xc, w3b[c * wide:(c + 1) * wide, :])
            h1, h3 = (d1, d3) if c == 0 else (h1 + d1, h3 + d3)
        hidden = (_silu(h1) * h3).astype(BF16)
        for c in range(w2b.shape[1] // wide):
            yc = _dot(hidden, w2b[:, c * wide:(c + 1) * wide])
            for s in range(2):
                y_ref[pl.ds(2 * c + s, blk, stride=SUBLANES), :] = yc[:, s * LANES:(s + 1) * LANES]

    @pl.when(j >= n_used)
    def _():
        y_ref[...] = jnp.zeros_like(y_ref)


def _moe(xs, block_e, n_used, w1, w3, w2, layer):
    n_blocks = block_e.shape[0]
    d, hid = w1.shape[-2:]
    wmap = lambda j, be, nu: (layer, be[j], 0, 0)
    tiles = MOE_BLOCK * SUBLANES
    return pl.pallas_call(
        _moe_kernel,
        grid_spec=pltpu.PrefetchScalarGridSpec(
            num_scalar_prefetch=2,
            grid=(n_blocks,),
            in_specs=[
                pl.BlockSpec((tiles, LANES), lambda j, be, nu: (jnp.minimum(j, jnp.maximum(nu[0] - 1, 0)), 0)),
                pl.BlockSpec((None, None, d, hid), wmap),
                pl.BlockSpec((None, None, d, hid), wmap),
                pl.BlockSpec((None, None, hid, d), wmap),
            ],
            out_specs=pl.BlockSpec((tiles, LANES), lambda j, be, nu: (j, 0)),
            scratch_shapes=[pltpu.VMEM((d, hid), BF16), pltpu.VMEM((d, hid), BF16), pltpu.VMEM((hid, d), BF16)],
        ),
        out_shape=jax.ShapeDtypeStruct((n_blocks * tiles, LANES), F32),
        compiler_params=_cparams(("arbitrary",)),
        name="moe_experts",
    )(block_e, n_used, xs, w1, w3, w2)


def _comb_kernel(dn_alpha, dest_ref, y_hbm, x1_ref, rw_ref, mod_ref, lng_ref, lnb_ref, o_ref, ybuf, sems):
    bb = pl.program_id(0)
    i = pl.program_id(1)
    tm = x1_ref.shape[0]
    n_steps = pl.num_programs(0) * pl.num_programs(1)
    n_tok = n_steps * tm
    step = bb * pl.num_programs(1) + i
    slot = step % 2

    def gather(tile, buf, start):
        def body(r, carry):
            for kk in range(2):
                cp = _tile_copy(y_hbm, dest_ref[kk * n_tok + tile * tm + r], ybuf.at[buf, kk], r * SUBLANES,
                                sems.at[buf, kk])
                if start:
                    cp.start(priority=kk)
                else:
                    cp.wait()
            return carry
        lax.fori_loop(0, tm, body, 0, unroll=8)

    @pl.when(step == 0)
    def _():
        gather(0, 0, True)

    @pl.when(step + 1 < n_steps)
    def _():
        gather(step + 1, 1 - slot, True)

    gather(step, slot, False)
    rw = rw_ref[...]
    y = (rw[:, 0:1] * _rows_from_tiles(ybuf.at[slot, 0], tm)
         + rw[:, 1:2] * _rows_from_tiles(ybuf.at[slot, 1], tm))
    gate2 = mod_ref[5:6, :]
    o_ref[...] = _ln(dn_alpha * x1_ref[...] + gate2 * y) * lng_ref[...] + lnb_ref[...]


def _combine(dest, y_slots, x1, rw, modtab, ln_g, ln_b, dn_alpha, seg_of_tile):
    b, n, d = x1.shape
    n_t = n // TM
    row = lambda bb, i, dr: (bb, i, 0)
    const2 = lambda bb, i, dr: (0, 0)
    return pl.pallas_call(
        functools.partial(_comb_kernel, dn_alpha),
        grid_spec=pltpu.PrefetchScalarGridSpec(
            num_scalar_prefetch=1,
            grid=(b, n_t),
            in_specs=[
                pl.BlockSpec(memory_space=pl.ANY),
                pl.BlockSpec((None, TM, d), row),
                pl.BlockSpec((None, TM, LANES), row),
                pl.BlockSpec((None, None, 6, d), lambda bb, i, dr: (bb, seg_of_tile(i), 0, 0)),
                pl.BlockSpec((1, d), const2), pl.BlockSpec((1, d), const2),
            ],
            out_specs=pl.BlockSpec((None, TM, d), row),
            scratch_shapes=[pltpu.VMEM((2, 2, TM * SUBLANES, LANES), F32), pltpu.SemaphoreType.DMA((2, 2))],
        ),
        out_shape=jax.ShapeDtypeStruct((b, n, d), F32),
        compiler_params=_cparams(("arbitrary", "arbitrary")),
        name="moe_combine",
    )(dest, y_slots, x1, rw, modtab, ln_g, ln_b)


def _dispatch_tables(info, tile_counts, n_blocks):
    b, n_t, _, tm = info.shape
    info = info.astype(jnp.int32)
    expert = jnp.moveaxis(info[:, :, 0:2, :], 2, 0).reshape(2, b * n_t, tm)
    rank = jnp.moveaxis(info[:, :, 2:4, :], 2, 0).reshape(2, b * n_t, tm)
    counts = jnp.sum(tile_counts, axis=0)
    padded = (counts + MOE_BLOCK - 1) // MOE_BLOCK * MOE_BLOCK
    pad_end = jnp.cumsum(padded)
    pad_start = pad_end - padded
    tile_base = pad_start[None, :] + jnp.cumsum(tile_counts, axis=0) - tile_counts
    onehot = expert[..., None] == jnp.arange(MOE_EXPERTS, dtype=jnp.int32)
    dest = jnp.sum(jnp.where(onehot, tile_base[None, :, None, :], 0), axis=-1) + rank
    dest8 = (dest * SUBLANES).reshape(-1).astype(jnp.int32)
    block_start = jnp.arange(n_blocks, dtype=jnp.int32) * MOE_BLOCK
    block_e = jnp.minimum(jnp.sum(pad_end[None, :] <= block_start[:, None], axis=1), MOE_EXPERTS - 1).astype(jnp.int32)
    n_used = (pad_end[-1] // MOE_BLOCK).astype(jnp.int32).reshape(1)
    i32 = lambda v: v.astype(jnp.int32)
    return dest8, i32(counts), i32(pad_start), i32(padded), block_e, n_used


def _pad_heads(w, width):
    lead = w.shape[:-1]
    w = w.reshape(*lead, HEADS, width)
    w = jnp.pad(w, [(0, 0)] * len(lead) + [(0, 0), (0, LANES - width)])
    return w.reshape(*lead, PAD_W)


def _expand_matrix(width):
    e = np.zeros((HEADS * width, PAD_W), np.float32)
    for h in range(HEADS):
        e[h * width + np.arange(width), h * LANES + np.arange(width)] = 1.0
    return jnp.asarray(e, BF16)


def _rope_tables(n_ctx, n_lat):
    n_freq = DA_DH // 4
    inv_freq = ROPE_BASE ** (-jnp.arange(n_freq, dtype=F32) / n_freq)
    pos = jnp.arange(n_lat, dtype=jnp.int32)
    ang_r = (pos // GRID_W).astype(F32)[:, None] * inv_freq
    ang_c = (pos % GRID_W).astype(F32)[:, None] * inv_freq
    ang = jnp.concatenate([ang_r, ang_r, ang_c, ang_c] * 2, axis=-1)
    cos, sin = jnp.cos(ang), jnp.sin(ang)
    first = (jnp.arange(LANES) % 32) < 16
    sin_a = jnp.where(first, -sin, 0.0)
    sin_b = jnp.where(first, 0.0, sin)
    ctx_rows = lambda v: jnp.full((n_ctx, LANES), v, F32)
    return jnp.stack([jnp.concatenate([ctx_rows(1.0), cos]), jnp.concatenate([ctx_rows(0.0), sin_a]),
                      jnp.concatenate([ctx_rows(0.0), sin_b])])


def kernel(x, c, ctx, c_ctx, w_ada, b_ada, w_in, da_lambda, da_norm, ml_conv_w, ml_conv_b, ml_ib, ml_fb,
           ml_norm, gl_wa, gl_ba, gl_norm, w_out, ln_mix_g, ln_mix_b, ln_ffn_g, ln_ffn_b,
           moe_wg, moe_we, moe_w1, moe_w3, moe_w2):
    b, s, d = x.shape
    n_ctx = ctx.shape[1]
    n = n_ctx + s
    depth = w_in.shape[0]
    assert n_ctx % TM == 0 and s % (TM * ATTN_NQ) == 0 and d % LANES == 0 and b % SCAN_NB == 0
    dn_alpha = (2 * depth) ** 0.25

    rows = -(-(b + 1) // SUBLANES) * SUBLANES
    cc = jnp.concatenate([c, c_ctx[None, :], jnp.zeros((rows - b - 1, d), F32)], axis=0)
    mod = _ada(cc, w_ada, b_ada).reshape(depth, rows, 6, d)

    x_ctx, x_lat, lat_off = ctx, x, 0
    rope = _rope_tables(n_ctx, s)
    e64, e32 = _expand_matrix(64), _expand_matrix(32)

    for l in range(depth):
        last = l == depth - 1
        tile0 = n_ctx // TM if last else 0
        lam_init = 0.8 - 0.6 * math.exp(-0.3 * l)
        modtab = jnp.stack([jnp.broadcast_to(mod[l, b], (b, 6, d)), mod[l, :b]], axis=1)

        wi = w_in[l]
        o = np.cumsum([0, 512, 512, 512, 512, 256, 256, 8, 8, 128, 128, 256, 256, 32])
        misc_w = jnp.concatenate([wi[:, o[6]:o[8]], wi[:, o[12]:o[13]],
                                  jnp.zeros((d, LANES - 16 - 2 * GL_RANK), F32)], axis=1)
        w_in_p = jnp.concatenate([wi[:, :o[6]], wi[:, o[8]:o[12]], misc_w], axis=1).astype(BF16)

        (qa, ka, va, mq, mk, mv, mo, gq, gk, gv, gr, misc) = _in_proj(
            x_ctx, x_lat, lat_off, s, modtab, w_in_p, rope, ml_conv_w[l], ml_conv_b[l].reshape(1, -1), e64, e32, n_ctx)

        n_ctx_tiles = n_ctx // TM
        attend = functools.partial(_attention, qa, ka, va, da_lambda[l], da_norm[l].reshape(1, -1), lam_init)
        da_ctx = None if last else attend(0, n_ctx_tiles, n_ctx, 1)
        da_lat = attend(n_ctx_tiles, s // TM, n, ATTN_NQ)

        gate_bias = jnp.concatenate([ml_ib[l].reshape(-1), ml_fb[l].reshape(-1),
                                     jnp.zeros((LANES - 16,), F32)]).reshape(1, LANES)
        mhf, mhb = _mlstm(mq, mk, mv, misc, gate_bias, n_ctx)

        wa_p = jnp.stack([jnp.pad(_pad_heads(gl_wa[l, dd], GL_DK),
                                  ((M_GA + dd * GL_RANK, LANES - M_GA - (dd + 1) * GL_RANK), (0, 0)))
                          for dd in range(2)])
        ba_p = _pad_heads(gl_ba[l], GL_DK).reshape(2, 1, PAD_W)
        ghf, ghb = _gla(gq, gk, gv, misc, wa_p.astype(BF16), ba_p, n_ctx)

        wo = w_out[l]
        wo_p = jnp.concatenate([
            wo[:512],
            jnp.pad(wo[512:768].reshape(HEADS, 64, d), ((0, 0), (0, 64), (0, 0))).reshape(PAD_W, d),
            jnp.pad(wo[768:1024].reshape(HEADS, 64, d), ((0, 0), (0, 64), (0, 0))).reshape(PAD_W, d),
        ], axis=0).astype(BF16)
        w_route = jnp.concatenate([moe_wg[l], moe_we[l],
                                   jnp.zeros((d, LANES - MOE_GROUPS - MOE_EXPERTS), F32)], axis=1)
        w_route_top = _bf16_part(w_route)
        w_route = jnp.stack([w_route_top.astype(BF16), (w_route - w_route_top).astype(BF16)])
        ml_g = _pad_heads(jnp.tile(ml_norm[l], HEADS), ML_DH).reshape(1, PAD_W)
        gl_g = _pad_heads(jnp.tile(gl_norm[l], HEADS), GL_DV).reshape(1, PAD_W)
        x1, f3, info, cnt, rw = _out_proj(x_ctx, x_lat, lat_off, n, da_ctx, da_lat, mhf, mhb, mo, ghf, ghb, gr, modtab,
                                          ln_mix_g[l].reshape(1, d), ln_mix_b[l].reshape(1, d), ml_g, gl_g,
                                          wo_p, w_route, dn_alpha, n_ctx, tile0)

        n_rows = x1.shape[1]
        n_t = n_rows // TM
        a_all = b * n_rows * 2
        n_blocks = -(-(a_all + MOE_EXPERTS * (MOE_BLOCK - 1)) // MOE_BLOCK)
        tile_counts = cnt.reshape(b * n_t, SUBLANES, LANES)[:, 0, MOE_GROUPS:MOE_GROUPS + MOE_EXPERTS]
        dest8, counts, pad_start, padded, block_e, n_used = _dispatch_tables(
            info.reshape(b, n_t, SUBLANES, TM), tile_counts.astype(jnp.int32), n_blocks)
        xs = _dispatch(dest8, counts, pad_start, padded, f3, n_blocks * MOE_BLOCK)
        y_slots = _moe(xs, block_e, n_used, moe_w1, moe_w3, moe_w2, l)
        n_ctx_tiles_here = 0 if last else n_ctx // TM
        seg = lambda i: jnp.where(i >= n_ctx_tiles_here, 1, 0)
        x_new = _combine(dest8, y_slots, x1, rw, modtab, ln_ffn_g[l].reshape(1, d), ln_ffn_b[l].reshape(1, d),
                         dn_alpha, seg)
        x_ctx, x_lat, lat_off = x_new, x_new, n_ctx // TM
    return x_new
```
